```python
import math
import jax, jax.numpy as jnp
from jax import lax
import numpy as np

D_MODEL = 2048
BATCH = 4
SEQ = 2048
DEPTH = 4

N_MIXERS = 3
MIX_WIDTH = 3 * D_MODEL // 4
MEM_WIDTH = D_MODEL - MIX_WIDTH
MEM_LEN = 256
MEM_HEADS = 4
MEM_HEAD_DIM = MEM_WIDTH // MEM_HEADS

POOL_WINDOWS = (2, 4, 8, 16)
POOL_GROUPS = 4
POOL_GROUP_DIM = MIX_WIDTH // POOL_GROUPS

DIFF_HEAD_DIM = 128
DIFF_HEADS = MIX_WIDTH // (2 * DIFF_HEAD_DIM)
DIFF_V_DIM = 2 * DIFF_HEAD_DIM
ROPE_THETA = 10000.0
Q_BLOCK = 128

SSD_HEAD_DIM = 64
SSD_HEADS = MIX_WIDTH // SSD_HEAD_DIM
SSD_GROUPS = 4
SSD_HEADS_PER_GROUP = SSD_HEADS // SSD_GROUPS
SSD_STATE = 128
SSD_CONV = 4
SSD_CHUNK = 128
SSD_CONV_DIM = MIX_WIDTH + 2 * SSD_GROUPS * SSD_STATE

N_EXPERT_GROUPS = 4
EXPERTS_PER_GROUP = 4
N_EXPERTS = N_EXPERT_GROUPS * EXPERTS_PER_GROUP
TOP_K = 2
D_EXPERT = D_MODEL // 4

RMS_EPS = 1e-6

POOL_IN = MIX_WIDTH + MEM_WIDTH
DIFF_IN = 3 * MIX_WIDTH + MEM_WIDTH
SSD_MIX_IN = MIX_WIDTH + SSD_CONV_DIM + SSD_HEADS
SSD_IN = SSD_MIX_IN + MEM_WIDTH

N_POOL_LAYERS = (DEPTH + 2) // 3
N_DIFF_LAYERS = (DEPTH + 1) // 3
N_SSD_LAYERS = DEPTH // 3

kernel_name = 'hybrid_pool_diffattn_ssd_hmoe'


def rmsnorm(x, g):
    xf = x.astype(jnp.float32)
    y = xf * lax.rsqrt(jnp.mean(xf * xf, axis=-1, keepdims=True) + RMS_EPS)
    return (y * g.astype(jnp.float32)).astype(x.dtype)


def rope_tables(seq, dim):
    pos = jnp.arange(seq, dtype=jnp.float32)
    inv = ROPE_THETA ** (-jnp.arange(0, dim, 2, dtype=jnp.float32) / dim)
    ang = pos[:, None] * inv[None, :]
    ang = jnp.concatenate([ang, ang], axis=-1)
    return jnp.cos(ang), jnp.sin(ang)


def apply_rope(x, cos, sin):
    s, dim = x.shape[1], x.shape[-1]
    shape = (1, s) + (1,) * (x.ndim - 3) + (dim,)
    c, sn = cos.reshape(shape), sin.reshape(shape)
    xf = x.astype(jnp.float32)
    x1, x2 = xf[..., : dim // 2], xf[..., dim // 2:]
    rot = jnp.concatenate([-x2, x1], axis=-1)
    return (xf * c + rot * sn).astype(x.dtype)


def pool_mixer(u, w_grp, scale):
    b, s, _ = u.shape
    ug = u.reshape(b, s, POOL_GROUPS, POOL_GROUP_DIM)
    cs = jnp.cumsum(ug.astype(jnp.float32), axis=1)
    cs = jnp.concatenate([jnp.zeros_like(cs[:, :1]), cs], axis=1)
    t = jnp.arange(s)[:, None]
    win = jnp.array(POOL_WINDOWS, dtype=jnp.int32)[None, :]
    lo = jnp.maximum(t + 1 - win, 0)
    cnt = jnp.minimum(t + 1, win).astype(jnp.float32)
    g_idx = jnp.arange(POOL_GROUPS)[None, :]
    lower = cs[:, lo, g_idx]
    pooled = (cs[:, 1:] - lower) / cnt[None, :, :, None]
    mixed = (pooled - ug.astype(jnp.float32)).astype(u.dtype)
    out = jnp.einsum('bsgc,gcd->bsgd', mixed, w_grp).reshape(b, s, MIX_WIDTH)
    return out * scale


def diff_attention(proj, lam, subln, lambda_init, cos, sin):
    b, s, _ = proj.shape
    q = proj[..., :MIX_WIDTH].reshape(b, s, DIFF_HEADS, 2, DIFF_HEAD_DIM)
    k = proj[..., MIX_WIDTH:2 * MIX_WIDTH].reshape(b, s, DIFF_HEADS, 2, DIFF_HEAD_DIM)
    v = proj[..., 2 * MIX_WIDTH:3 * MIX_WIDTH].reshape(b, s, DIFF_HEADS, DIFF_V_DIM)
    q = apply_rope(q, cos, sin) * (DIFF_HEAD_DIM ** -0.5)
    k = apply_rope(k, cos, sin)
    lf = lam.astype(jnp.float32)
    lmbda = jnp.exp(jnp.sum(lf[0] * lf[1])) - jnp.exp(jnp.sum(lf[2] * lf[3])) + lambda_init
    nb = s // Q_BLOCK
    qb = q.reshape(b, nb, Q_BLOCK, DIFF_HEADS, 2, DIFF_HEAD_DIM).transpose(1, 0, 2, 3, 4, 5)
    k_pos = jnp.arange(s)

    def block(args):
        q_blk, i = args
        q_pos = i * Q_BLOCK + jnp.arange(Q_BLOCK)
        sc = jnp.einsum('bqhcd,bkhcd->bhcqk', q_blk, k).astype(jnp.float32)
        mask = k_pos[None, :] <= q_pos[:, None]
        sc = jnp.where(mask, sc, -jnp.inf)
        p = jax.nn.softmax(sc, axis=-1)
        a = p[:, :, 0] - lmbda * p[:, :, 1]
        return jnp.einsum('bhqk,bkhv->bqhv', a.astype(v.dtype), v)

    o = lax.map(block, (qb, jnp.arange(nb)))
    o = o.transpose(1, 0, 2, 3, 4).reshape(b, s, DIFF_HEADS, DIFF_V_DIM)
    o = rmsnorm(o, subln) * (1.0 - lambda_init)
    return o.reshape(b, s, MIX_WIDTH)


def causal_dwconv(u, w, bias):
    y = lax.conv_general_dilated(u, w[:, None, :], window_strides=(1,),
                                 padding=[(SSD_CONV - 1, 0)],
                                 dimension_numbers=('NWC', 'WIO', 'NWC'),
                                 feature_group_count=u.shape[-1])
    return y + bias


def ssd_scan(x, dt, a, bm, cm):
    b, s, g, hg, p = x.shape
    n = bm.shape[-1]
    nc = s // SSD_CHUNK
    dtg = dt.reshape(b, s, g, hg)
    xc = (x.astype(jnp.float32) * dtg[..., None]).reshape(b, nc, SSD_CHUNK, g, hg, p)
    bc = bm.astype(jnp.float32).reshape(b, nc, SSD_CHUNK, g, n)
    cc = cm.astype(jnp.float32).reshape(b, nc, SSD_CHUNK, g, n)
    ac = (dtg * a.reshape(g, hg)).reshape(b, nc, SSD_CHUNK, g, hg).transpose(0, 3, 4, 1, 2)
    acs = jnp.cumsum(ac, axis=-1)
    seg = acs[..., :, None] - acs[..., None, :]
    causal = jnp.tril(jnp.ones((SSD_CHUNK, SSD_CHUNK), dtype=bool))
    decay = jnp.exp(jnp.where(causal, seg, -jnp.inf))
    scores = jnp.einsum('bclgn,bcsgn->bgcls', cc, bc)
    y_diag = jnp.einsum('bghcls,bcsghp->bclghp', scores[:, :, None] * decay, xc)
    decay_states = jnp.exp(acs[..., -1:] - acs)
    states = jnp.einsum('bcsgn,bghcs,bcsghp->bcghpn', bc, decay_states, xc)
    chunk_decay = jnp.exp(acs[..., -1])

    def step(h, inp):
        st, dec = inp
        return h * dec[..., None, None] + st, h

    h0 = jnp.zeros((b, g, hg, p, n), jnp.float32)
    _, prev = lax.scan(step, h0, (states.transpose(1, 0, 2, 3, 4, 5), chunk_decay.transpose(3, 0, 1, 2)))
    prev = prev.transpose(1, 0, 2, 3, 4, 5)
    y_off = jnp.einsum('bclgn,bcghpn,bghcl->bclghp', cc, prev, jnp.exp(acs))
    return (y_diag + y_off).reshape(b, s, g, hg, p)


def ssd_mixer(proj, conv_w, conv_b, dt_bias, a_log, d_skip, norm_g):
    b, s, _ = proj.shape
    z = proj[..., :MIX_WIDTH]
    xbc = jax.nn.silu(causal_dwconv(proj[..., MIX_WIDTH:MIX_WIDTH + SSD_CONV_DIM], conv_w, conv_b))
    dt_raw = proj[..., MIX_WIDTH + SSD_CONV_DIM:SSD_MIX_IN]
    xs = xbc[..., :MIX_WIDTH].reshape(b, s, SSD_GROUPS, SSD_HEADS_PER_GROUP, SSD_HEAD_DIM)
    bm = xbc[..., MIX_WIDTH:MIX_WIDTH + SSD_GROUPS * SSD_STATE].reshape(b, s, SSD_GROUPS, SSD_STATE)
    cm = xbc[..., MIX_WIDTH + SSD_GROUPS * SSD_STATE:].reshape(b, s, SSD_GROUPS, SSD_STATE)
    dt = jax.nn.softplus(dt_raw.astype(jnp.float32) + dt_bias.astype(jnp.float32))
    a = -jnp.exp(a_log.astype(jnp.float32))
    y = ssd_scan(xs, dt, a, bm, cm)
    y = y + d_skip.astype(jnp.float32).reshape(SSD_GROUPS, SSD_HEADS_PER_GROUP)[..., None] * xs.astype(jnp.float32)
    y = (y.reshape(b, s, MIX_WIDTH) * jax.nn.silu(z.astype(jnp.float32))).astype(proj.dtype)
    y = rmsnorm(y.reshape(b, s, SSD_GROUPS, -1), norm_g.reshape(SSD_GROUPS, -1))
    return y.reshape(b, s, MIX_WIDTH)


def memory_attention(q_proj, mem_kv):
    b, s, _ = q_proj.shape
    m = mem_kv.shape[1]
    q = q_proj.reshape(b, s, MEM_HEADS, MEM_HEAD_DIM) * (MEM_HEAD_DIM ** -0.5)
    k = mem_kv[..., :MEM_WIDTH].reshape(b, m, MEM_HEADS, MEM_HEAD_DIM)
    v = mem_kv[..., MEM_WIDTH:].reshape(b, m, MEM_HEADS, MEM_HEAD_DIM)
    p = jax.nn.softmax(jnp.einsum('bshd,bmhd->bhsm', q, k).astype(jnp.float32), axis=-1)
    o = jnp.einsum('bhsm,bmhd->bshd', p.astype(v.dtype), v)
    return o.reshape(b, s, MEM_WIDTH)


def hier_moe(h, w_group, b_group, w_expert, b_expert, w_gate, w_up, w_down):
    b, s, d = h.shape
    t = h.reshape(b * s, d)
    g_logits = (t @ w_group).astype(jnp.float32) + b_group.astype(jnp.float32)
    g_prob = jax.nn.softmax(g_logits, axis=-1)
    g_sel = jnp.argmax(g_logits, axis=-1)
    g_w = jnp.take_along_axis(g_prob, g_sel[:, None], axis=-1)
    e_logits = ((t @ w_expert).astype(jnp.float32) + b_expert.astype(jnp.float32)).reshape(-1, N_EXPERT_GROUPS, EXPERTS_PER_GROUP)
    e_in = jnp.take_along_axis(e_logits, g_sel[:, None, None], axis=1)[:, 0]
    e_prob = jax.nn.softmax(e_in, axis=-1)
    top_w, top_i = lax.top_k(e_prob, TOP_K)
    top_w = top_w / jnp.sum(top_w, axis=-1, keepdims=True) * g_w
    expert_idx = g_sel[:, None] * EXPERTS_PER_GROUP + top_i
    gates = jnp.sum(jax.nn.one_hot(expert_idx, N_EXPERTS, dtype=jnp.float32) * top_w[..., None], axis=1)
    hid = jax.nn.silu(jnp.einsum('td,edf->tef', t, w_gate)) * jnp.einsum('td,edf->tef', t, w_up)
    hid = (hid * gates[..., None]).astype(h.dtype)
    y = jnp.einsum('tef,efd->td', hid, w_down)
    return y.reshape(b, s, d).astype(h.dtype)


def lambda_init_fn(depth_idx):
    return 0.8 - 0.6 * math.exp(-0.3 * depth_idx)


def setup_inputs(seed: int = 0) -> dict:
    key = jax.random.key(seed)
    ks = jax.random.split(key, 32)

    def nrm(k, shape, scale):
        return jax.random.normal(k, shape, jnp.float32) * scale

    resid = (2.0 * DEPTH) ** -0.5
    dt0 = jnp.exp(jax.random.uniform(ks[19], (N_SSD_LAYERS, SSD_HEADS), jnp.float32, math.log(1e-3), math.log(1e-1)))
    return {
        'x': nrm(ks[0], (BATCH, SEQ, D_MODEL), 1.0),
        'mem': nrm(ks[1], (BATCH, MEM_LEN, D_MODEL), 1.0),
        'norm_mix': 1.0 + nrm(ks[2], (DEPTH, D_MODEL), 0.02),
        'norm_mem': 1.0 + nrm(ks[3], (DEPTH, D_MODEL), 0.02),
        'norm_ffn': 1.0 + nrm(ks[4], (DEPTH, D_MODEL), 0.02),
        'norm_final': 1.0 + nrm(ks[5], (D_MODEL,), 0.02),
        'w_out': nrm(ks[6], (DEPTH, D_MODEL, D_MODEL), D_MODEL ** -0.5 * resid),
        'w_mem_kv': nrm(ks[7], (DEPTH, D_MODEL, 2 * MEM_WIDTH), D_MODEL ** -0.5),
        'pool_w_in': nrm(ks[8], (N_POOL_LAYERS, D_MODEL, POOL_IN), D_MODEL ** -0.5),
        'pool_w_grp': nrm(ks[9], (N_POOL_LAYERS, POOL_GROUPS, POOL_GROUP_DIM, POOL_GROUP_DIM), POOL_GROUP_DIM ** -0.5),
        'pool_scale': 1.0 + nrm(ks[10], (N_POOL_LAYERS, MIX_WIDTH), 0.02),
        'diff_w_in': nrm(ks[11], (N_DIFF_LAYERS, D_MODEL, DIFF_IN), D_MODEL ** -0.5),
        'diff_lambda': nrm(ks[12], (N_DIFF_LAYERS, 4, DIFF_HEAD_DIM), 0.1),
        'diff_subln': 1.0 + nrm(ks[13], (N_DIFF_LAYERS, DIFF_V_DIM), 0.02),
        'ssd_w_in': nrm(ks[14], (N_SSD_LAYERS, D_MODEL, SSD_IN), D_MODEL ** -0.5),
        'ssd_conv_w': nrm(ks[15], (N_SSD_LAYERS, SSD_CONV, SSD_CONV_DIM), SSD_CONV ** -0.5),
        'ssd_conv_b': nrm(ks[16], (N_SSD_LAYERS, SSD_CONV_DIM), 0.01),
        'ssd_dt_bias': dt0 + jnp.log(-jnp.expm1(-dt0)),
        'ssd_a_log': jnp.log(jax.random.uniform(ks[17], (N_SSD_LAYERS, SSD_HEADS), jnp.float32, 1.0, 16.0)),
        'ssd_d': 1.0 + nrm(ks[18], (N_SSD_LAYERS, SSD_HEADS), 0.1),
        'ssd_norm': 1.0 + nrm(ks[20], (N_SSD_LAYERS, MIX_WIDTH), 0.02),
        'moe_w_group': nrm(ks[21], (DEPTH, D_MODEL, N_EXPERT_GROUPS), D_MODEL ** -0.5),
        'moe_b_group': nrm(ks[22], (DEPTH, N_EXPERT_GROUPS), 0.01),
        'moe_w_expert': nrm(ks[23], (DEPTH, D_MODEL, N_EXPERTS), D_MODEL ** -0.5),
        'moe_b_expert': nrm(ks[24], (DEPTH, N_EXPERTS), 0.01),
        'moe_w_gate': nrm(ks[25], (DEPTH, N_EXPERTS, D_MODEL, D_EXPERT), D_MODEL ** -0.5),
        'moe_w_up': nrm(ks[26], (DEPTH, N_EXPERTS, D_MODEL, D_EXPERT), D_MODEL ** -0.5),
        'moe_w_down': nrm(ks[27], (DEPTH, N_EXPERTS, D_EXPERT, D_MODEL), D_EXPERT ** -0.5 * resid),
    }


def reference(x, mem, norm_mix, norm_mem, norm_ffn, norm_final, w_out, w_mem_kv,
              pool_w_in, pool_w_grp, pool_scale, diff_w_in, diff_lambda, diff_subln,
              ssd_w_in, ssd_conv_w, ssd_conv_b, ssd_dt_bias, ssd_a_log, ssd_d, ssd_norm,
              moe_w_group, moe_b_group, moe_w_expert, moe_b_expert, moe_w_gate, moe_w_up, moe_w_down):
    s = x.shape[1]
    cos, sin = rope_tables(s, DIFF_HEAD_DIM)
    for i in range(DEPTH):
        kind, slot = i % N_MIXERS, i // N_MIXERS
        h = rmsnorm(x, norm_mix[i])
        mem_kv = rmsnorm(mem, norm_mem[i]) @ w_mem_kv[i]
        if kind == 0:
            proj = h @ pool_w_in[slot]
            mix = pool_mixer(proj[..., :MIX_WIDTH], pool_w_grp[slot], pool_scale[slot])
        elif kind == 1:
            proj = h @ diff_w_in[slot]
            mix = diff_attention(proj[..., :3 * MIX_WIDTH], diff_lambda[slot], diff_subln[slot],
                                 lambda_init_fn(i), cos, sin)
        else:
            proj = h @ ssd_w_in[slot]
            mix = ssd_mixer(proj[..., :SSD_MIX_IN], ssd_conv_w[slot], ssd_conv_b[slot], ssd_dt_bias[slot],
                            ssd_a_log[slot], ssd_d[slot], ssd_norm[slot])
        mem_out = memory_attention(proj[..., -MEM_WIDTH:], mem_kv)
        x = x + jnp.concatenate([mix, mem_out], axis=-1) @ w_out[i]
        x = x + hier_moe(rmsnorm(x, norm_ffn[i]), moe_w_group[i], moe_b_group[i], moe_w_expert[i],
                         moe_b_expert[i], moe_w_gate[i], moe_w_up[i], moe_w_down[i])
    return rmsnorm(x, norm_final)
```

```python
import functools
import math

import jax
import jax.numpy as jnp
from jax import lax
from jax.experimental import pallas as pl
from jax.experimental.pallas import tpu as pltpu

F32 = jnp.float32
BF16 = jnp.bfloat16

D_MODEL = 2048
DEPTH = 4
N_MIXERS = 3
MIX_WIDTH = 1536
MEM_WIDTH = 512
MEM_HEADS = 4
MEM_HEAD_DIM = 128
POOL_WINDOWS = (2, 4, 8, 16)
POOL_GROUPS = 4
POOL_GROUP_DIM = 384
DIFF_HEAD_DIM = 128
DIFF_HEADS = 6
DIFF_V_DIM = 256
ROPE_THETA = 10000.0
SSD_HEAD_DIM = 64
SSD_HEADS = 24
SSD_GROUPS = 4
SSD_HEADS_PER_GROUP = 6
SSD_STATE = 128
SSD_CONV = 4
SSD_CHUNK = 128
SSD_CONV_DIM = 2560
SSD_MIX_IN = 4120
SSD_IN_PADDED = 5120
N_EXPERT_GROUPS = 4
EXPERTS_PER_GROUP = 4
N_EXPERTS = 16
D_EXPERT = 512
RMS_EPS = 1e-6

MOE_TILE = 256
GATHER_CHUNK = 128
MIB = 1024 * 1024

NT_DIMS = (((1,), (1,)), ((), ()))
TN_DIMS = (((0,), (0,)), ((), ()))


def _cp(sem, vmem_mib):
    return pltpu.CompilerParams(dimension_semantics=sem, vmem_limit_bytes=vmem_mib * MIB)


def _sigmoid(x):
    return 1.0 / (1.0 + jnp.exp(-x))


def _softplus(x):
    return jnp.maximum(x, 0.0) + jnp.log1p(jnp.exp(-jnp.abs(x)))


def _split3(v):
    hi = v.astype(BF16)
    r = v - hi.astype(F32)
    mid = r.astype(BF16)
    lo = (r - mid.astype(F32)).astype(BF16)
    return hi, mid, lo


def _norm_matmul_kernel(x_ref, g_ref, w_ref, o_ref, h_ref):
    @pl.when(pl.program_id(1) == 0)
    def _():
        x = x_ref[...]
        ms = jnp.mean(x * x, axis=-1, keepdims=True)
        h_ref[...] = (x * lax.rsqrt(ms + RMS_EPS) * g_ref[...]).astype(BF16)

    o_ref[...] = jnp.dot(h_ref[...], w_ref[...].astype(BF16), preferred_element_type=F32)


def norm_matmul(x, g, w, li, n_cols, tm, tn):
    m, k = x.shape
    return pl.pallas_call(
        _norm_matmul_kernel,
        grid=(m // tm, n_cols // tn),
        in_specs=[
            pl.BlockSpec((tm, k), lambda i, j: (i, 0)),
            pl.BlockSpec((1, k), lambda i, j: (0, 0)),
            pl.BlockSpec((None, k, tn), lambda i, j: (li, 0, j)),
        ],
        out_specs=pl.BlockSpec((tm, tn), lambda i, j: (i, j)),
        out_shape=jax.ShapeDtypeStruct((m, n_cols), F32),
        scratch_shapes=[pltpu.VMEM((tm, k), BF16)],
        compiler_params=_cp(("parallel", "arbitrary"), 48),
        name="norm_matmul",
    )(x, g.reshape(1, k), w)


def _pool_kernel(u_ref, w_ref, sc_ref, o_ref, pad_ref):
    grp = pl.program_id(1)
    s, c = u_ref.shape[1], u_ref.shape[2]
    rows = 256
    pad_ref[0:16, :] = jnp.zeros((16, c), F32)
    pad_ref[16:, :] = u_ref[0]
    wb = w_ref[...].astype(BF16)
    sc = sc_ref[0]

    for gi, win in enumerate(POOL_WINDOWS):
        @pl.when(grp == gi)
        def _(win=win):
            for r in range(s // rows):
                xh = pad_ref[r * rows:r * rows + rows + 16, :]
                acc = xh
                k = 1
                while k < win:
                    acc = acc + pltpu.roll(acc, k, axis=0)
                    k *= 2
                t = r * rows + lax.broadcasted_iota(jnp.int32, (rows, 1), 0)
                cnt = jnp.minimum(t + 1, win).astype(F32)
                mixed = (acc[16:, :] / cnt - xh[16:, :]).astype(BF16)
                o_ref[0, r * rows:(r + 1) * rows, :] = (
                    jnp.dot(mixed, wb, preferred_element_type=F32) * sc)


def pool_mixer(proj3, w_grp, li, scale):
    b, s, _ = proj3.shape
    c = POOL_GROUP_DIM
    return pl.pallas_call(
        _pool_kernel,
        grid=(b, POOL_GROUPS),
        in_specs=[
            pl.BlockSpec((1, s, c), lambda i, g: (i, 0, g)),
            pl.BlockSpec((None, None, c, c), lambda i, g: (li, g, 0, 0)),
            pl.BlockSpec((1, 1, c), lambda i, g: (g, 0, 0)),
        ],
        out_specs=pl.BlockSpec((1, s, c), lambda i, g: (i, 0, g)),
        out_shape=jax.ShapeDtypeStruct((b, s, MIX_WIDTH), F32),
        scratch_shapes=[pltpu.VMEM((s + 16, c), F32)],
        compiler_params=_cp(("parallel", "parallel"), 40),
        name="pool_mixer",
    )(proj3, w_grp, scale.reshape(POOL_GROUPS, 1, c))


def _rope_kernel(q_ref, k_ref, v_ref, cos_ref, sin_ref, qo_ref, ko_ref, vo_ref):
    cos = cos_ref[...]
    sin = sin_ref[...]
    half = DIFF_HEAD_DIM // 2
    for c in range(MIX_WIDTH // DIFF_HEAD_DIM):
        sl = slice(c * DIFF_HEAD_DIM, (c + 1) * DIFF_HEAD_DIM)
        x = q_ref[0, :, sl]
        r = x * cos + pltpu.roll(x, half, axis=1) * sin
        qo_ref[0, :, sl] = (r * (DIFF_HEAD_DIM ** -0.5)).astype(BF16)
        x = k_ref[0, :, sl]
        ko_ref[0, :, sl] = (x * cos + pltpu.roll(x, half, axis=1) * sin).astype(BF16)
    vo_ref[...] = v_ref[...].astype(BF16)


def rope_qkv(proj3, cos, sin_signed):
    b, s, _ = proj3.shape
    ts = 256
    w = MIX_WIDTH
    out = jax.ShapeDtypeStruct((b, s, w), BF16)
    return pl.pallas_call(
        _rope_kernel,
        grid=(b, s // ts),
        in_specs=[
            pl.BlockSpec((1, ts, w), lambda i, j: (i, j, 0)),
            pl.BlockSpec((1, ts, w), lambda i, j: (i, j, 1)),
            pl.BlockSpec((1, ts, w), lambda i, j: (i, j, 2)),
            pl.BlockSpec((ts, DIFF_HEAD_DIM), lambda i, j: (j, 0)),
            pl.BlockSpec((ts, DIFF_HEAD_DIM), lambda i, j: (j, 0)),
        ],
        out_specs=[pl.BlockSpec((1, ts, w), lambda i, j: (i, j, 0))] * 3,
        out_shape=[out, out, out],
        compiler_params=_cp(("parallel", "parallel"), 40),
        name="rope_qkv",
    )(proj3, proj3, proj3, cos, sin_signed)


def _diff_attn_kernel(lam_ref, sub_ref, q_ref, k_ref, v_ref, o_ref, *, lambda_init):
    lam = lam_ref[...]
    s1 = jnp.sum(lam[0:1] * lam[1:2], axis=-1, keepdims=True)
    s2 = jnp.sum(lam[2:3] * lam[3:4], axis=-1, keepdims=True)
    lmbda = jnp.exp(s1) - jnp.exp(s2) + lambda_init
    s = q_ref.shape[1]
    tq = 256
    d = DIFF_HEAD_DIM
    for i in range(s // tq):
        kv = (i + 1) * tq
        q = q_ref[0, i * tq:(i + 1) * tq, :]
        row = i * tq + lax.broadcasted_iota(jnp.int32, (tq, kv), 0)
        col = lax.broadcasted_iota(jnp.int32, (tq, kv), 1)
        mask = col <= row
        probs = []
        for c in range(2):
            sc = lax.dot_general(q[:, c * d:(c + 1) * d], k_ref[0, 0:kv, c * d:(c + 1) * d],
                                 NT_DIMS, preferred_element_type=F32)
            sc = jnp.where(mask, sc, -jnp.inf)
            e = jnp.exp(sc - jnp.max(sc, axis=-1, keepdims=True))
            probs.append(e / jnp.sum(e, axis=-1, keepdims=True))
        a = (probs[0] - lmbda * probs[1]).astype(BF16)
        o = jnp.dot(a, v_ref[0, 0:kv, :], preferred_element_type=F32)
        ms = jnp.mean(o * o, axis=-1, keepdims=True)
        o_ref[0, i * tq:(i + 1) * tq, :] = (
            o * lax.rsqrt(ms + RMS_EPS) * sub_ref[...] * (1.0 - lambda_init))


def diff_attention(q, k, v, lam, subln, lambda_init):
    b, s, _ = q.shape
    vd = DIFF_V_DIM
    blk = pl.BlockSpec((1, s, vd), lambda i, h: (i, 0, h))
    return pl.pallas_call(
        functools.partial(_diff_attn_kernel, lambda_init=lambda_init),
        grid=(b, DIFF_HEADS),
        in_specs=[
            pl.BlockSpec((4, DIFF_HEAD_DIM), lambda i, h: (0, 0)),
            pl.BlockSpec((1, vd), lambda i, h: (0, 0)),
            blk, blk, blk,
        ],
        out_specs=blk,
        out_shape=jax.ShapeDtypeStruct((b, s, MIX_WIDTH), F32),
        compiler_params=_cp(("parallel", "parallel"), 48),
        name="diff_attention",
    )(lam, subln.reshape(1, vd), q, k, v)


def _conv_kernel(u_ref, w_ref, b_ref, o_ref, pad_ref):
    s, c = u_ref.shape[1], u_ref.shape[2]
    rows = 256
    pad_ref[0:8, :] = jnp.zeros((8, c), F32)
    pad_ref[8:, :] = u_ref[0]
    w = w_ref[...]
    bias = b_ref[...]
    for r in range(s // rows):
        xh = pad_ref[r * rows:r * rows + rows + 8, :]
        y = xh * w[3:4]
        for j in range(1, SSD_CONV):
            y = y + pltpu.roll(xh, j, axis=0) * w[SSD_CONV - 1 - j:SSD_CONV - j]
        y = y[8:, :] + bias
        o_ref[0, r * rows:(r + 1) * rows, :] = y * _sigmoid(y)


def ssd_conv(proj3, conv_w, conv_b):
    b, s, _ = proj3.shape
    tc = 512
    off = MIX_WIDTH // tc
    return pl.pallas_call(
        _conv_kernel,
        grid=(b, SSD_CONV_DIM // tc),
        in_specs=[
            pl.BlockSpec((1, s, tc), lambda i, j: (i, 0, off + j)),
            pl.BlockSpec((SSD_CONV, tc), lambda i, j: (0, j)),
            pl.BlockSpec((1, tc), lambda i, j: (0, j)),
        ],
        out_specs=pl.BlockSpec((1, s, tc), lambda i, j: (i, 0, j)),
        out_shape=jax.ShapeDtypeStruct((b, s, SSD_CONV_DIM), F32),
        scratch_shapes=[pltpu.VMEM((s + 8, tc), F32)],
        compiler_params=_cp(("parallel", "parallel"), 48),
        name="ssd_conv",
    )(proj3, conv_w, conv_b.reshape(1, SSD_CONV_DIM))


def _ssd_kernel(x_ref, b_ref, c_ref, z_ref, dtc_ref, dtr_ref, bc_ref, br_ref, ac_ref, ar_ref,
                d_ref, ng_ref, o_ref, state_ref, y_ref):
    @pl.when(pl.program_id(2) == 0)
    def _():
        state_ref[...] = jnp.zeros(state_ref.shape, F32)

    ln = SSD_CHUNK
    p = SSD_HEAD_DIM
    x = x_ref[0]
    bm = b_ref[0]
    cm = c_ref[0].astype(BF16)
    row = lax.broadcasted_iota(jnp.int32, (ln, ln), 0)
    col = lax.broadcasted_iota(jnp.int32, (ln, ln), 1)
    causal = col <= row
    ones_lower = jnp.where(causal, 1.0, 0.0).astype(BF16)
    ones_upper = jnp.where(row <= col, 1.0, 0.0).astype(BF16)

    dtc = _softplus(dtc_ref[0, 0] + bc_ref[0])
    da_c = dtc * (-jnp.exp(ac_ref[0]))
    acs_c = sum(jnp.dot(ones_lower, t, preferred_element_type=F32) for t in _split3(da_c))
    dtr = _softplus(dtr_ref[0, 0] + br_ref[0])
    da_r = dtr * (-jnp.exp(ar_ref[0]))
    acs_r = sum(jnp.dot(t, ones_upper, preferred_element_type=F32) for t in _split3(da_r))

    scores = lax.dot_general(cm, bm.astype(BF16), NT_DIMS, preferred_element_type=F32)
    dsk = d_ref[0]
    for h in range(SSD_HEADS_PER_GROUP):
        ac = acs_c[:, h:h + 1]
        ar = acs_r[h:h + 1, :]
        decay = jnp.exp(jnp.where(causal, ac - ar, -jnp.inf))
        xh = x[:, h * p:(h + 1) * p]
        xc = (xh * dtc[:, h:h + 1]).astype(BF16)
        y = jnp.dot((scores * decay).astype(BF16), xc, preferred_element_type=F32)
        prev = state_ref[h]
        y = y + jnp.dot(cm, prev.astype(BF16), preferred_element_type=F32) * jnp.exp(ac)
        a_last = ac[ln - 1:ln, :]
        bs = (bm * jnp.exp(a_last - ac)).astype(BF16)
        st = lax.dot_general(bs, xc, TN_DIMS, preferred_element_type=F32)
        state_ref[h] = prev * jnp.exp(a_last) + st
        y_ref[:, h * p:(h + 1) * p] = y + dsk[:, h:h + 1] * xh

    z = z_ref[0]
    yz = y_ref[...] * (z * _sigmoid(z))
    ms = jnp.mean(yz * yz, axis=-1, keepdims=True)
    o_ref[0] = yz * lax.rsqrt(ms + RMS_EPS) * ng_ref[0]


def ssd_scan(xbc, proj3, dt_raw, dt_bias, a_log, d_skip, norm_g):
    b, s, _ = xbc.shape
    ln, g, hg, n = SSD_CHUNK, SSD_GROUPS, SSD_HEADS_PER_GROUP, SSD_STATE
    gw = hg * SSD_HEAD_DIM
    dt4 = dt_raw.reshape(b, s, g, hg)
    dt_col = dt4.transpose(0, 2, 1, 3)
    dt_row = dt4.transpose(0, 2, 3, 1)
    col = lambda v: v.reshape(g, 1, hg)
    rowv = lambda v: v.reshape(g, hg, 1)
    pc = pl.BlockSpec((1, 1, hg), lambda i, j, c: (j, 0, 0))
    pr = pl.BlockSpec((1, hg, 1), lambda i, j, c: (j, 0, 0))
    return pl.pallas_call(
        _ssd_kernel,
        grid=(b, g, s // ln),
        in_specs=[
            pl.BlockSpec((1, ln, gw), lambda i, j, c: (i, c, j)),
            pl.BlockSpec((1, ln, n), lambda i, j, c: (i, c, MIX_WIDTH // n + j)),
            pl.BlockSpec((1, ln, n), lambda i, j, c: (i, c, MIX_WIDTH // n + g + j)),
            pl.BlockSpec((1, ln, gw), lambda i, j, c: (i, c, j)),
            pl.BlockSpec((1, 1, ln, hg), lambda i, j, c: (i, j, c, 0)),
            pl.BlockSpec((1, 1, hg, ln), lambda i, j, c: (i, j, 0, c)),
            pc, pr, pc, pr, pc,
            pl.BlockSpec((1, 1, gw), lambda i, j, c: (j, 0, 0)),
        ],
        out_specs=pl.BlockSpec((1, ln, gw), lambda i, j, c: (i, c, j)),
        out_shape=jax.ShapeDtypeStruct((b, s, MIX_WIDTH), F32),
        scratch_shapes=[pltpu.VMEM((hg, n, SSD_HEAD_DIM), F32), pltpu.VMEM((ln, gw), F32)],
        compiler_params=_cp(("parallel", "parallel", "arbitrary"), 32),
        name="ssd_scan",
    )(xbc, xbc, xbc, proj3, dt_col, dt_row, col(dt_bias), rowv(dt_bias), col(a_log), rowv(a_log),
      col(d_skip), norm_g.reshape(g, 1, gw))


def _mem_attn_kernel(q_ref, kv_ref, o_ref):
    d = MEM_HEAD_DIM
    for h in range(MEM_HEADS):
        q = (q_ref[0, :, h * d:(h + 1) * d] * (d ** -0.5)).astype(BF16)
        k = kv_ref[0, :, h * d:(h + 1) * d].astype(BF16)
        v = kv_ref[0, :, MEM_WIDTH + h * d:MEM_WIDTH + (h + 1) * d].astype(BF16)
        sc = lax.dot_general(q, k, NT_DIMS, preferred_element_type=F32)
        e = jnp.exp(sc - jnp.max(sc, axis=-1, keepdims=True))
        pr = (e / jnp.sum(e, axis=-1, keepdims=True)).astype(BF16)
        o_ref[0, :, h * d:(h + 1) * d] = jnp.dot(pr, v, preferred_element_type=F32)


def memory_attention(proj3, q_block, mem_kv):
    b, s, _ = proj3.shape
    m = mem_kv.shape[1]
    tq = 512
    return pl.pallas_call(
        _mem_attn_kernel,
        grid=(b, s // tq),
        in_specs=[
            pl.BlockSpec((1, tq, MEM_WIDTH), lambda i, j: (i, j, q_block)),
            pl.BlockSpec((1, m, 2 * MEM_WIDTH), lambda i, j: (i, 0, 0)),
        ],
        out_specs=pl.BlockSpec((1, tq, MEM_WIDTH), lambda i, j: (i, j, 0)),
        out_shape=jax.ShapeDtypeStruct((b, s, MEM_WIDTH), F32),
        compiler_params=_cp(("parallel", "parallel"), 32),
        name="memory_attention",
    )(proj3, mem_kv)


def _outproj_kernel(x_ref, a_ref, m_ref, wa_ref, wm_ref, o_ref):
    acc = jnp.dot(a_ref[...].astype(BF16), wa_ref[...].astype(BF16), preferred_element_type=F32)
    acc = acc + jnp.dot(m_ref[...].astype(BF16), wm_ref[...].astype(BF16),
                        preferred_element_type=F32)
    o_ref[...] = x_ref[...] + acc


def out_projection(x, mix, mem_out, w_out, li):
    m, d = x.shape
    tm, tn = 1024, 512
    return pl.pallas_call(
        _outproj_kernel,
        grid=(m // tm, d // tn),
        in_specs=[
            pl.BlockSpec((tm, tn), lambda i, j: (i, j)),
            pl.BlockSpec((tm, MIX_WIDTH), lambda i, j: (i, 0)),
            pl.BlockSpec((tm, MEM_WIDTH), lambda i, j: (i, 0)),
            pl.BlockSpec((None, MIX_WIDTH, tn), lambda i, j: (li, 0, j)),
            pl.BlockSpec((None, MEM_WIDTH, tn), lambda i, j: (li, MIX_WIDTH // MEM_WIDTH, j)),
        ],
        out_specs=pl.BlockSpec((tm, tn), lambda i, j: (i, j)),
        out_shape=jax.ShapeDtypeStruct((m, d), F32),
        compiler_params=_cp(("parallel", "parallel"), 48),
        name="out_projection",
    )(x, mix, mem_out, w_out, w_out)


def _router_kernel(x_ref, g_ref, wr_ref, br_ref, t_ref, r_ref):
    x = x_ref[...]
    ms = jnp.mean(x * x, axis=-1, keepdims=True)
    t = x * lax.rsqrt(ms + RMS_EPS) * g_ref[...]
    th = t.astype(BF16)
    t_ref[...] = t
    tl = (t - th.astype(F32)).astype(BF16)
    w = wr_ref[...]
    wh = w.astype(BF16)
    wl = (w - wh.astype(F32)).astype(BF16)
    lg = (lax.dot_general(wh, th, NT_DIMS, preferred_element_type=F32)
          + lax.dot_general(wh, tl, NT_DIMS, preferred_element_type=F32)
          + lax.dot_general(wl, th, NT_DIMS, preferred_element_type=F32)) + br_ref[...]
    ng, epg = N_EXPERT_GROUPS, EXPERTS_PER_GROUP
    gl = [lg[j:j + 1] for j in range(ng)]
    el = [lg[ng + j:ng + j + 1] for j in range(N_EXPERTS)]

    def first_argmax(vals):
        top = functools.reduce(jnp.maximum, vals)
        idx = jnp.full(top.shape, len(vals) - 1, jnp.int32)
        for j in range(len(vals) - 2, -1, -1):
            idx = jnp.where(vals[j] >= top, j, idx)
        return top, idx

    gmax, gsel = first_argmax(gl)
    g_w = 1.0 / functools.reduce(lambda a, b: a + b, [jnp.exp(v - gmax) for v in gl])
    e_in = []
    for j in range(epg):
        v = el[(ng - 1) * epg + j]
        for gi in range(ng - 2, -1, -1):
            v = jnp.where(gsel == gi, el[gi * epg + j], v)
        e_in.append(v)
    emax = functools.reduce(jnp.maximum, e_in)
    pe = [jnp.exp(v - emax) for v in e_in]
    se = functools.reduce(lambda a, b: a + b, pe)
    prob = [v / se for v in pe]
    v1, i1 = first_argmax(prob)
    rest = [jnp.where(i1 == j, -1.0, prob[j]) for j in range(epg)]
    v2, i2 = first_argmax(rest)
    tot = v1 + v2
    r_ref[...] = jnp.concatenate(
        [(gsel * epg + i1).astype(F32), (gsel * epg + i2).astype(F32),
         v1 / tot * g_w, v2 / tot * g_w, jnp.zeros((4, x.shape[0]), F32)], axis=0)


def moe_router(x, g, w_group, b_group, w_expert, b_expert):
    m, d = x.shape
    tm = 512
    nr = 32
    wr = jnp.zeros((nr, d), F32).at[:N_EXPERT_GROUPS].set(w_group.T)
    wr = wr.at[N_EXPERT_GROUPS:N_EXPERT_GROUPS + N_EXPERTS].set(w_expert.T)
    br = jnp.zeros((nr, 1), F32).at[:N_EXPERT_GROUPS, 0].set(b_group)
    br = br.at[N_EXPERT_GROUPS:N_EXPERT_GROUPS + N_EXPERTS, 0].set(b_expert)
    return pl.pallas_call(
        _router_kernel,
        grid=(m // tm,),
        in_specs=[
            pl.BlockSpec((tm, d), lambda i: (i, 0)),
            pl.BlockSpec((1, d), lambda i: (0, 0)),
            pl.BlockSpec((nr, d), lambda i: (0, 0)),
            pl.BlockSpec((nr, 1), lambda i: (0, 0)),
        ],
        out_specs=[pl.BlockSpec((tm, d), lambda i: (i, 0)), pl.BlockSpec((8, tm), lambda i: (0, i))],
        out_shape=[jax.ShapeDtypeStruct((m, d), F32), jax.ShapeDtypeStruct((8, m), F32)],
        compiler_params=_cp(("parallel",), 40),
        name="moe_router",
    )(x, g.reshape(1, d), wr, br)


def _row_copy(src_ref, dst_ref, sem, src_row, dst_row, n_rows):
    return pltpu.make_async_copy(src_ref.at[pl.ds(src_row, n_rows)],
                                 dst_ref.at[pl.ds(dst_row, n_rows)], sem)


def _gather_kernel(idx_ref, src_ref, dst_ref, sem):
    n_chunks = dst_ref.shape[0] // GATHER_CHUNK

    def issue(chunk):
        def body(r, carry):
            i = chunk * GATHER_CHUNK + r
            _row_copy(src_ref, dst_ref, sem, idx_ref[i], i, 1).start()
            return carry
        lax.fori_loop(0, GATHER_CHUNK, body, 0, unroll=8)

    def wait_chunk():
        _row_copy(src_ref, dst_ref, sem, 0, 0, GATHER_CHUNK).wait()

    issue(0)

    def step(chunk, carry):
        issue(chunk)
        wait_chunk()
        return carry
    lax.fori_loop(1, n_chunks, step, 0)
    wait_chunk()


def row_gather(src, idx):
    n = idx.shape[0]
    d = src.shape[1]
    return pl.pallas_call(
        _gather_kernel,
        grid_spec=pltpu.PrefetchScalarGridSpec(
            num_scalar_prefetch=1,
            grid=(1,),
            in_specs=[pl.BlockSpec(memory_space=pl.ANY)],
            out_specs=pl.BlockSpec(memory_space=pl.ANY),
            scratch_shapes=[pltpu.SemaphoreType.DMA(())],
        ),
        out_shape=jax.ShapeDtypeStruct((n, d), src.dtype),
        compiler_params=pltpu.CompilerParams(dimension_semantics=("arbitrary",),
                                             has_side_effects=True),
        name="row_gather",
    )(idx, src)


def _ffn_kernel(te_ref, x_ref, wg_ref, wu_ref, wd_ref, o_ref, wgb_ref, wub_ref, wdb_ref):
    i = pl.program_id(0)
    prev = te_ref[jnp.maximum(i - 1, 0)]

    @pl.when((i == 0) | (te_ref[i] != prev))
    def _():
        wgb_ref[...] = wg_ref[...].astype(BF16)
        wub_ref[...] = wu_ref[...].astype(BF16)
        wdb_ref[...] = wd_ref[...].astype(BF16)

    x = x_ref[...].astype(BF16)
    gate = jnp.dot(x, wgb_ref[...], preferred_element_type=F32)
    up = jnp.dot(x, wub_ref[...], preferred_element_type=F32)
    hid = (gate * _sigmoid(gate) * up).astype(BF16)
    o_ref[...] = jnp.dot(hid, wdb_ref[...], preferred_element_type=F32)


def grouped_ffn(xs, tile_expert, w_gate, w_up, w_down, li):
    p, d = xs.shape
    f = D_EXPERT
    tm = MOE_TILE
    return pl.pallas_call(
        _ffn_kernel,
        grid_spec=pltpu.PrefetchScalarGridSpec(
            num_scalar_prefetch=1,
            grid=(p // tm,),
            in_specs=[
                pl.BlockSpec((tm, d), lambda i, te: (i, 0)),
                pl.BlockSpec((None, None, d, f), lambda i, te: (li, te[i], 0, 0)),
                pl.BlockSpec((None, None, d, f), lambda i, te: (li, te[i], 0, 0)),
                pl.BlockSpec((None, None, f, d), lambda i, te: (li, te[i], 0, 0)),
            ],
            out_specs=pl.BlockSpec((tm, d), lambda i, te: (i, 0)),
            scratch_shapes=[pltpu.VMEM((d, f), BF16), pltpu.VMEM((d, f), BF16),
                            pltpu.VMEM((f, d), BF16)],
        ),
        out_shape=jax.ShapeDtypeStruct((p, d), F32),
        compiler_params=_cp(("arbitrary",), 48),
        name="grouped_ffn",
    )(tile_expert, xs, w_gate, w_up, w_down)


def _combine_kernel(x_ref, y0_ref, y1_ref, w_ref, o_ref):
    w = w_ref[...]
    o_ref[...] = x_ref[...] + w[:, 0:1] * y0_ref[...] + w[:, 1:2] * y1_ref[...]


def _combine_norm_kernel(x_ref, y0_ref, y1_ref, w_ref, g_ref, o_ref):
    w = w_ref[...]
    x = x_ref[...] + w[:, 0:1] * y0_ref[...] + w[:, 1:2] * y1_ref[...]
    ms = jnp.mean(x * x, axis=-1, keepdims=True)
    o_ref[...] = x * lax.rsqrt(ms + RMS_EPS) * g_ref[...]


def moe_combine(x, y01, w01, final_g=None):
    m, d = x.shape
    tm = 512
    nb = m // tm
    specs = [
        pl.BlockSpec((tm, d), lambda i: (i, 0)),
        pl.BlockSpec((tm, d), lambda i: (i, 0)),
        pl.BlockSpec((tm, d), lambda i: (i + nb, 0)),
        pl.BlockSpec((tm, 2), lambda i: (i, 0)),
    ]
    args = [x, y01, y01, w01]
    body = _combine_kernel
    if final_g is not None:
        specs.append(pl.BlockSpec((1, d), lambda i: (0, 0)))
        args.append(final_g.reshape(1, d))
        body = _combine_norm_kernel
    return pl.pallas_call(
        body,
        grid=(nb,),
        in_specs=specs,
        out_specs=pl.BlockSpec((tm, d), lambda i: (i, 0)),
        out_shape=jax.ShapeDtypeStruct((m, d), F32),
        compiler_params=_cp(("parallel",), 48),
        name="moe_combine",
    )(*args)


def hier_moe_layer(x, norm_g, w_group, b_group, w_expert, b_expert, w_gate, w_up, w_down, li,
                   final_g=None):
    m, _ = x.shape
    t, route = moe_router(x, norm_g, w_group, b_group, w_expert, b_expert)

    tm = MOE_TILE
    n_slots = 2 * m + N_EXPERTS * tm
    flat_e = route[0:2].astype(jnp.int32).reshape(-1)
    onehot = (flat_e[:, None] == jnp.arange(N_EXPERTS)[None, :]).astype(jnp.int32)
    csum = jnp.cumsum(onehot, axis=0)
    rank = jnp.sum(csum * onehot, axis=1) - 1
    counts = csum[-1]
    padded = (counts + tm - 1) // tm * tm
    ends = jnp.cumsum(padded)
    starts = ends - padded
    pos = jnp.sum(onehot * starts[None, :], axis=1) + rank
    token = jnp.arange(2 * m, dtype=jnp.int32) % m
    src = jnp.zeros((n_slots,), jnp.int32).at[pos].set(token)
    tile_start = jnp.arange(n_slots // tm, dtype=jnp.int32) * tm
    tile_expert = jnp.minimum(
        jnp.sum((tile_start[:, None] >= ends[None, :]).astype(jnp.int32), axis=1), N_EXPERTS - 1)

    xs = row_gather(t, src)
    ys = grouped_ffn(xs, tile_expert, w_gate, w_up, w_down, li)
    y01 = row_gather(ys, pos.astype(jnp.int32))
    return moe_combine(x, y01, route[2:4].T, final_g)


def _rope_tables(seq, dim):
    pos = jnp.arange(seq, dtype=F32)
    inv = ROPE_THETA ** (-jnp.arange(0, dim, 2, dtype=F32) / dim)
    ang = pos[:, None] * inv[None, :]
    ang = jnp.concatenate([ang, ang], axis=-1)
    sign = jnp.concatenate([-jnp.ones((dim // 2,), F32), jnp.ones((dim // 2,), F32)])
    return jnp.cos(ang), jnp.sin(ang) * sign[None, :]


def _lambda_init(depth_idx):
    return 0.8 - 0.6 * math.exp(-0.3 * depth_idx)


def kernel(x, mem, norm_mix, norm_mem, norm_ffn, norm_final, w_out, w_mem_kv, pool_w_in, pool_w_grp, pool_scale, diff_w_in, diff_lambda, diff_subln, ssd_w_in, ssd_conv_w, ssd_conv_b, ssd_dt_bias, ssd_a_log, ssd_d, ssd_norm, moe_w_group, moe_b_group, moe_w_expert, moe_b_expert, moe_w_gate, moe_w_up, moe_w_down):
    b, s, d = x.shape
    m = b * s
    mem_len = mem.shape[1]
    cos, sin_signed = _rope_tables(s, DIFF_HEAD_DIM)
    xt = x.reshape(m, d)
    memt = mem.reshape(b * mem_len, d)
    for i in range(DEPTH):
        kind, slot = i % N_MIXERS, i // N_MIXERS
        mem_kv = norm_matmul(memt, norm_mem[i], w_mem_kv, i, 2 * MEM_WIDTH, b * mem_len, 512)
        mem_kv = mem_kv.reshape(b, mem_len, 2 * MEM_WIDTH)
        if kind == 0:
            n_in = MIX_WIDTH + MEM_WIDTH
            proj = norm_matmul(xt, norm_mix[i], pool_w_in, slot, n_in, 1024, 512).reshape(b, s, n_in)
            mix = pool_mixer(proj, pool_w_grp, slot, pool_scale[slot])
        elif kind == 1:
            n_in = 3 * MIX_WIDTH + MEM_WIDTH
            proj = norm_matmul(xt, norm_mix[i], diff_w_in, slot, n_in, 1024, 512).reshape(b, s, n_in)
            q, k, v = rope_qkv(proj, cos, sin_signed)
            mix = diff_attention(q, k, v, diff_lambda[slot], diff_subln[slot], _lambda_init(i))
        else:
            w = ssd_w_in[slot]
            n_main = MIX_WIDTH + SSD_CONV_DIM
            w = jnp.concatenate(
                [w[:, :n_main], w[:, SSD_MIX_IN:], w[:, n_main:SSD_MIX_IN],
                 jnp.zeros((d, SSD_IN_PADDED - w.shape[1]), F32)], axis=1)
            n_in = SSD_IN_PADDED
            proj = norm_matmul(xt, norm_mix[i], w[None], 0, n_in, 1024, 512).reshape(b, s, n_in)
            xbc = ssd_conv(proj, ssd_conv_w[slot], ssd_conv_b[slot])
            dt_off = n_main + MEM_WIDTH
            dt_raw = proj[:, :, dt_off:dt_off + SSD_HEADS]
            mix = ssd_scan(xbc, proj, dt_raw, ssd_dt_bias[slot], ssd_a_log[slot], ssd_d[slot],
                           ssd_norm[slot])
        q_block = (MIX_WIDTH if kind == 0 else 3 * MIX_WIDTH if kind == 1
                   else MIX_WIDTH + SSD_CONV_DIM) // MEM_WIDTH
        mem_out = memory_attention(proj, q_block, mem_kv)
        xt = out_projection(xt, mix.reshape(m, MIX_WIDTH), mem_out.reshape(m, MEM_WIDTH), w_out, i)
        xt = hier_moe_layer(xt, norm_ffn[i], moe_w_group[i], moe_b_group[i], moe_w_expert[i],
                            moe_b_expert[i], moe_w_gate, moe_w_up, moe_w_down, i,
                            final_g=norm_final if i == DEPTH - 1 else None)
    return xt.reshape(b, s, d)
```

```python
import functools
import math

import jax
import jax.numpy as jnp
from jax import lax
from jax.experimental import pallas as pl
from jax.experimental.pallas import tpu as pltpu

F32 = jnp.float32
BF16 = jnp.bfloat16

D_MODEL = 2048
DEPTH = 4
N_MIXERS = 3
MIX_WIDTH = 1536
MEM_WIDTH = 512
MEM_HEADS = 4
MEM_HEAD_DIM = 128
POOL_WINDOWS = (2, 4, 8, 16)
POOL_GROUPS = 4
POOL_GROUP_DIM = 384
DIFF_HEAD_DIM = 128
DIFF_HEADS = 6
DIFF_V_DIM = 256
ROPE_THETA = 10000.0
SSD_HEAD_DIM = 64
SSD_HEADS = 24
SSD_GROUPS = 4
SSD_HEADS_PER_GROUP = 6
SSD_STATE = 128
SSD_CONV = 4
SSD_CHUNK = 128
SSD_CONV_DIM = 2560
SSD_MIX_IN = 4120
SSD_IN_PADDED = 5120
N_EXPERT_GROUPS = 4
EXPERTS_PER_GROUP = 4
N_EXPERTS = 16
D_EXPERT = 512
RMS_EPS = 1e-6

MOE_TILE = 256
MOE_SUB = 256
SEG_ALIGN = 8
MOE_CBUF = 2 * MOE_SUB + 128
MOE_WCOLS = 128
MIB = 1024 * 1024

NT_DIMS = (((1,), (1,)), ((), ()))
TN_DIMS = (((0,), (0,)), ((), ()))


def _cp(sem, vmem_mib):
    return pltpu.CompilerParams(dimension_semantics=sem, vmem_limit_bytes=vmem_mib * MIB)


def _sigmoid(x):
    return 1.0 / (1.0 + jnp.exp(-x))


def _softplus(x):
    return jnp.maximum(x, 0.0) + jnp.log1p(jnp.exp(-jnp.abs(x)))


def _split3(v):
    hi = v.astype(BF16)
    r = v - hi.astype(F32)
    mid = r.astype(BF16)
    lo = (r - mid.astype(F32)).astype(BF16)
    return hi, mid, lo


def _norm_matmul_kernel(x_ref, g_ref, w_ref, o_ref, h_ref):
    @pl.when(pl.program_id(1) == 0)
    def _():
        x = x_ref[...]
        ms = jnp.mean(x * x, axis=-1, keepdims=True)
        h_ref[...] = (x * lax.rsqrt(ms + RMS_EPS) * g_ref[...]).astype(BF16)

    o_ref[...] = jnp.dot(h_ref[...], w_ref[...].astype(BF16), preferred_element_type=F32)


def norm_matmul(x, g, w, li, n_cols, tm, tn):
    m, k = x.shape
    return pl.pallas_call(
        _norm_matmul_kernel,
        grid=(m // tm, n_cols // tn),
        in_specs=[
            pl.BlockSpec((tm, k), lambda i, j: (i, 0)),
            pl.BlockSpec((1, k), lambda i, j: (0, 0)),
            pl.BlockSpec((None, k, tn), lambda i, j: (li, 0, j)),
        ],
        out_specs=pl.BlockSpec((tm, tn), lambda i, j: (i, j)),
        out_shape=jax.ShapeDtypeStruct((m, n_cols), F32),
        scratch_shapes=[pltpu.VMEM((tm, k), BF16)],
        compiler_params=_cp(("parallel", "arbitrary"), 48),
        name="norm_matmul",
    )(x, g.reshape(1, k), w)


def _pool_kernel(u_ref, w_ref, sc_ref, o_ref, pad_ref):
    grp = pl.program_id(1)
    s, c = u_ref.shape[1], u_ref.shape[2]
    rows = 256
    pad_ref[0:16, :] = jnp.zeros((16, c), F32)
    pad_ref[16:, :] = u_ref[0]
    wb = w_ref[...].astype(BF16)
    sc = sc_ref[0]

    for gi, win in enumerate(POOL_WINDOWS):
        @pl.when(grp == gi)
        def _(win=win):
            for r in range(s // rows):
                xh = pad_ref[r * rows:r * rows + rows + 16, :]
                acc = xh
                k = 1
                while k < win:
                    acc = acc + pltpu.roll(acc, k, axis=0)
                    k *= 2
                t = r * rows + lax.broadcasted_iota(jnp.int32, (rows, 1), 0)
                cnt = jnp.minimum(t + 1, win).astype(F32)
                mixed = (acc[16:, :] / cnt - xh[16:, :]).astype(BF16)
                o_ref[0, r * rows:(r + 1) * rows, :] = (
                    jnp.dot(mixed, wb, preferred_element_type=F32) * sc)


def pool_mixer(proj3, w_grp, li, scale):
    b, s, _ = proj3.shape
    c = POOL_GROUP_DIM
    return pl.pallas_call(
        _pool_kernel,
        grid=(b, POOL_GROUPS),
        in_specs=[
            pl.BlockSpec((1, s, c), lambda i, g: (i, 0, g)),
            pl.BlockSpec((None, None, c, c), lambda i, g: (li, g, 0, 0)),
            pl.BlockSpec((1, 1, c), lambda i, g: (g, 0, 0)),
        ],
        out_specs=pl.BlockSpec((1, s, c), lambda i, g: (i, 0, g)),
        out_shape=jax.ShapeDtypeStruct((b, s, MIX_WIDTH), F32),
        scratch_shapes=[pltpu.VMEM((s + 16, c), F32)],
        compiler_params=_cp(("parallel", "parallel"), 40),
        name="pool_mixer",
    )(proj3, w_grp, scale.reshape(POOL_GROUPS, 1, c))


def _rope_kernel(q_ref, k_ref, v_ref, cos_ref, sin_ref, qo_ref, ko_ref, vo_ref):
    cos = cos_ref[...]
    sin = sin_ref[...]
    half = DIFF_HEAD_DIM // 2
    for c in range(MIX_WIDTH // DIFF_HEAD_DIM):
        sl = slice(c * DIFF_HEAD_DIM, (c + 1) * DIFF_HEAD_DIM)
        x = q_ref[0, :, sl]
        r = x * cos + pltpu.roll(x, half, axis=1) * sin
        qo_ref[0, :, sl] = (r * (DIFF_HEAD_DIM ** -0.5)).astype(BF16)
        x = k_ref[0, :, sl]
        ko_ref[0, :, sl] = (x * cos + pltpu.roll(x, half, axis=1) * sin).astype(BF16)
    vo_ref[...] = v_ref[...].astype(BF16)


def rope_qkv(proj3, cos, sin_signed):
    b, s, _ = proj3.shape
    ts = 256
    w = MIX_WIDTH
    out = jax.ShapeDtypeStruct((b, s, w), BF16)
    return pl.pallas_call(
        _rope_kernel,
        grid=(b, s // ts),
        in_specs=[
            pl.BlockSpec((1, ts, w), lambda i, j: (i, j, 0)),
            pl.BlockSpec((1, ts, w), lambda i, j: (i, j, 1)),
            pl.BlockSpec((1, ts, w), lambda i, j: (i, j, 2)),
            pl.BlockSpec((ts, DIFF_HEAD_DIM), lambda i, j: (j, 0)),
            pl.BlockSpec((ts, DIFF_HEAD_DIM), lambda i, j: (j, 0)),
        ],
        out_specs=[pl.BlockSpec((1, ts, w), lambda i, j: (i, j, 0))] * 3,
        out_shape=[out, out, out],
        compiler_params=_cp(("parallel", "parallel"), 40),
        name="rope_qkv",
    )(proj3, proj3, proj3, cos, sin_signed)


def _diff_attn_kernel(lam_ref, sub_ref, q_ref, k_ref, v_ref, o_ref, *, lambda_init):
    lam = lam_ref[...]
    s1 = jnp.sum(lam[0:1] * lam[1:2], axis=-1, keepdims=True)
    s2 = jnp.sum(lam[2:3] * lam[3:4], axis=-1, keepdims=True)
    lmbda = jnp.exp(s1) - jnp.exp(s2) + lambda_init
    s = q_ref.shape[1]
    tq = 256
    d = DIFF_HEAD_DIM
    for i in range(s // tq):
        kv = (i + 1) * tq
        q = q_ref[0, i * tq:(i + 1) * tq, :]
        row = i * tq + lax.broadcasted_iota(jnp.int32, (tq, kv), 0)
        col = lax.broadcasted_iota(jnp.int32, (tq, kv), 1)
        mask = col <= row
        probs = []
        for c in range(2):
            sc = lax.dot_general(q[:, c * d:(c + 1) * d], k_ref[0, 0:kv, c * d:(c + 1) * d],
                                 NT_DIMS, preferred_element_type=F32)
            sc = jnp.where(mask, sc, -jnp.inf)
            e = jnp.exp(sc - jnp.max(sc, axis=-1, keepdims=True))
            probs.append(e / jnp.sum(e, axis=-1, keepdims=True))
        a = (probs[0] - lmbda * probs[1]).astype(BF16)
        o = jnp.dot(a, v_ref[0, 0:kv, :], preferred_element_type=F32)
        ms = jnp.mean(o * o, axis=-1, keepdims=True)
        o_ref[0, i * tq:(i + 1) * tq, :] = (
            o * lax.rsqrt(ms + RMS_EPS) * sub_ref[...] * (1.0 - lambda_init))


def diff_attention(q, k, v, lam, subln, lambda_init):
    b, s, _ = q.shape
    vd = DIFF_V_DIM
    blk = pl.BlockSpec((1, s, vd), lambda i, h: (i, 0, h))
    return pl.pallas_call(
        functools.partial(_diff_attn_kernel, lambda_init=lambda_init),
        grid=(b, DIFF_HEADS),
        in_specs=[
            pl.BlockSpec((4, DIFF_HEAD_DIM), lambda i, h: (0, 0)),
            pl.BlockSpec((1, vd), lambda i, h: (0, 0)),
            blk, blk, blk,
        ],
        out_specs=blk,
        out_shape=jax.ShapeDtypeStruct((b, s, MIX_WIDTH), F32),
        compiler_params=_cp(("parallel", "parallel"), 48),
        name="diff_attention",
    )(lam, subln.reshape(1, vd), q, k, v)


def _conv_kernel(u_ref, w_ref, b_ref, o_ref, pad_ref):
    s, c = u_ref.shape[1], u_ref.shape[2]
    rows = 256
    pad_ref[0:8, :] = jnp.zeros((8, c), F32)
    pad_ref[8:, :] = u_ref[0]
    w = w_ref[...]
    bias = b_ref[...]
    for r in range(s // rows):
        xh = pad_ref[r * rows:r * rows + rows + 8, :]
        y = xh * w[3:4]
        for j in range(1, SSD_CONV):
            y = y + pltpu.roll(xh, j, axis=0) * w[SSD_CONV - 1 - j:SSD_CONV - j]
        y = y[8:, :] + bias
        o_ref[0, r * rows:(r + 1) * rows, :] = y * _sigmoid(y)


def ssd_conv(proj3, conv_w, conv_b):
    b, s, _ = proj3.shape
    tc = 512
    off = MIX_WIDTH // tc
    return pl.pallas_call(
        _conv_kernel,
        grid=(b, SSD_CONV_DIM // tc),
        in_specs=[
            pl.BlockSpec((1, s, tc), lambda i, j: (i, 0, off + j)),
            pl.BlockSpec((SSD_CONV, tc), lambda i, j: (0, j)),
            pl.BlockSpec((1, tc), lambda i, j: (0, j)),
        ],
        out_specs=pl.BlockSpec((1, s, tc), lambda i, j: (i, 0, j)),
        out_shape=jax.ShapeDtypeStruct((b, s, SSD_CONV_DIM), F32),
        scratch_shapes=[pltpu.VMEM((s + 8, tc), F32)],
        compiler_params=_cp(("parallel", "parallel"), 48),
        name="ssd_conv",
    )(proj3, conv_w, conv_b.reshape(1, SSD_CONV_DIM))


def _ssd_kernel(x_ref, b_ref, c_ref, z_ref, dtc_ref, dtr_ref, bc_ref, br_ref, ac_ref, ar_ref,
                d_ref, ng_ref, o_ref, state_ref, y_ref):
    @pl.when(pl.program_id(2) == 0)
    def _():
        state_ref[...] = jnp.zeros(state_ref.shape, F32)

    ln = SSD_CHUNK
    p = SSD_HEAD_DIM
    x = x_ref[0]
    bm = b_ref[0]
    cm = c_ref[0].astype(BF16)
    row = lax.broadcasted_iota(jnp.int32, (ln, ln), 0)
    col = lax.broadcasted_iota(jnp.int32, (ln, ln), 1)
    causal = col <= row
    ones_lower = jnp.where(causal, 1.0, 0.0).astype(BF16)
    ones_upper = jnp.where(row <= col, 1.0, 0.0).astype(BF16)

    dtc = _softplus(dtc_ref[0, 0] + bc_ref[0])
    da_c = dtc * (-jnp.exp(ac_ref[0]))
    acs_c = sum(jnp.dot(ones_lower, t, preferred_element_type=F32) for t in _split3(da_c))
    dtr = _softplus(dtr_ref[0, 0] + br_ref[0])
    da_r = dtr * (-jnp.exp(ar_ref[0]))
    acs_r = sum(jnp.dot(t, ones_upper, preferred_element_type=F32) for t in _split3(da_r))

    scores = lax.dot_general(cm, bm.astype(BF16), NT_DIMS, preferred_element_type=F32)
    dsk = d_ref[0]
    for h in range(SSD_HEADS_PER_GROUP):
        ac = acs_c[:, h:h + 1]
        ar = acs_r[h:h + 1, :]
        decay = jnp.exp(jnp.where(causal, ac - ar, -jnp.inf))
        xh = x[:, h * p:(h + 1) * p]
        xc = (xh * dtc[:, h:h + 1]).astype(BF16)
        y = jnp.dot((scores * decay).astype(BF16), xc, preferred_element_type=F32)
        prev = state_ref[h]
        y = y + jnp.dot(cm, prev.astype(BF16), preferred_element_type=F32) * jnp.exp(ac)
        a_last = ac[ln - 1:ln, :]
        bs = (bm * jnp.exp(a_last - ac)).astype(BF16)
        st = lax.dot_general(bs, xc, TN_DIMS, preferred_element_type=F32)
        state_ref[h] = prev * jnp.exp(a_last) + st
        y_ref[:, h * p:(h + 1) * p] = y + dsk[:, h:h + 1] * xh

    z = z_ref[0]
    yz = y_ref[...] * (z * _sigmoid(z))
    ms = jnp.mean(yz * yz, axis=-1, keepdims=True)
    o_ref[0] = yz * lax.rsqrt(ms + RMS_EPS) * ng_ref[0]


def ssd_scan(xbc, proj3, dt_raw, dt_bias, a_log, d_skip, norm_g):
    b, s, _ = xbc.shape
    ln, g, hg, n = SSD_CHUNK, SSD_GROUPS, SSD_HEADS_PER_GROUP, SSD_STATE
    gw = hg * SSD_HEAD_DIM
    dt4 = dt_raw.reshape(b, s, g, hg)
    dt_col = dt4.transpose(0, 2, 1, 3)
    dt_row = dt4.transpose(0, 2, 3, 1)
    col = lambda v: v.reshape(g, 1, hg)
    rowv = lambda v: v.reshape(g, hg, 1)
    pc = pl.BlockSpec((1, 1, hg), lambda i, j, c: (j, 0, 0))
    pr = pl.BlockSpec((1, hg, 1), lambda i, j, c: (j, 0, 0))
    return pl.pallas_call(
        _ssd_kernel,
        grid=(b, g, s // ln),
        in_specs=[
            pl.BlockSpec((1, ln, gw), lambda i, j, c: (i, c, j)),
            pl.BlockSpec((1, ln, n), lambda i, j, c: (i, c, MIX_WIDTH // n + j)),
            pl.BlockSpec((1, ln, n), lambda i, j, c: (i, c, MIX_WIDTH // n + g + j)),
            pl.BlockSpec((1, ln, gw), lambda i, j, c: (i, c, j)),
            pl.BlockSpec((1, 1, ln, hg), lambda i, j, c: (i, j, c, 0)),
            pl.BlockSpec((1, 1, hg, ln), lambda i, j, c: (i, j, 0, c)),
            pc, pr, pc, pr, pc,
            pl.BlockSpec((1, 1, gw), lambda i, j, c: (j, 0, 0)),
        ],
        out_specs=pl.BlockSpec((1, ln, gw), lambda i, j, c: (i, c, j)),
        out_shape=jax.ShapeDtypeStruct((b, s, MIX_WIDTH), F32),
        scratch_shapes=[pltpu.VMEM((hg, n, SSD_HEAD_DIM), F32), pltpu.VMEM((ln, gw), F32)],
        compiler_params=_cp(("parallel", "parallel", "arbitrary"), 32),
        name="ssd_scan",
    )(xbc, xbc, xbc, proj3, dt_col, dt_row, col(dt_bias), rowv(dt_bias), col(a_log), rowv(a_log),
      col(d_skip), norm_g.reshape(g, 1, gw))


def _mem_attn_kernel(q_ref, kv_ref, o_ref):
    d = MEM_HEAD_DIM
    for h in range(MEM_HEADS):
        q = (q_ref[0, :, h * d:(h + 1) * d] * (d ** -0.5)).astype(BF16)
        k = kv_ref[0, :, h * d:(h + 1) * d].astype(BF16)
        v = kv_ref[0, :, MEM_WIDTH + h * d:MEM_WIDTH + (h + 1) * d].astype(BF16)
        sc = lax.dot_general(q, k, NT_DIMS, preferred_element_type=F32)
        e = jnp.exp(sc - jnp.max(sc, axis=-1, keepdims=True))
        pr = (e / jnp.sum(e, axis=-1, keepdims=True)).astype(BF16)
        o_ref[0, :, h * d:(h + 1) * d] = jnp.dot(pr, v, preferred_element_type=F32)


def memory_attention(proj3, q_block, mem_kv):
    b, s, _ = proj3.shape
    m = mem_kv.shape[1]
    tq = 512
    return pl.pallas_call(
        _mem_attn_kernel,
        grid=(b, s // tq),
        in_specs=[
            pl.BlockSpec((1, tq, MEM_WIDTH), lambda i, j: (i, j, q_block)),
            pl.BlockSpec((1, m, 2 * MEM_WIDTH), lambda i, j: (i, 0, 0)),
        ],
        out_specs=pl.BlockSpec((1, tq, MEM_WIDTH), lambda i, j: (i, j, 0)),
        out_shape=jax.ShapeDtypeStruct((b, s, MEM_WIDTH), F32),
        compiler_params=_cp(("parallel", "parallel"), 32),
        name="memory_attention",
    )(proj3, mem_kv)


def _outproj_kernel(x_ref, a_ref, m_ref, wa_ref, wm_ref, o_ref):
    acc = jnp.dot(a_ref[...].astype(BF16), wa_ref[...].astype(BF16), preferred_element_type=F32)
    acc = acc + jnp.dot(m_ref[...].astype(BF16), wm_ref[...].astype(BF16),
                        preferred_element_type=F32)
    o_ref[...] = x_ref[...] + acc


def out_projection(x, mix, mem_out, w_out, li):
    m, d = x.shape
    tm, tn = 1024, 512
    return pl.pallas_call(
        _outproj_kernel,
        grid=(m // tm, d // tn),
        in_specs=[
            pl.BlockSpec((tm, tn), lambda i, j: (i, j)),
            pl.BlockSpec((tm, MIX_WIDTH), lambda i, j: (i, 0)),
            pl.BlockSpec((tm, MEM_WIDTH), lambda i, j: (i, 0)),
            pl.BlockSpec((None, MIX_WIDTH, tn), lambda i, j: (li, 0, j)),
            pl.BlockSpec((None, MEM_WIDTH, tn), lambda i, j: (li, MIX_WIDTH // MEM_WIDTH, j)),
        ],
        out_specs=pl.BlockSpec((tm, tn), lambda i, j: (i, j)),
        out_shape=jax.ShapeDtypeStruct((m, d), F32),
        compiler_params=_cp(("parallel", "parallel"), 48),
        name="out_projection",
    )(x, mix, mem_out, w_out, w_out)


def _router_kernel(x_ref, g_ref, wr_ref, br_ref, r_ref):
    x = x_ref[...]
    ms = jnp.mean(x * x, axis=-1, keepdims=True)
    t = x * lax.rsqrt(ms + RMS_EPS) * g_ref[...]
    th = t.astype(BF16)
    tl = (t - th.astype(F32)).astype(BF16)
    w = wr_ref[...]
    wh = w.astype(BF16)
    wl = (w - wh.astype(F32)).astype(BF16)
    lg = (lax.dot_general(wh, th, NT_DIMS, preferred_element_type=F32)
          + lax.dot_general(wh, tl, NT_DIMS, preferred_element_type=F32)
          + lax.dot_general(wl, th, NT_DIMS, preferred_element_type=F32)) + br_ref[...]
    ng, epg = N_EXPERT_GROUPS, EXPERTS_PER_GROUP
    gl = [lg[j:j + 1] for j in range(ng)]
    el = [lg[ng + j:ng + j + 1] for j in range(N_EXPERTS)]

    def first_argmax(vals):
        top = functools.reduce(jnp.maximum, vals)
        idx = jnp.full(top.shape, len(vals) - 1, jnp.int32)
        for j in range(len(vals) - 2, -1, -1):
            idx = jnp.where(vals[j] >= top, j, idx)
        return top, idx

    gmax, gsel = first_argmax(gl)
    g_w = 1.0 / functools.reduce(lambda a, b: a + b, [jnp.exp(v - gmax) for v in gl])
    e_in = []
    for j in range(epg):
        v = el[(ng - 1) * epg + j]
        for gi in range(ng - 2, -1, -1):
            v = jnp.where(gsel == gi, el[gi * epg + j], v)
        e_in.append(v)
    emax = functools.reduce(jnp.maximum, e_in)
    pe = [jnp.exp(v - emax) for v in e_in]
    se = functools.reduce(lambda a, b: a + b, pe)
    prob = [v / se for v in pe]
    v1, i1 = first_argmax(prob)
    rest = [jnp.where(i1 == j, -1.0, prob[j]) for j in range(epg)]
    v2, i2 = first_argmax(rest)
    tot = v1 + v2
    r_ref[...] = jnp.concatenate(
        [(gsel * epg + i1).astype(F32), (gsel * epg + i2).astype(F32),
         v1 / tot * g_w, v2 / tot * g_w, jnp.zeros((4, x.shape[0]), F32)], axis=0)


def moe_router(x, g, w_group, b_group, w_expert, b_expert):
    m, d = x.shape
    tm = 512
    nr = 32
    wr = jnp.zeros((nr, d), F32).at[:N_EXPERT_GROUPS].set(w_group.T)
    wr = wr.at[N_EXPERT_GROUPS:N_EXPERT_GROUPS + N_EXPERTS].set(w_expert.T)
    br = jnp.zeros((nr, 1), F32).at[:N_EXPERT_GROUPS, 0].set(b_group)
    br = br.at[N_EXPERT_GROUPS:N_EXPERT_GROUPS + N_EXPERTS, 0].set(b_expert)
    return pl.pallas_call(
        _router_kernel,
        grid=(m // tm,),
        in_specs=[
            pl.BlockSpec((tm, d), lambda i: (i, 0)),
            pl.BlockSpec((1, d), lambda i: (0, 0)),
            pl.BlockSpec((nr, d), lambda i: (0, 0)),
            pl.BlockSpec((nr, 1), lambda i: (0, 0)),
        ],
        out_specs=pl.BlockSpec((8, tm), lambda i: (0, i)),
        out_shape=jax.ShapeDtypeStruct((8, m), F32),
        compiler_params=_cp(("parallel",), 40),
        name="moe_router",
    )(x, g.reshape(1, d), wr, br)


def moe_plan(e01, m):
    nsub = m // MOE_SUB
    tm = MOE_TILE
    ar = jnp.arange(N_EXPERTS, dtype=jnp.int32)
    oh = ((e01[0][:, None] == ar[None, :]) | (e01[1][:, None] == ar[None, :])).astype(jnp.int32)
    oh3 = oh.reshape(nsub, MOE_SUB, N_EXPERTS)
    cs = jnp.cumsum(oh3, axis=1)
    excl = (cs - oh3).reshape(m, N_EXPERTS)
    seg = (cs[:, -1, :] + SEG_ALIGN - 1) // SEG_ALIGN * SEG_ALIGN
    loc_off = jnp.cumsum(seg, axis=1) - seg
    reg_off = jnp.cumsum(seg, axis=0) - seg
    length = jnp.sum(seg, axis=0)
    padded = (length + tm - 1) // tm * tm
    e_end = jnp.cumsum(padded)
    e_start = e_end - padded
    glob_off = e_start[None, :] + reg_off
    sub = jnp.arange(m, dtype=jnp.int32) // MOE_SUB
    lpos = []
    for k in range(2):
        rank = jnp.take_along_axis(excl, e01[k][:, None], axis=1)[:, 0]
        lpos.append(loc_off[sub, e01[k]] + rank)
    n_slots = (2 * m + nsub * N_EXPERTS * (SEG_ALIGN - 1) + N_EXPERTS * (tm - 1) + tm - 1) // tm * tm
    tile_start = jnp.arange(n_slots // tm, dtype=jnp.int32) * tm
    tile_expert = jnp.minimum(
        jnp.sum((tile_start[:, None] >= e_end[None, :]).astype(jnp.int32), axis=1), N_EXPERTS - 1)
    return dict(seg=seg.reshape(-1), loc_off=loc_off.reshape(-1), glob_off=glob_off.reshape(-1),
                lpos=jnp.stack(lpos), n_slots=n_slots, tile_expert=tile_expert,
                n_used=(e_end[-1:] // tm).astype(jnp.int32),
                pad_start=jnp.concatenate([e_start + length, e_end[-1:]]).astype(jnp.int32),
                pad_len=jnp.concatenate([padded - length, n_slots - e_end[-1:]]).astype(jnp.int32))


def _aligned(v):
    return v if isinstance(v, int) else pl.multiple_of(v, SEG_ALIGN)


def _seg_copy(src_ref, dst_ref, sem, src_row, dst_row, n_rows):
    n_rows = _aligned(n_rows)
    return pltpu.make_async_copy(src_ref.at[pl.ds(_aligned(src_row), n_rows)],
                                 dst_ref.at[pl.ds(_aligned(dst_row), n_rows)], sem)


def _start_then_wait(copies):
    for n, cp in copies:
        @pl.when(n > 0)
        def _(cp=cp):
            cp.start()
    for n, cp in copies:
        @pl.when(n > 0)
        def _(cp=cp):
            cp.wait()


def _dispatch_kernel(seg_ref, loc_ref, glob_ref, pst_ref, pln_ref, x_ref, g_ref, lp_ref, rt_ref,
                     xs_ref, cbuf_ref, zbuf_ref, sem):
    s = pl.program_id(0)
    d = x_ref.shape[1]
    x = x_ref[...]
    ms = jnp.mean(x * x, axis=-1, keepdims=True)
    t = (x * lax.rsqrt(ms + RMS_EPS) * g_ref[...]).astype(BF16)
    lp = lp_ref[...]
    slot = lax.broadcasted_iota(jnp.int32, (MOE_CBUF, MOE_SUB), 0)
    hit0 = slot == lp[0:1]
    hit1 = slot == lp[1:2]
    onehot = jnp.where(hit0 | hit1, 1.0, 0.0).astype(BF16)
    cbuf_ref[:, 0:d] = jnp.dot(onehot, t, preferred_element_type=F32)
    rt = rt_ref[...]
    wslot = jnp.sum(jnp.where(hit0, rt[2:3], 0.0) + jnp.where(hit1, rt[3:4], 0.0),
                    axis=1, keepdims=True)
    cbuf_ref[:, d:] = jnp.broadcast_to(wslot, (MOE_CBUF, MOE_WCOLS))
    _start_then_wait([
        (seg_ref[s * N_EXPERTS + e],
         _seg_copy(cbuf_ref, xs_ref, sem, loc_ref[s * N_EXPERTS + e], glob_ref[s * N_EXPERTS + e],
                   seg_ref[s * N_EXPERTS + e]))
        for e in range(N_EXPERTS)])

    @pl.when(s == 0)
    def _():
        zbuf_ref[...] = jnp.zeros(zbuf_ref.shape, F32)
        _start_then_wait([
            (pln_ref[e], _seg_copy(zbuf_ref, xs_ref, sem, 0, pst_ref[e], pln_ref[e]))
            for e in range(N_EXPERTS)])
        tail_tiles = pln_ref[N_EXPERTS] // MOE_TILE

        def tail_copy(k):
            return _seg_copy(zbuf_ref, xs_ref, sem, 0, pst_ref[N_EXPERTS] + k * MOE_TILE, MOE_TILE)

        def start(k, carry):
            tail_copy(k).start()
            return carry

        def wait(k, carry):
            tail_copy(k).wait()
            return carry
        lax.fori_loop(0, tail_tiles, start, 0)
        lax.fori_loop(0, tail_tiles, wait, 0)


def moe_dispatch(x, g, route, plan):
    m, d = x.shape
    dw = d + MOE_WCOLS
    return pl.pallas_call(
        _dispatch_kernel,
        grid_spec=pltpu.PrefetchScalarGridSpec(
            num_scalar_prefetch=5,
            grid=(m // MOE_SUB,),
            in_specs=[
                pl.BlockSpec((MOE_SUB, d), lambda i, *_: (i, 0)),
                pl.BlockSpec((1, d), lambda i, *_: (0, 0)),
                pl.BlockSpec((2, MOE_SUB), lambda i, *_: (0, i)),
                pl.BlockSpec((8, MOE_SUB), lambda i, *_: (0, i)),
            ],
            out_specs=pl.BlockSpec(memory_space=pl.ANY),
            scratch_shapes=[pltpu.VMEM((MOE_CBUF, dw), F32), pltpu.VMEM((MOE_TILE, dw), F32),
                            pltpu.SemaphoreType.DMA(())],
        ),
        out_shape=jax.ShapeDtypeStruct((plan["n_slots"], dw), F32),
        compiler_params=pltpu.CompilerParams(dimension_semantics=("arbitrary",),
                                             vmem_limit_bytes=40 * MIB, has_side_effects=True),
        name="moe_dispatch",
    )(plan["seg"], plan["loc_off"], plan["glob_off"], plan["pad_start"], plan["pad_len"],
      x, g.reshape(1, d), plan["lpos"], route)


def _ffn_kernel(te_ref, nu_ref, x_ref, wg_ref, wu_ref, wd_ref, o_ref,
                wgb_ref, wub_ref, wdb_ref):
    i = pl.program_id(0)
    d = o_ref.shape[1]
    prev = te_ref[jnp.maximum(i - 1, 0)]

    @pl.when(i < nu_ref[0])
    def _():
        @pl.when((i == 0) | (te_ref[i] != prev))
        def _():
            wgb_ref[...] = wg_ref[...].astype(BF16)
            wub_ref[...] = wu_ref[...].astype(BF16)
            wdb_ref[...] = wd_ref[...].astype(BF16)

        x = x_ref[:, 0:d].astype(BF16)
        gate = jnp.dot(x, wgb_ref[...], preferred_element_type=F32)
        up = jnp.dot(x, wub_ref[...], preferred_element_type=F32)
        hid = (gate * _sigmoid(gate) * up * x_ref[:, d:d + 1]).astype(BF16)
        o_ref[...] = jnp.dot(hid, wdb_ref[...], preferred_element_type=F32)

    @pl.when(i >= nu_ref[0])
    def _():
        o_ref[...] = jnp.zeros(o_ref.shape, F32)


def grouped_ffn(xs, plan, w_gate, w_up, w_down, li):
    p, dw = xs.shape
    d = dw - MOE_WCOLS
    f = D_EXPERT
    tm = MOE_TILE
    row = lambda i, te, nu: (jnp.minimum(i, nu[0] - 1), 0)
    wmap = lambda i, te, nu: (li, te[jnp.minimum(i, nu[0] - 1)], 0, 0)
    return pl.pallas_call(
        _ffn_kernel,
        grid_spec=pltpu.PrefetchScalarGridSpec(
            num_scalar_prefetch=2,
            grid=(p // tm,),
            in_specs=[
                pl.BlockSpec((tm, dw), row),
                pl.BlockSpec((None, None, d, f), wmap),
                pl.BlockSpec((None, None, d, f), wmap),
                pl.BlockSpec((None, None, f, d), wmap),
            ],
            out_specs=pl.BlockSpec((tm, d), lambda i, te, nu: (i, 0)),
            scratch_shapes=[pltpu.VMEM((d, f), BF16), pltpu.VMEM((d, f), BF16),
                            pltpu.VMEM((f, d), BF16)],
        ),
        out_shape=jax.ShapeDtypeStruct((p, d), F32),
        compiler_params=_cp(("arbitrary",), 48),
        name="grouped_ffn",
    )(plan["tile_expert"], plan["n_used"], xs, w_gate, w_up, w_down)


def _combine_kernel(seg_ref, loc_ref, glob_ref, x_ref, lp_ref, ys_ref, *rest, final_norm):
    if final_norm:
        g_ref, o_ref, ybuf_ref, sem = rest
    else:
        o_ref, ybuf_ref, sem = rest
    s = pl.program_id(0)
    ybuf_ref[...] = jnp.zeros(ybuf_ref.shape, F32)
    _start_then_wait([
        (seg_ref[s * N_EXPERTS + e],
         _seg_copy(ys_ref, ybuf_ref, sem, glob_ref[s * N_EXPERTS + e], loc_ref[s * N_EXPERTS + e],
                   seg_ref[s * N_EXPERTS + e]))
        for e in range(N_EXPERTS)])

    lp = lp_ref[...]
    slot = lax.broadcasted_iota(jnp.int32, (MOE_SUB, MOE_CBUF), 1)
    onehot = jnp.where((slot == lp[:, 0:1]) | (slot == lp[:, 1:2]), 1.0, 0.0).astype(BF16)
    y = ybuf_ref[...]
    hi = y.astype(BF16)
    lo = (y - hi.astype(F32)).astype(BF16)
    out = (x_ref[...] + jnp.dot(onehot, hi, preferred_element_type=F32)
           + jnp.dot(onehot, lo, preferred_element_type=F32))
    if final_norm:
        ms = jnp.mean(out * out, axis=-1, keepdims=True)
        out = out * lax.rsqrt(ms + RMS_EPS) * g_ref[...]
    o_ref[...] = out


def moe_combine(x, ys, plan, final_g=None):
    m, d = x.shape
    in_specs = [
        pl.BlockSpec((MOE_SUB, d), lambda i, *_: (i, 0)),
        pl.BlockSpec((MOE_SUB, 2), lambda i, *_: (i, 0)),
        pl.BlockSpec(memory_space=pl.ANY),
    ]
    args = [x, plan["lpos"].T, ys]
    if final_g is not None:
        in_specs.append(pl.BlockSpec((1, d), lambda i, *_: (0, 0)))
        args.append(final_g.reshape(1, d))
    return pl.pallas_call(
        functools.partial(_combine_kernel, final_norm=final_g is not None),
        grid_spec=pltpu.PrefetchScalarGridSpec(
            num_scalar_prefetch=3,
            grid=(m // MOE_SUB,),
            in_specs=in_specs,
            out_specs=pl.BlockSpec((MOE_SUB, d), lambda i, *_: (i, 0)),
            scratch_shapes=[pltpu.VMEM((MOE_CBUF, d), F32), pltpu.SemaphoreType.DMA(())],
        ),
        out_shape=jax.ShapeDtypeStruct((m, d), F32),
        compiler_params=pltpu.CompilerParams(dimension_semantics=("arbitrary",),
                                             vmem_limit_bytes=40 * MIB),
        name="moe_combine",
    )(plan["seg"], plan["loc_off"], plan["glob_off"], *args)


def hier_moe_layer(x, norm_g, w_group, b_group, w_expert, b_expert, w_gate, w_up, w_down, li,
                   final_g=None):
    m, _ = x.shape
    route = moe_router(x, norm_g, w_group, b_group, w_expert, b_expert)
    plan = moe_plan(route[0:2].astype(jnp.int32), m)
    xs = moe_dispatch(x, norm_g, route, plan)
    ys = grouped_ffn(xs, plan, w_gate, w_up, w_down, li)
    return moe_combine(x, ys, plan, final_g)


def _rope_tables(seq, dim):
    pos = jnp.arange(seq, dtype=F32)
    inv = ROPE_THETA ** (-jnp.arange(0, dim, 2, dtype=F32) / dim)
    ang = pos[:, None] * inv[None, :]
    ang = jnp.concatenate([ang, ang], axis=-1)
    sign = jnp.concatenate([-jnp.ones((dim // 2,), F32), jnp.ones((dim // 2,), F32)])
    return jnp.cos(ang), jnp.sin(ang) * sign[None, :]


def _lambda_init(depth_idx):
    return 0.8 - 0.6 * math.exp(-0.3 * depth_idx)


def kernel(x, mem, norm_mix, norm_mem, norm_ffn, norm_final, w_out, w_mem_kv, pool_w_in, pool_w_grp, pool_scale, diff_w_in, diff_lambda, diff_subln, ssd_w_in, ssd_conv_w, ssd_conv_b, ssd_dt_bias, ssd_a_log, ssd_d, ssd_norm, moe_w_group, moe_b_group, moe_w_expert, moe_b_expert, moe_w_gate, moe_w_up, moe_w_down):
    b, s, d = x.shape
    m = b * s
    mem_len = mem.shape[1]
    cos, sin_signed = _rope_tables(s, DIFF_HEAD_DIM)
    xt = x.reshape(m, d)
    memt = mem.reshape(b * mem_len, d)
    for i in range(DEPTH):
        kind, slot = i % N_MIXERS, i // N_MIXERS
        mem_kv = norm_matmul(memt, norm_mem[i], w_mem_kv, i, 2 * MEM_WIDTH, b * mem_len, 512)
        mem_kv = mem_kv.reshape(b, mem_len, 2 * MEM_WIDTH)
        if kind == 0:
            n_in = MIX_WIDTH + MEM_WIDTH
            proj = norm_matmul(xt, norm_mix[i], pool_w_in, slot, n_in, 1024, 512).reshape(b, s, n_in)
            mix = pool_mixer(proj, pool_w_grp, slot, pool_scale[slot])
        elif kind == 1:
            n_in = 3 * MIX_WIDTH + MEM_WIDTH
            proj = norm_matmul(xt, norm_mix[i], diff_w_in, slot, n_in, 1024, 512).reshape(b, s, n_in)
            q, k, v = rope_qkv(proj, cos, sin_signed)
            mix = diff_attention(q, k, v, diff_lambda[slot], diff_subln[slot], _lambda_init(i))
        else:
            w = ssd_w_in[slot]
            n_main = MIX_WIDTH + SSD_CONV_DIM
            w = jnp.concatenate(
                [w[:, :n_main], w[:, SSD_MIX_IN:], w[:, n_main:SSD_MIX_IN],
                 jnp.zeros((d, SSD_IN_PADDED - w.shape[1]), F32)], axis=1)
            n_in = SSD_IN_PADDED
            proj = norm_matmul(xt, norm_mix[i], w[None], 0, n_in, 1024, 512).reshape(b, s, n_in)
            xbc = ssd_conv(proj, ssd_conv_w[slot], ssd_conv_b[slot])
            dt_off = n_main + MEM_WIDTH
            dt_raw = proj[:, :, dt_off:dt_off + SSD_HEADS]
            mix = ssd_scan(xbc, proj, dt_raw, ssd_dt_bias[slot], ssd_a_log[slot], ssd_d[slot],
                           ssd_norm[slot])
        q_block = (MIX_WIDTH if kind == 0 else 3 * MIX_WIDTH if kind == 1
                   else MIX_WIDTH + SSD_CONV_DIM) // MEM_WIDTH
        mem_out = memory_attention(proj, q_block, mem_kv)
        xt = out_projection(xt, mix.reshape(m, MIX_WIDTH), mem_out.reshape(m, MEM_WIDTH), w_out, i)
        xt = hier_moe_layer(xt, norm_ffn[i], moe_w_group[i], moe_b_group[i], moe_w_expert[i],
                            moe_b_expert[i], moe_w_gate, moe_w_up, moe_w_down, i,
                            final_g=norm_final if i == DEPTH - 1 else None)
    return xt.reshape(b, s, d)
```

```python
import functools
import math

import jax
import jax.numpy as jnp
from jax import lax
from jax.experimental import pallas as pl
from jax.experimental.pallas import tpu as pltpu

F32 = jnp.float32
BF16 = jnp.bfloat16

D_MODEL = 2048
DEPTH = 4
N_MIXERS = 3
MIX_WIDTH = 1536
MEM_WIDTH = 512
MEM_HEADS = 4
MEM_HEAD_DIM = 128
POOL_WINDOWS = (2, 4, 8, 16)
POOL_GROUPS = 4
POOL_GROUP_DIM = 384
DIFF_HEAD_DIM = 128
DIFF_HEADS = 6
DIFF_V_DIM = 256
ROPE_THETA = 10000.0
SSD_HEAD_DIM = 64
SSD_HEADS = 24
SSD_GROUPS = 4
SSD_HEADS_PER_GROUP = 6
SSD_STATE = 128
SSD_CONV = 4
SSD_CHUNK = 128
SSD_CONV_DIM = 2560
SSD_MIX_IN = 4120
SSD_IN_PADDED = 5120
N_EXPERT_GROUPS = 4
EXPERTS_PER_GROUP = 4
N_EXPERTS = 16
D_EXPERT = 512
RMS_EPS = 1e-6

IN_TM, IN_TN = 2048, 256
MOE_TILE = 256
MOE_SUB = 256
SEG_ALIGN = 8
MOE_CBUF = 2 * MOE_SUB + 128
MOE_WCOLS = 128
MIB = 1024 * 1024

NT_DIMS = (((1,), (1,)), ((), ()))
TN_DIMS = (((0,), (0,)), ((), ()))


def _cp(sem, vmem_mib):
    return pltpu.CompilerParams(dimension_semantics=sem, vmem_limit_bytes=vmem_mib * MIB)


def _sigmoid(x):
    return 1.0 / (1.0 + jnp.exp(-x))


def _softplus(x):
    return jnp.maximum(x, 0.0) + jnp.log1p(jnp.exp(-jnp.abs(x)))


def _split3(v):
    hi = v.astype(BF16)
    r = v - hi.astype(F32)
    mid = r.astype(BF16)
    lo = (r - mid.astype(F32)).astype(BF16)
    return hi, mid, lo


def _norm_matmul_kernel(x_ref, g_ref, w_ref, *rest, rope_tiles):
    if rope_tiles is None:
        o_ref, h_ref = rest
    else:
        cos_ref, sin_ref, o_ref, h_ref = rest
    j = pl.program_id(1)

    @pl.when(j == 0)
    def _():
        x = x_ref[...]
        ms = jnp.mean(x * x, axis=-1, keepdims=True)
        h_ref[...] = (x * lax.rsqrt(ms + RMS_EPS) * g_ref[...]).astype(BF16)

    acc = jnp.dot(h_ref[...], w_ref[...].astype(BF16), preferred_element_type=F32)
    if rope_tiles is None:
        o_ref[...] = acc.astype(o_ref.dtype)
        return
    n_q, n_k = rope_tiles
    hd = DIFF_HEAD_DIM

    @pl.when(j < n_q + n_k)
    def _():
        scale = jnp.where(j < n_q, hd ** -0.5, 1.0)
        cos = cos_ref[...]
        sin = sin_ref[...]
        for c in range(o_ref.shape[1] // hd):
            x = acc[:, c * hd:(c + 1) * hd]
            r = x * cos + pltpu.roll(x, hd // 2, axis=1) * sin
            o_ref[:, c * hd:(c + 1) * hd] = (r * scale).astype(o_ref.dtype)

    @pl.when(j >= n_q + n_k)
    def _():
        o_ref[...] = acc.astype(o_ref.dtype)


def norm_matmul(x, g, w, li, n_cols, tm, tn, out_dtype, rope=None):
    m, k = x.shape
    in_specs = [
        pl.BlockSpec((tm, k), lambda i, j: (i, 0)),
        pl.BlockSpec((1, k), lambda i, j: (0, 0)),
        pl.BlockSpec((None, k, tn), lambda i, j: (li, 0, j)),
    ]
    args = [x, g.reshape(1, k), w]
    rope_tiles = None
    if rope is not None:
        cos, sin_signed, n_q, n_k, seq = rope
        rope_tiles = (n_q // tn, n_k // tn)
        pos_blocks = seq // tm
        tab = pl.BlockSpec((tm, DIFF_HEAD_DIM), lambda i, j: (i % pos_blocks, 0))
        in_specs += [tab, tab]
        args += [cos, sin_signed]
    return pl.pallas_call(
        functools.partial(_norm_matmul_kernel, rope_tiles=rope_tiles),
        grid=(m // tm, n_cols // tn),
        in_specs=in_specs,
        out_specs=pl.BlockSpec((tm, tn), lambda i, j: (i, j)),
        out_shape=jax.ShapeDtypeStruct((m, n_cols), out_dtype),
        scratch_shapes=[pltpu.VMEM((tm, k), BF16)],
        compiler_params=_cp(("parallel", "arbitrary"), 58),
        name="norm_matmul",
    )(*args)


def _pool_kernel(u_ref, w_ref, sc_ref, o_ref, pad_ref):
    grp = pl.program_id(1)
    s, c = u_ref.shape[1], u_ref.shape[2]
    rows = 256
    pad_ref[0:16, :] = jnp.zeros((16, c), F32)
    pad_ref[16:, :] = u_ref[0].astype(F32)
    wb = w_ref[...].astype(BF16)
    sc = sc_ref[0]

    for gi, win in enumerate(POOL_WINDOWS):
        @pl.when(grp == gi)
        def _(win=win):
            for r in range(s // rows):
                xh = pad_ref[r * rows:r * rows + rows + 16, :]
                acc = xh
                k = 1
                while k < win:
                    acc = acc + pltpu.roll(acc, k, axis=0)
                    k *= 2
                t = r * rows + lax.broadcasted_iota(jnp.int32, (rows, 1), 0)
                cnt = jnp.minimum(t + 1, win).astype(F32)
                mixed = (acc[16:, :] / cnt - xh[16:, :]).astype(BF16)
                o_ref[0, r * rows:(r + 1) * rows, :] = (
                    jnp.dot(mixed, wb, preferred_element_type=F32) * sc).astype(o_ref.dtype)


def pool_mixer(proj3, w_grp, li, scale):
    b, s, _ = proj3.shape
    c = POOL_GROUP_DIM
    return pl.pallas_call(
        _pool_kernel,
        grid=(b, POOL_GROUPS),
        in_specs=[
            pl.BlockSpec((1, s, c), lambda i, g: (i, 0, g)),
            pl.BlockSpec((None, None, c, c), lambda i, g: (li, g, 0, 0)),
            pl.BlockSpec((1, 1, c), lambda i, g: (g, 0, 0)),
        ],
        out_specs=pl.BlockSpec((1, s, c), lambda i, g: (i, 0, g)),
        out_shape=jax.ShapeDtypeStruct((b, s, MIX_WIDTH), BF16),
        scratch_shapes=[pltpu.VMEM((s + 16, c), F32)],
        compiler_params=_cp(("parallel", "parallel"), 40),
        name="pool_mixer",
    )(proj3, w_grp, scale.reshape(POOL_GROUPS, 1, c))


def _diff_attn_kernel(lam_ref, sub_ref, q_ref, k_ref, v_ref, o_ref, *, lambda_init):
    lam = lam_ref[...]
    s1 = jnp.sum(lam[0:1] * lam[1:2], axis=-1, keepdims=True)
    s2 = jnp.sum(lam[2:3] * lam[3:4], axis=-1, keepdims=True)
    lmbda = jnp.exp(s1) - jnp.exp(s2) + lambda_init
    s = q_ref.shape[1]
    tq = 256
    d = DIFF_HEAD_DIM
    for i in range(s // tq):
        kv = (i + 1) * tq
        q = q_ref[0, i * tq:(i + 1) * tq, :]
        row = i * tq + lax.broadcasted_iota(jnp.int32, (tq, kv), 0)
        col = lax.broadcasted_iota(jnp.int32, (tq, kv), 1)
        mask = col <= row
        probs = []
        for c in range(2):
            sc = lax.dot_general(q[:, c * d:(c + 1) * d], k_ref[0, 0:kv, c * d:(c + 1) * d],
                                 NT_DIMS, preferred_element_type=F32)
            sc = jnp.where(mask, sc, -jnp.inf)
            e = jnp.exp(sc - jnp.max(sc, axis=-1, keepdims=True))
            probs.append(e / jnp.sum(e, axis=-1, keepdims=True))
        a = (probs[0] - lmbda * probs[1]).astype(BF16)
        o = jnp.dot(a, v_ref[0, 0:kv, :], preferred_element_type=F32)
        ms = jnp.mean(o * o, axis=-1, keepdims=True)
        o_ref[0, i * tq:(i + 1) * tq, :] = (
            o * lax.rsqrt(ms + RMS_EPS) * sub_ref[...] * (1.0 - lambda_init)).astype(o_ref.dtype)


def diff_attention(proj3, lam, subln, lambda_init):
    b, s, _ = proj3.shape
    vd = DIFF_V_DIM
    nh = DIFF_HEADS
    return pl.pallas_call(
        functools.partial(_diff_attn_kernel, lambda_init=lambda_init),
        grid=(b, nh),
        in_specs=[
            pl.BlockSpec((4, DIFF_HEAD_DIM), lambda i, h: (0, 0)),
            pl.BlockSpec((1, vd), lambda i, h: (0, 0)),
            pl.BlockSpec((1, s, vd), lambda i, h: (i, 0, h)),
            pl.BlockSpec((1, s, vd), lambda i, h: (i, 0, nh + h)),
            pl.BlockSpec((1, s, vd), lambda i, h: (i, 0, 2 * nh + h)),
        ],
        out_specs=pl.BlockSpec((1, s, vd), lambda i, h: (i, 0, h)),
        out_shape=jax.ShapeDtypeStruct((b, s, MIX_WIDTH), BF16),
        compiler_params=_cp(("parallel", "parallel"), 48),
        name="diff_attention",
    )(lam, subln.reshape(1, vd), proj3, proj3, proj3)


def _conv_kernel(u_ref, w_ref, b_ref, o_ref, pad_ref):
    s, c = u_ref.shape[1], u_ref.shape[2]
    rows = 256
    pad_ref[0:8, :] = jnp.zeros((8, c), F32)
    pad_ref[8:, :] = u_ref[0]
    w = w_ref[...]
    bias = b_ref[...]
    for r in range(s // rows):
        xh = pad_ref[r * rows:r * rows + rows + 8, :]
        y = xh * w[3:4]
        for j in range(1, SSD_CONV):
            y = y + pltpu.roll(xh, j, axis=0) * w[SSD_CONV - 1 - j:SSD_CONV - j]
        y = y[8:, :] + bias
        o_ref[0, r * rows:(r + 1) * rows, :] = y * _sigmoid(y)


def ssd_conv(proj3, conv_w, conv_b):
    b, s, _ = proj3.shape
    tc = 512
    off = MIX_WIDTH // tc
    return pl.pallas_call(
        _conv_kernel,
        grid=(b, SSD_CONV_DIM // tc),
        in_specs=[
            pl.BlockSpec((1, s, tc), lambda i, j: (i, 0, off + j)),
            pl.BlockSpec((SSD_CONV, tc), lambda i, j: (0, j)),
            pl.BlockSpec((1, tc), lambda i, j: (0, j)),
        ],
        out_specs=pl.BlockSpec((1, s, tc), lambda i, j: (i, 0, j)),
        out_shape=jax.ShapeDtypeStruct((b, s, SSD_CONV_DIM), F32),
        scratch_shapes=[pltpu.VMEM((s + 8, tc), F32)],
        compiler_params=_cp(("parallel", "parallel"), 48),
        name="ssd_conv",
    )(proj3, conv_w, conv_b.reshape(1, SSD_CONV_DIM))


def _ssd_kernel(x_ref, b_ref, c_ref, z_ref, dtc_ref, dtr_ref, bc_ref, br_ref, ac_ref, ar_ref,
                d_ref, ng_ref, o_ref, state_ref, y_ref):
    @pl.when(pl.program_id(2) == 0)
    def _():
        state_ref[...] = jnp.zeros(state_ref.shape, F32)

    ln = SSD_CHUNK
    p = SSD_HEAD_DIM
    x = x_ref[0]
    bm = b_ref[0]
    cm = c_ref[0].astype(BF16)
    row = lax.broadcasted_iota(jnp.int32, (ln, ln), 0)
    col = lax.broadcasted_iota(jnp.int32, (ln, ln), 1)
    causal = col <= row
    ones_lower = jnp.where(causal, 1.0, 0.0).astype(BF16)
    ones_upper = jnp.where(row <= col, 1.0, 0.0).astype(BF16)

    dtc = _softplus(dtc_ref[0, 0] + bc_ref[0])
    da_c = dtc * (-jnp.exp(ac_ref[0]))
    acs_c = sum(jnp.dot(ones_lower, t, preferred_element_type=F32) for t in _split3(da_c))
    dtr = _softplus(dtr_ref[0, 0] + br_ref[0])
    da_r = dtr * (-jnp.exp(ar_ref[0]))
    acs_r = sum(jnp.dot(t, ones_upper, preferred_element_type=F32) for t in _split3(da_r))

    scores = lax.dot_general(cm, bm.astype(BF16), NT_DIMS, preferred_element_type=F32)
    dsk = d_ref[0]
    for h in range(SSD_HEADS_PER_GROUP):
        ac = acs_c[:, h:h + 1]
        ar = acs_r[h:h + 1, :]
        decay = jnp.exp(jnp.where(causal, ac - ar, -jnp.inf))
        xh = x[:, h * p:(h + 1) * p]
        xc = (xh * dtc[:, h:h + 1]).astype(BF16)
        y = jnp.dot((scores * decay).astype(BF16), xc, preferred_element_type=F32)
        prev = state_ref[h]
        y = y + jnp.dot(cm, prev.astype(BF16), preferred_element_type=F32) * jnp.exp(ac)
        a_last = ac[ln - 1:ln, :]
        bs = (bm * jnp.exp(a_last - ac)).astype(BF16)
        st = lax.dot_general(bs, xc, TN_DIMS, preferred_element_type=F32)
        state_ref[h] = prev * jnp.exp(a_last) + st
        y_ref[:, h * p:(h + 1) * p] = y + dsk[:, h:h + 1] * xh

    z = z_ref[0]
    yz = y_ref[...] * (z * _sigmoid(z))
    ms = jnp.mean(yz * yz, axis=-1, keepdims=True)
    o_ref[0] = (yz * lax.rsqrt(ms + RMS_EPS) * ng_ref[0]).astype(o_ref.dtype)


def ssd_scan(xbc, proj3, dt_raw, dt_bias, a_log, d_skip, norm_g):
    b, s, _ = xbc.shape
    ln, g, hg, n = SSD_CHUNK, SSD_GROUPS, SSD_HEADS_PER_GROUP, SSD_STATE
    gw = hg * SSD_HEAD_DIM
    dt4 = dt_raw.reshape(b, s, g, hg)
    dt_col = dt4.transpose(0, 2, 1, 3)
    dt_row = dt4.transpose(0, 2, 3, 1)
    col = lambda v: v.reshape(g, 1, hg)
    rowv = lambda v: v.reshape(g, hg, 1)
    pc = pl.BlockSpec((1, 1, hg), lambda i, j, c: (j, 0, 0))
    pr = pl.BlockSpec((1, hg, 1), lambda i, j, c: (j, 0, 0))
    return pl.pallas_call(
        _ssd_kernel,
        grid=(b, g, s // ln),
        in_specs=[
            pl.BlockSpec((1, ln, gw), lambda i, j, c: (i, c, j)),
            pl.BlockSpec((1, ln, n), lambda i, j, c: (i, c, MIX_WIDTH // n + j)),
            pl.BlockSpec((1, ln, n), lambda i, j, c: (i, c, MIX_WIDTH // n + g + j)),
            pl.BlockSpec((1, ln, gw), lambda i, j, c: (i, c, j)),
            pl.BlockSpec((1, 1, ln, hg), lambda i, j, c: (i, j, c, 0)),
            pl.BlockSpec((1, 1, hg, ln), lambda i, j, c: (i, j, 0, c)),
            pc, pr, pc, pr, pc,
            pl.BlockSpec((1, 1, gw), lambda i, j, c: (j, 0, 0)),
        ],
        out_specs=pl.BlockSpec((1, ln, gw), lambda i, j, c: (i, c, j)),
        out_shape=jax.ShapeDtypeStruct((b, s, MIX_WIDTH), BF16),
        scratch_shapes=[pltpu.VMEM((hg, n, SSD_HEAD_DIM), F32), pltpu.VMEM((ln, gw), F32)],
        compiler_params=_cp(("parallel", "parallel", "arbitrary"), 32),
        name="ssd_scan",
    )(xbc, xbc, xbc, proj3, dt_col, dt_row, col(dt_bias), rowv(dt_bias), col(a_log), rowv(a_log),
      col(d_skip), norm_g.reshape(g, 1, gw))


def _mem_attn_kernel(q_ref, kv_ref, o_ref):
    d = MEM_HEAD_DIM
    for h in range(MEM_HEADS):
        q = (q_ref[0, :, h * d:(h + 1) * d].astype(F32) * (d ** -0.5)).astype(BF16)
        k = kv_ref[0, :, h * d:(h + 1) * d].astype(BF16)
        v = kv_ref[0, :, MEM_WIDTH + h * d:MEM_WIDTH + (h + 1) * d].astype(BF16)
        sc = lax.dot_general(q, k, NT_DIMS, preferred_element_type=F32)
        e = jnp.exp(sc - jnp.max(sc, axis=-1, keepdims=True))
        pr = (e / jnp.sum(e, axis=-1, keepdims=True)).astype(BF16)
        o_ref[0, :, h * d:(h + 1) * d] = jnp.dot(
            pr, v, preferred_element_type=F32).astype(o_ref.dtype)


def memory_attention(proj3, q_block, mem_kv):
    b, s, _ = proj3.shape
    m = mem_kv.shape[1]
    tq = 512
    return pl.pallas_call(
        _mem_attn_kernel,
        grid=(b, s // tq),
        in_specs=[
            pl.BlockSpec((1, tq, MEM_WIDTH), lambda i, j: (i, j, q_block)),
            pl.BlockSpec((1, m, 2 * MEM_WIDTH), lambda i, j: (i, 0, 0)),
        ],
        out_specs=pl.BlockSpec((1, tq, MEM_WIDTH), lambda i, j: (i, j, 0)),
        out_shape=jax.ShapeDtypeStruct((b, s, MEM_WIDTH), BF16),
        compiler_params=_cp(("parallel", "parallel"), 32),
        name="memory_attention",
    )(proj3, mem_kv)


def _outproj_kernel(x_ref, a_ref, m_ref, wa_ref, wm_ref, o_ref):
    acc = jnp.dot(a_ref[...], wa_ref[...].astype(BF16), preferred_element_type=F32)
    acc = acc + jnp.dot(m_ref[...], wm_ref[...].astype(BF16), preferred_element_type=F32)
    o_ref[...] = x_ref[...] + acc


def out_projection(x, mix, mem_out, w_out, li):
    m, d = x.shape
    tm, tn = 2048, 512
    return pl.pallas_call(
        _outproj_kernel,
        grid=(m // tm, d // tn),
        in_specs=[
            pl.BlockSpec((tm, tn), lambda i, j: (i, j)),
            pl.BlockSpec((tm, MIX_WIDTH), lambda i, j: (i, 0)),
            pl.BlockSpec((tm, MEM_WIDTH), lambda i, j: (i, 0)),
            pl.BlockSpec((None, MIX_WIDTH, tn), lambda i, j: (li, 0, j)),
            pl.BlockSpec((None, MEM_WIDTH, tn), lambda i, j: (li, MIX_WIDTH // MEM_WIDTH, j)),
        ],
        out_specs=pl.BlockSpec((tm, tn), lambda i, j: (i, j)),
        out_shape=jax.ShapeDtypeStruct((m, d), F32),
        compiler_params=_cp(("parallel", "parallel"), 48),
        name="out_projection",
    )(x, mix, mem_out, w_out, w_out)


def _router_kernel(x_ref, g_ref, wr_ref, br_ref, r_ref, seg_ref):
    x = x_ref[...]
    ms = jnp.mean(x * x, axis=-1, keepdims=True)
    t = x * lax.rsqrt(ms + RMS_EPS) * g_ref[...]
    th = t.astype(BF16)
    tl = (t - th.astype(F32)).astype(BF16)
    w = wr_ref[...]
    wh = w.astype(BF16)
    wl = (w - wh.astype(F32)).astype(BF16)
    lg = (lax.dot_general(wh, th, NT_DIMS, preferred_element_type=F32)
          + lax.dot_general(wh, tl, NT_DIMS, preferred_element_type=F32)
          + lax.dot_general(wl, th, NT_DIMS, preferred_element_type=F32)) + br_ref[...]
    ng, epg = N_EXPERT_GROUPS, EXPERTS_PER_GROUP
    gl = [lg[j:j + 1] for j in range(ng)]
    el = [lg[ng + j:ng + j + 1] for j in range(N_EXPERTS)]

    def first_argmax(vals):
        top = functools.reduce(jnp.maximum, vals)
        idx = jnp.full(top.shape, len(vals) - 1, jnp.int32)
        for j in range(len(vals) - 2, -1, -1):
            idx = jnp.where(vals[j] >= top, j, idx)
        return top, idx

    gmax, gsel = first_argmax(gl)
    g_w = 1.0 / functools.reduce(lambda a, b: a + b, [jnp.exp(v - gmax) for v in gl])
    e_in = []
    for j in range(epg):
        v = el[(ng - 1) * epg + j]
        for gi in range(ng - 2, -1, -1):
            v = jnp.where(gsel == gi, el[gi * epg + j], v)
        e_in.append(v)
    emax = functools.reduce(jnp.maximum, e_in)
    pe = [jnp.exp(v - emax) for v in e_in]
    se = functools.reduce(lambda a, b: a + b, pe)
    prob = [v / se for v in pe]
    v1, i1 = first_argmax(prob)
    rest = [jnp.where(i1 == j, -1.0, prob[j]) for j in range(epg)]
    v2, i2 = first_argmax(rest)
    tot = v1 + v2
    e1 = gsel * epg + i1
    e2 = gsel * epg + i2

    tm = x.shape[0]
    eidx = lax.broadcasted_iota(jnp.int32, (N_EXPERTS, tm), 0)
    hit1 = eidx == e1
    hit2 = eidx == e2
    onehot = jnp.where(hit1 | hit2, 1.0, 0.0)
    tok_r = lax.broadcasted_iota(jnp.int32, (tm, tm), 0)
    tok_c = lax.broadcasted_iota(jnp.int32, (tm, tm), 1)
    earlier = jnp.where(tok_r < tok_c, 1.0, 0.0).astype(BF16)
    rank = jnp.dot(onehot.astype(BF16), earlier, preferred_element_type=F32)
    cnt = jnp.sum(onehot, axis=1, keepdims=True).astype(jnp.int32)
    seg = jnp.bitwise_and(cnt + (SEG_ALIGN - 1), -SEG_ALIGN)
    ex_r = lax.broadcasted_iota(jnp.int32, (N_EXPERTS, N_EXPERTS), 0)
    ex_c = lax.broadcasted_iota(jnp.int32, (N_EXPERTS, N_EXPERTS), 1)
    lower = jnp.where(ex_c < ex_r, 1.0, 0.0).astype(BF16)
    seg_lanes = jnp.broadcast_to(seg.astype(F32), (N_EXPERTS, 128))
    start = jnp.dot(lower, seg_lanes.astype(BF16), preferred_element_type=F32)[:, 0:1]
    slot = start + rank
    lpos1 = jnp.sum(jnp.where(hit1, slot, 0.0), axis=0, keepdims=True)
    lpos2 = jnp.sum(jnp.where(hit2, slot, 0.0), axis=0, keepdims=True)
    r_ref[...] = jnp.concatenate(
        [e1.astype(F32), e2.astype(F32), v1 / tot * g_w, v2 / tot * g_w, lpos1, lpos2,
         jnp.zeros((2, tm), F32)], axis=0)
    seg_ref[0] = jnp.broadcast_to(seg, (N_EXPERTS, 128))


def moe_router(x, g, w_group, b_group, w_expert, b_expert):
    m, d = x.shape
    tm = MOE_SUB
    nr = 32
    wr = jnp.zeros((nr, d), F32).at[:N_EXPERT_GROUPS].set(w_group.T)
    wr = wr.at[N_EXPERT_GROUPS:N_EXPERT_GROUPS + N_EXPERTS].set(w_expert.T)
    br = jnp.zeros((nr, 1), F32).at[:N_EXPERT_GROUPS, 0].set(b_group)
    br = br.at[N_EXPERT_GROUPS:N_EXPERT_GROUPS + N_EXPERTS, 0].set(b_expert)
    route, seg3 = pl.pallas_call(
        _router_kernel,
        grid=(m // tm,),
        in_specs=[
            pl.BlockSpec((tm, d), lambda i: (i, 0)),
            pl.BlockSpec((1, d), lambda i: (0, 0)),
            pl.BlockSpec((nr, d), lambda i: (0, 0)),
            pl.BlockSpec((nr, 1), lambda i: (0, 0)),
        ],
        out_specs=[pl.BlockSpec((8, tm), lambda i: (0, i)),
                   pl.BlockSpec((1, N_EXPERTS, 128), lambda i: (i, 0, 0))],
        out_shape=[jax.ShapeDtypeStruct((8, m), F32),
                   jax.ShapeDtypeStruct((m // tm, N_EXPERTS, 128), jnp.int32)],
        compiler_params=_cp(("parallel",), 40),
        name="moe_router",
    )(x, g.reshape(1, d), wr, br)
    return route, seg3[:, :, 0]


def moe_plan(seg, m):
    nsub = m // MOE_SUB
    tm = MOE_TILE
    loc_off = jnp.cumsum(seg, axis=1) - seg
    reg_off = jnp.cumsum(seg, axis=0) - seg
    length = jnp.sum(seg, axis=0)
    padded = (length + tm - 1) // tm * tm
    e_end = jnp.cumsum(padded)
    e_start = e_end - padded
    glob_off = e_start[None, :] + reg_off
    n_slots = (2 * m + nsub * N_EXPERTS * (SEG_ALIGN - 1) + N_EXPERTS * (tm - 1) + tm - 1) // tm * tm
    tile_start = jnp.arange(n_slots // tm, dtype=jnp.int32) * tm
    tile_expert = jnp.minimum(
        jnp.sum((tile_start[:, None] >= e_end[None, :]).astype(jnp.int32), axis=1), N_EXPERTS - 1)
    return dict(seg=seg.reshape(-1), loc_off=loc_off.reshape(-1), glob_off=glob_off.reshape(-1),
                n_slots=n_slots, tile_expert=tile_expert,
                n_used=(e_end[-1:] // tm).astype(jnp.int32),
                pad_start=jnp.concatenate([e_start + length, e_end[-1:]]).astype(jnp.int32),
                pad_len=jnp.concatenate([padded - length, n_slots - e_end[-1:]]).astype(jnp.int32))


def _aligned(v):
    return v if isinstance(v, int) else pl.multiple_of(v, SEG_ALIGN)


def _seg_copy(src_ref, dst_ref, sem, src_row, dst_row, n_rows):
    n_rows = _aligned(n_rows)
    return pltpu.make_async_copy(src_ref.at[pl.ds(_aligned(src_row), n_rows)],
                                 dst_ref.at[pl.ds(_aligned(dst_row), n_rows)], sem)


def _segment_copies(step, seg_ref, loc_ref, glob_ref, make):
    out = []
    for e in range(N_EXPERTS):
        k = step * N_EXPERTS + e
        out.append((seg_ref[k], make(loc_ref[k], glob_ref[k], seg_ref[k])))
    return out


def _start_all(copies):
    for n, cp in copies:
        @pl.when(n > 0)
        def _(cp=cp):
            cp.start()


def _wait_all(copies):
    for n, cp in copies:
        @pl.when(n > 0)
        def _(cp=cp):
            cp.wait()


def _dispatch_kernel(seg_ref, loc_ref, glob_ref, pst_ref, pln_ref, x_ref, g_ref, rt_ref,
                     xs_ref, cbuf_ref, zbuf_ref, sems):
    s = pl.program_id(0)
    n_steps = pl.num_programs(0)
    buf = s % 2
    d = x_ref.shape[1]

    def copies(step, b):
        return _segment_copies(
            step, seg_ref, loc_ref, glob_ref,
            lambda loc, glob, n: _seg_copy(cbuf_ref.at[b], xs_ref, sems.at[b], loc, glob, n))

    @pl.when(s >= 2)
    def _():
        _wait_all(copies(s - 2, buf))

    x = x_ref[...]
    ms = jnp.mean(x * x, axis=-1, keepdims=True)
    t = (x * lax.rsqrt(ms + RMS_EPS) * g_ref[...]).astype(BF16)
    rt = rt_ref[...]
    lp = rt[4:6].astype(jnp.int32)
    slot = lax.broadcasted_iota(jnp.int32, (MOE_CBUF, MOE_SUB), 0)
    hit0 = slot == lp[0:1]
    hit1 = slot == lp[1:2]
    onehot = jnp.where(hit0 | hit1, 1.0, 0.0).astype(BF16)
    cbuf_ref[buf, :, 0:d] = jnp.dot(onehot, t, preferred_element_type=F32)
    wslot = jnp.sum(jnp.where(hit0, rt[2:3], 0.0) + jnp.where(hit1, rt[3:4], 0.0),
                    axis=1, keepdims=True)
    cbuf_ref[buf, :, d:] = jnp.broadcast_to(wslot, (MOE_CBUF, MOE_WCOLS))
    _start_all(copies(s, buf))

    @pl.when(s == 0)
    def _():
        zsem = sems.at[2]
        zbuf_ref[...] = jnp.zeros(zbuf_ref.shape, F32)
        pads = [(pln_ref[e], _seg_copy(zbuf_ref, xs_ref, zsem, 0, pst_ref[e], pln_ref[e]))
                for e in range(N_EXPERTS)]
        _start_all(pads)
        _wait_all(pads)
        tail_tiles = pln_ref[N_EXPERTS] // MOE_TILE

        def tail_copy(k):
            return _seg_copy(zbuf_ref, xs_ref, zsem, 0, pst_ref[N_EXPERTS] + k * MOE_TILE, MOE_TILE)

        def start(k, carry):
            tail_copy(k).start()
            return carry

        def wait(k, carry):
            tail_copy(k).wait()
            return carry
        lax.fori_loop(0, tail_tiles, start, 0)
        lax.fori_loop(0, tail_tiles, wait, 0)

    @pl.when(s == n_steps - 1)
    def _():
        @pl.when(s >= 1)
        def _():
            _wait_all(copies(s - 1, 1 - buf))
        _wait_all(copies(s, buf))


def moe_dispatch(x, g, route, plan):
    m, d = x.shape
    dw = d + MOE_WCOLS
    return pl.pallas_call(
        _dispatch_kernel,
        grid_spec=pltpu.PrefetchScalarGridSpec(
            num_scalar_prefetch=5,
            grid=(m // MOE_SUB,),
            in_specs=[
                pl.BlockSpec((MOE_SUB, d), lambda i, *_: (i, 0)),
                pl.BlockSpec((1, d), lambda i, *_: (0, 0)),
                pl.BlockSpec((8, MOE_SUB), lambda i, *_: (0, i)),
            ],
            out_specs=pl.BlockSpec(memory_space=pl.ANY),
            scratch_shapes=[pltpu.VMEM((2, MOE_CBUF, dw), F32), pltpu.VMEM((MOE_TILE, dw), F32),
                            pltpu.SemaphoreType.DMA((3,))],
        ),
        out_shape=jax.ShapeDtypeStruct((plan["n_slots"], dw), F32),
        compiler_params=pltpu.CompilerParams(dimension_semantics=("arbitrary",),
                                             vmem_limit_bytes=40 * MIB, has_side_effects=True),
        name="moe_dispatch",
    )(plan["seg"], plan["loc_off"], plan["glob_off"], plan["pad_start"], plan["pad_len"],
      x, g.reshape(1, d), route)


def _ffn_kernel(te_ref, nu_ref, x_ref, wg_ref, wu_ref, wd_ref, o_ref,
                wgb_ref, wub_ref, wdb_ref):
    i = pl.program_id(0)
    d = o_ref.shape[1]
    prev = te_ref[jnp.maximum(i - 1, 0)]

    @pl.when(i < nu_ref[0])
    def _():
        @pl.when((i == 0) | (te_ref[i] != prev))
        def _():
            wgb_ref[...] = wg_ref[...].astype(BF16)
            wub_ref[...] = wu_ref[...].astype(BF16)
            wdb_ref[...] = wd_ref[...].astype(BF16)

        x = x_ref[:, 0:d].astype(BF16)
        gate = jnp.dot(x, wgb_ref[...], preferred_element_type=F32)
        up = jnp.dot(x, wub_ref[...], preferred_element_type=F32)
        hid = (gate * _sigmoid(gate) * up * x_ref[:, d:d + 1]).astype(BF16)
        o_ref[...] = jnp.dot(hid, wdb_ref[...], preferred_element_type=F32)

    @pl.when(i >= nu_ref[0])
    def _():
        o_ref[...] = jnp.zeros(o_ref.shape, F32)


def grouped_ffn(xs, plan, w_gate, w_up, w_down, li):
    p, dw = xs.shape
    d = dw - MOE_WCOLS
    f = D_EXPERT
    tm = MOE_TILE
    row = lambda i, te, nu: (jnp.minimum(i, nu[0] - 1), 0)
    wmap = lambda i, te, nu: (li, te[jnp.minimum(i, nu[0] - 1)], 0, 0)
    return pl.pallas_call(
        _ffn_kernel,
        grid_spec=pltpu.PrefetchScalarGridSpec(
            num_scalar_prefetch=2,
            grid=(p // tm,),
            in_specs=[
                pl.BlockSpec((tm, dw), row),
                pl.BlockSpec((None, None, d, f), wmap),
                pl.BlockSpec((None, None, d, f), wmap),
                pl.BlockSpec((None, None, f, d), wmap),
            ],
            out_specs=pl.BlockSpec((tm, d), lambda i, te, nu: (i, 0)),
            scratch_shapes=[pltpu.VMEM((d, f), BF16), pltpu.VMEM((d, f), BF16),
                            pltpu.VMEM((f, d), BF16)],
        ),
        out_shape=jax.ShapeDtypeStruct((p, d), F32),
        compiler_params=_cp(("arbitrary",), 48),
        name="grouped_ffn",
    )(plan["tile_expert"], plan["n_used"], xs, w_gate, w_up, w_down)


def _combine_kernel(seg_ref, loc_ref, glob_ref, x_ref, lp_ref, ys_ref, *rest, final_norm):
    if final_norm:
        g_ref, o_ref, ybuf_ref, sems = rest
    else:
        o_ref, ybuf_ref, sems = rest
    s = pl.program_id(0)
    n_steps = pl.num_programs(0)
    buf = s % 2

    def copies(step, b):
        return _segment_copies(
            step, seg_ref, loc_ref, glob_ref,
            lambda loc, glob, n: _seg_copy(ys_ref, ybuf_ref.at[b], sems.at[b], glob, loc, n))

    def fetch(step, b):
        ybuf_ref[b] = jnp.zeros(ybuf_ref.shape[1:], F32)
        _start_all(copies(step, b))

    @pl.when(s == 0)
    def _():
        fetch(s, buf)

    @pl.when(s + 1 < n_steps)
    def _():
        fetch(s + 1, 1 - buf)

    _wait_all(copies(s, buf))
    lp = lp_ref[...]
    slot = lax.broadcasted_iota(jnp.int32, (MOE_SUB, MOE_CBUF), 1)
    onehot = jnp.where((slot == lp[:, 0:1]) | (slot == lp[:, 1:2]), 1.0, 0.0).astype(BF16)
    y = ybuf_ref[buf]
    hi = y.astype(BF16)
    lo = (y - hi.astype(F32)).astype(BF16)
    out = (x_ref[...] + jnp.dot(onehot, hi, preferred_element_type=F32)
           + jnp.dot(onehot, lo, preferred_element_type=F32))
    if final_norm:
        ms = jnp.mean(out * out, axis=-1, keepdims=True)
        out = out * lax.rsqrt(ms + RMS_EPS) * g_ref[...]
    o_ref[...] = out


def moe_combine(x, ys, route, plan, final_g=None):
    m, d = x.shape
    in_specs = [
        pl.BlockSpec((MOE_SUB, d), lambda i, *_: (i, 0)),
        pl.BlockSpec((MOE_SUB, 2), lambda i, *_: (i, 0)),
        pl.BlockSpec(memory_space=pl.ANY),
    ]
    args = [x, route[4:6].T.astype(jnp.int32), ys]
    if final_g is not None:
        in_specs.append(pl.BlockSpec((1, d), lambda i, *_: (0, 0)))
        args.append(final_g.reshape(1, d))
    return pl.pallas_call(
        functools.partial(_combine_kernel, final_norm=final_g is not None),
        grid_spec=pltpu.PrefetchScalarGridSpec(
            num_scalar_prefetch=3,
            grid=(m // MOE_SUB,),
            in_specs=in_specs,
            out_specs=pl.BlockSpec((MOE_SUB, d), lambda i, *_: (i, 0)),
            scratch_shapes=[pltpu.VMEM((2, MOE_CBUF, d), F32), pltpu.SemaphoreType.DMA((2,))],
        ),
        out_shape=jax.ShapeDtypeStruct((m, d), F32),
        compiler_params=pltpu.CompilerParams(dimension_semantics=("arbitrary",),
                                             vmem_limit_bytes=40 * MIB),
        name="moe_combine",
    )(plan["seg"], plan["loc_off"], plan["glob_off"], *args)


def hier_moe_layer(x, norm_g, w_group, b_group, w_expert, b_expert, w_gate, w_up, w_down, li,
                   final_g=None):
    m, _ = x.shape
    route, seg = moe_router(x, norm_g, w_group, b_group, w_expert, b_expert)
    plan = moe_plan(seg, m)
    xs = moe_dispatch(x, norm_g, route, plan)
    ys = grouped_ffn(xs, plan, w_gate, w_up, w_down, li)
    return moe_combine(x, ys, route, plan, final_g)


def _rope_tables(seq, dim):
    pos = jnp.arange(seq, dtype=F32)
    inv = ROPE_THETA ** (-jnp.arange(0, dim, 2, dtype=F32) / dim)
    ang = pos[:, None] * inv[None, :]
    ang = jnp.concatenate([ang, ang], axis=-1)
    sign = jnp.concatenate([-jnp.ones((dim // 2,), F32), jnp.ones((dim // 2,), F32)])
    return jnp.cos(ang), jnp.sin(ang) * sign[None, :]


def _lambda_init(depth_idx):
    return 0.8 - 0.6 * math.exp(-0.3 * depth_idx)


def kernel(x, mem, norm_mix, norm_mem, norm_ffn, norm_final, w_out, w_mem_kv, pool_w_in, pool_w_grp, pool_scale, diff_w_in, diff_lambda, diff_subln, ssd_w_in, ssd_conv_w, ssd_conv_b, ssd_dt_bias, ssd_a_log, ssd_d, ssd_norm, moe_w_group, moe_b_group, moe_w_expert, moe_b_expert, moe_w_gate, moe_w_up, moe_w_down):
    b, s, d = x.shape
    m = b * s
    mem_len = mem.shape[1]
    cos, sin_signed = _rope_tables(s, DIFF_HEAD_DIM)
    xt = x.reshape(m, d)
    memt = mem.reshape(b * mem_len, d)
    for i in range(DEPTH):
        kind, slot = i % N_MIXERS, i // N_MIXERS
        mem_kv = norm_matmul(memt, norm_mem[i], w_mem_kv, i, 2 * MEM_WIDTH, b * mem_len, 512, F32)
        mem_kv = mem_kv.reshape(b, mem_len, 2 * MEM_WIDTH)
        if kind == 0:
            n_in = MIX_WIDTH + MEM_WIDTH
            proj = norm_matmul(xt, norm_mix[i], pool_w_in, slot, n_in, IN_TM, IN_TN, BF16)
            proj = proj.reshape(b, s, n_in)
            mix = pool_mixer(proj, pool_w_grp, slot, pool_scale[slot])
        elif kind == 1:
            n_in = 3 * MIX_WIDTH + MEM_WIDTH
            proj = norm_matmul(xt, norm_mix[i], diff_w_in, slot, n_in, IN_TM, IN_TN, BF16,
                               rope=(cos, sin_signed, MIX_WIDTH, MIX_WIDTH, s))
            proj = proj.reshape(b, s, n_in)
            mix = diff_attention(proj, diff_lambda[slot], diff_subln[slot], _lambda_init(i))
        else:
            w = ssd_w_in[slot]
            n_main = MIX_WIDTH + SSD_CONV_DIM
            w = jnp.concatenate(
                [w[:, :n_main], w[:, SSD_MIX_IN:], w[:, n_main:SSD_MIX_IN],
                 jnp.zeros((d, SSD_IN_PADDED - w.shape[1]), F32)], axis=1)
            n_in = SSD_IN_PADDED
            proj = norm_matmul(xt, norm_mix[i], w[None], 0, n_in, IN_TM, IN_TN, F32)
            proj = proj.reshape(b, s, n_in)
            xbc = ssd_conv(proj, ssd_conv_w[slot], ssd_conv_b[slot])
            dt_off = n_main + MEM_WIDTH
            dt_raw = proj[:, :, dt_off:dt_off + SSD_HEADS]
            mix = ssd_scan(xbc, proj, dt_raw, ssd_dt_bias[slot], ssd_a_log[slot], ssd_d[slot],
                           ssd_norm[slot])
        q_block = (MIX_WIDTH if kind == 0 else 3 * MIX_WIDTH if kind == 1
                   else MIX_WIDTH + SSD_CONV_DIM) // MEM_WIDTH
        mem_out = memory_attention(proj, q_block, mem_kv)
        xt = out_projection(xt, mix.reshape(m, MIX_WIDTH), mem_out.reshape(m, MEM_WIDTH), w_out, i)
        xt = hier_moe_layer(xt, norm_ffn[i], moe_w_group[i], moe_b_group[i], moe_w_expert[i],
                            moe_b_expert[i], moe_w_gate, moe_w_up, moe_w_down, i,
                            final_g=norm_final if i == DEPTH - 1 else None)
    return xt.reshape(b, s, d)
```

```python
import functools
import math

import jax
import jax.numpy as jnp
from jax import lax
from jax.experimental import pallas as pl
from jax.experimental.pallas import tpu as pltpu

F32 = jnp.float32
BF16 = jnp.bfloat16

D_MODEL = 2048
DEPTH = 4
N_MIXERS = 3
MIX_WIDTH = 1536
MEM_WIDTH = 512
MEM_HEADS = 4
MEM_HEAD_DIM = 128
POOL_WINDOWS = (2, 4, 8, 16)
POOL_GROUPS = 4
POOL_GROUP_DIM = 384
DIFF_HEAD_DIM = 128
DIFF_HEADS = 6
DIFF_V_DIM = 256
ROPE_THETA = 10000.0
SSD_HEAD_DIM = 64
SSD_HEADS = 24
SSD_GROUPS = 4
SSD_HEADS_PER_GROUP = 6
SSD_STATE = 128
SSD_CONV = 4
SSD_CHUNK = 128
SSD_CONV_DIM = 2560
SSD_MIX_IN = 4120
SSD_IN_PADDED = 5120
N_EXPERT_GROUPS = 4
EXPERTS_PER_GROUP = 4
N_EXPERTS = 16
D_EXPERT = 512
RMS_EPS = 1e-6

IN_TM, IN_TN = 2048, 256
MOE_TILE = 512
MOE_SUB = 512
SEG_ALIGN = 16
MOE_CBUF = 2 * MOE_SUB + 256
MOE_WCOLS = 256
MIB = 1024 * 1024

NT_DIMS = (((1,), (1,)), ((), ()))
TN_DIMS = (((0,), (0,)), ((), ()))


def _cp(sem, vmem_mib):
    return pltpu.CompilerParams(dimension_semantics=sem, vmem_limit_bytes=vmem_mib * MIB)


def _sigmoid(x):
    return 1.0 / (1.0 + jnp.exp(-x))


def _softplus(x):
    return jnp.maximum(x, 0.0) + jnp.log1p(jnp.exp(-jnp.abs(x)))


def _split3(v):
    hi = v.astype(BF16)
    r = v - hi.astype(F32)
    mid = r.astype(BF16)
    lo = (r - mid.astype(F32)).astype(BF16)
    return hi, mid, lo


def _norm_matmul_kernel(x_ref, g_ref, w_ref, *rest, rope_tiles):
    if rope_tiles is None:
        o_ref, h_ref = rest
    else:
        cos_ref, sin_ref, o_ref, h_ref = rest
    j = pl.program_id(1)

    @pl.when(j == 0)
    def _():
        x = x_ref[...]
        ms = jnp.mean(x * x, axis=-1, keepdims=True)
        h_ref[...] = (x * lax.rsqrt(ms + RMS_EPS) * g_ref[...]).astype(BF16)

    acc = jnp.dot(h_ref[...], w_ref[...].astype(BF16), preferred_element_type=F32)
    if rope_tiles is None:
        o_ref[...] = acc.astype(o_ref.dtype)
        return
    n_q, n_k = rope_tiles
    hd = DIFF_HEAD_DIM

    @pl.when(j < n_q + n_k)
    def _():
        scale = jnp.where(j < n_q, hd ** -0.5, 1.0)
        cos = cos_ref[...]
        sin = sin_ref[...]
        for c in range(o_ref.shape[1] // hd):
            x = acc[:, c * hd:(c + 1) * hd]
            r = x * cos + pltpu.roll(x, hd // 2, axis=1) * sin
            o_ref[:, c * hd:(c + 1) * hd] = (r * scale).astype(o_ref.dtype)

    @pl.when(j >= n_q + n_k)
    def _():
        o_ref[...] = acc.astype(o_ref.dtype)


def norm_matmul(x, g, w, li, n_cols, tm, tn, out_dtype, rope=None):
    m, k = x.shape
    in_specs = [
        pl.BlockSpec((tm, k), lambda i, j: (i, 0)),
        pl.BlockSpec((1, k), lambda i, j: (0, 0)),
        pl.BlockSpec((None, k, tn), lambda i, j: (li, 0, j)),
    ]
    args = [x, g.reshape(1, k), w]
    rope_tiles = None
    if rope is not None:
        cos, sin_signed, n_q, n_k, seq = rope
        rope_tiles = (n_q // tn, n_k // tn)
        pos_blocks = seq // tm
        tab = pl.BlockSpec((tm, DIFF_HEAD_DIM), lambda i, j: (i % pos_blocks, 0))
        in_specs += [tab, tab]
        args += [cos, sin_signed]
    return pl.pallas_call(
        functools.partial(_norm_matmul_kernel, rope_tiles=rope_tiles),
        grid=(m // tm, n_cols // tn),
        in_specs=in_specs,
        out_specs=pl.BlockSpec((tm, tn), lambda i, j: (i, j)),
        out_shape=jax.ShapeDtypeStruct((m, n_cols), out_dtype),
        scratch_shapes=[pltpu.VMEM((tm, k), BF16)],
        compiler_params=_cp(("parallel", "arbitrary"), 58),
        name="norm_matmul",
    )(*args)


def _pool_kernel(u_ref, w_ref, sc_ref, o_ref, pad_ref):
    grp = pl.program_id(1)
    s, c = u_ref.shape[1], u_ref.shape[2]
    rows = 256
    pad_ref[0:16, :] = jnp.zeros((16, c), F32)
    pad_ref[16:, :] = u_ref[0].astype(F32)
    wb = w_ref[...].astype(BF16)
    sc = sc_ref[0]

    for gi, win in enumerate(POOL_WINDOWS):
        @pl.when(grp == gi)
        def _(win=win):
            for r in range(s // rows):
                xh = pad_ref[r * rows:r * rows + rows + 16, :]
                acc = xh
                k = 1
                while k < win:
                    acc = acc + pltpu.roll(acc, k, axis=0)
                    k *= 2
                t = r * rows + lax.broadcasted_iota(jnp.int32, (rows, 1), 0)
                cnt = jnp.minimum(t + 1, win).astype(F32)
                mixed = (acc[16:, :] / cnt - xh[16:, :]).astype(BF16)
                o_ref[0, r * rows:(r + 1) * rows, :] = (
                    jnp.dot(mixed, wb, preferred_element_type=F32) * sc).astype(o_ref.dtype)


def pool_mixer(proj3, w_grp, li, scale):
    b, s, _ = proj3.shape
    c = POOL_GROUP_DIM
    return pl.pallas_call(
        _pool_kernel,
        grid=(b, POOL_GROUPS),
        in_specs=[
            pl.BlockSpec((1, s, c), lambda i, g: (i, 0, g)),
            pl.BlockSpec((None, None, c, c), lambda i, g: (li, g, 0, 0)),
            pl.BlockSpec((1, 1, c), lambda i, g: (g, 0, 0)),
        ],
        out_specs=pl.BlockSpec((1, s, c), lambda i, g: (i, 0, g)),
        out_shape=jax.ShapeDtypeStruct((b, s, MIX_WIDTH), BF16),
        scratch_shapes=[pltpu.VMEM((s + 16, c), F32)],
        compiler_params=_cp(("parallel", "parallel"), 40),
        name="pool_mixer",
    )(proj3, w_grp, scale.reshape(POOL_GROUPS, 1, c))


def _diff_attn_kernel(lam_ref, sub_ref, q_ref, k_ref, v_ref, o_ref, *, lambda_init):
    lam = lam_ref[...]
    s1 = jnp.sum(lam[0:1] * lam[1:2], axis=-1, keepdims=True)
    s2 = jnp.sum(lam[2:3] * lam[3:4], axis=-1, keepdims=True)
    lmbda = jnp.exp(s1) - jnp.exp(s2) + lambda_init
    s = q_ref.shape[1]
    tq = 512
    d = DIFF_HEAD_DIM
    for i in range(s // tq):
        kv = (i + 1) * tq
        q = q_ref[0, i * tq:(i + 1) * tq, :]
        row = i * tq + lax.broadcasted_iota(jnp.int32, (tq, kv), 0)
        col = lax.broadcasted_iota(jnp.int32, (tq, kv), 1)
        mask = col <= row
        probs = []
        for c in range(2):
            sc = lax.dot_general(q[:, c * d:(c + 1) * d], k_ref[0, 0:kv, c * d:(c + 1) * d],
                                 NT_DIMS, preferred_element_type=F32)
            sc = jnp.where(mask, sc, -jnp.inf)
            e = jnp.exp(sc - jnp.max(sc, axis=-1, keepdims=True))
            probs.append(e / jnp.sum(e, axis=-1, keepdims=True))
        a = (probs[0] - lmbda * probs[1]).astype(BF16)
        o = jnp.dot(a, v_ref[0, 0:kv, :], preferred_element_type=F32)
        ms = jnp.mean(o * o, axis=-1, keepdims=True)
        o_ref[0, i * tq:(i + 1) * tq, :] = (
            o * lax.rsqrt(ms + RMS_EPS) * sub_ref[...] * (1.0 - lambda_init)).astype(o_ref.dtype)


def diff_attention(proj3, lam, subln, lambda_init):
    b, s, _ = proj3.shape
    vd = DIFF_V_DIM
    nh = DIFF_HEADS
    return pl.pallas_call(
        functools.partial(_diff_attn_kernel, lambda_init=lambda_init),
        grid=(b, nh),
        in_specs=[
            pl.BlockSpec((4, DIFF_HEAD_DIM), lambda i, h: (0, 0)),
            pl.BlockSpec((1, vd), lambda i, h: (0, 0)),
            pl.BlockSpec((1, s, vd), lambda i, h: (i, 0, h)),
            pl.BlockSpec((1, s, vd), lambda i, h: (i, 0, nh + h)),
            pl.BlockSpec((1, s, vd), lambda i, h: (i, 0, 2 * nh + h)),
        ],
        out_specs=pl.BlockSpec((1, s, vd), lambda i, h: (i, 0, h)),
        out_shape=jax.ShapeDtypeStruct((b, s, MIX_WIDTH), BF16),
        compiler_params=_cp(("parallel", "parallel"), 48),
        name="diff_attention",
    )(lam, subln.reshape(1, vd), proj3, proj3, proj3)


def _conv_kernel(u_ref, w_ref, b_ref, o_ref, pad_ref):
    s, c = u_ref.shape[1], u_ref.shape[2]
    rows = 256
    pad_ref[0:8, :] = jnp.zeros((8, c), F32)
    pad_ref[8:, :] = u_ref[0]
    w = w_ref[...]
    bias = b_ref[...]
    for r in range(s // rows):
        xh = pad_ref[r * rows:r * rows + rows + 8, :]
        y = xh * w[3:4]
        for j in range(1, SSD_CONV):
            y = y + pltpu.roll(xh, j, axis=0) * w[SSD_CONV - 1 - j:SSD_CONV - j]
        y = y[8:, :] + bias
        o_ref[0, r * rows:(r + 1) * rows, :] = y * _sigmoid(y)


def ssd_conv(proj3, conv_w, conv_b):
    b, s, _ = proj3.shape
    tc = 512
    off = MIX_WIDTH // tc
    return pl.pallas_call(
        _conv_kernel,
        grid=(b, SSD_CONV_DIM // tc),
        in_specs=[
            pl.BlockSpec((1, s, tc), lambda i, j: (i, 0, off + j)),
            pl.BlockSpec((SSD_CONV, tc), lambda i, j: (0, j)),
            pl.BlockSpec((1, tc), lambda i, j: (0, j)),
        ],
        out_specs=pl.BlockSpec((1, s, tc), lambda i, j: (i, 0, j)),
        out_shape=jax.ShapeDtypeStruct((b, s, SSD_CONV_DIM), F32),
        scratch_shapes=[pltpu.VMEM((s + 8, tc), F32)],
        compiler_params=_cp(("parallel", "parallel"), 48),
        name="ssd_conv",
    )(proj3, conv_w, conv_b.reshape(1, SSD_CONV_DIM))


def _ssd_kernel(x_ref, b_ref, c_ref, z_ref, dtc_ref, dtr_ref, bc_ref, br_ref, ac_ref, ar_ref,
                d_ref, ng_ref, o_ref, state_ref, y_ref):
    @pl.when(pl.program_id(2) == 0)
    def _():
        state_ref[...] = jnp.zeros(state_ref.shape, F32)

    ln = SSD_CHUNK
    p = SSD_HEAD_DIM
    x = x_ref[0]
    bm = b_ref[0]
    cm = c_ref[0].astype(BF16)
    row = lax.broadcasted_iota(jnp.int32, (ln, ln), 0)
    col = lax.broadcasted_iota(jnp.int32, (ln, ln), 1)
    causal = col <= row
    ones_lower = jnp.where(causal, 1.0, 0.0).astype(BF16)
    ones_upper = jnp.where(row <= col, 1.0, 0.0).astype(BF16)

    dtc = _softplus(dtc_ref[0, 0] + bc_ref[0])
    da_c = dtc * (-jnp.exp(ac_ref[0]))
    acs_c = sum(jnp.dot(ones_lower, t, preferred_element_type=F32) for t in _split3(da_c))
    dtr = _softplus(dtr_ref[0, 0] + br_ref[0])
    da_r = dtr * (-jnp.exp(ar_ref[0]))
    acs_r = sum(jnp.dot(t, ones_upper, preferred_element_type=F32) for t in _split3(da_r))

    scores = lax.dot_general(cm, bm.astype(BF16), NT_DIMS, preferred_element_type=F32)
    dsk = d_ref[0]
    for h in range(SSD_HEADS_PER_GROUP):
        ac = acs_c[:, h:h + 1]
        ar = acs_r[h:h + 1, :]
        decay = jnp.exp(jnp.where(causal, ac - ar, -jnp.inf))
        xh = x[:, h * p:(h + 1) * p]
        xc = (xh * dtc[:, h:h + 1]).astype(BF16)
        y = jnp.dot((scores * decay).astype(BF16), xc, preferred_element_type=F32)
        prev = state_ref[h]
        y = y + jnp.dot(cm, prev.astype(BF16), preferred_element_type=F32) * jnp.exp(ac)
        a_last = ac[ln - 1:ln, :]
        bs = (bm * jnp.exp(a_last - ac)).astype(BF16)
        st = lax.dot_general(bs, xc, TN_DIMS, preferred_element_type=F32)
        state_ref[h] = prev * jnp.exp(a_last) + st
        y_ref[:, h * p:(h + 1) * p] = y + dsk[:, h:h + 1] * xh

    z = z_ref[0]
    yz = y_ref[...] * (z * _sigmoid(z))
    ms = jnp.mean(yz * yz, axis=-1, keepdims=True)
    o_ref[0] = (yz * lax.rsqrt(ms + RMS_EPS) * ng_ref[0]).astype(o_ref.dtype)


def ssd_scan(xbc, proj3, dt_raw, dt_bias, a_log, d_skip, norm_g):
    b, s, _ = xbc.shape
    ln, g, hg, n = SSD_CHUNK, SSD_GROUPS, SSD_HEADS_PER_GROUP, SSD_STATE
    gw = hg * SSD_HEAD_DIM
    dt4 = dt_raw.reshape(b, s, g, hg)
    dt_col = dt4.transpose(0, 2, 1, 3)
    dt_row = dt4.transpose(0, 2, 3, 1)
    col = lambda v: v.reshape(g, 1, hg)
    rowv = lambda v: v.reshape(g, hg, 1)
    pc = pl.BlockSpec((1, 1, hg), lambda i, j, c: (j, 0, 0))
    pr = pl.BlockSpec((1, hg, 1), lambda i, j, c: (j, 0, 0))
    return pl.pallas_call(
        _ssd_kernel,
        grid=(b, g, s // ln),
        in_specs=[
            pl.BlockSpec((1, ln, gw), lambda i, j, c: (i, c, j)),
            pl.BlockSpec((1, ln, n), lambda i, j, c: (i, c, MIX_WIDTH // n + j)),
            pl.BlockSpec((1, ln, n), lambda i, j, c: (i, c, MIX_WIDTH // n + g + j)),
            pl.BlockSpec((1, ln, gw), lambda i, j, c: (i, c, j)),
            pl.BlockSpec((1, 1, ln, hg), lambda i, j, c: (i, j, c, 0)),
            pl.BlockSpec((1, 1, hg, ln), lambda i, j, c: (i, j, 0, c)),
            pc, pr, pc, pr, pc,
            pl.BlockSpec((1, 1, gw), lambda i, j, c: (j, 0, 0)),
        ],
        out_specs=pl.BlockSpec((1, ln, gw), lambda i, j, c: (i, c, j)),
        out_shape=jax.ShapeDtypeStruct((b, s, MIX_WIDTH), BF16),
        scratch_shapes=[pltpu.VMEM((hg, n, SSD_HEAD_DIM), F32), pltpu.VMEM((ln, gw), F32)],
        compiler_params=_cp(("parallel", "parallel", "arbitrary"), 32),
        name="ssd_scan",
    )(xbc, xbc, xbc, proj3, dt_col, dt_row, col(dt_bias), rowv(dt_bias), col(a_log), rowv(a_log),
      col(d_skip), norm_g.reshape(g, 1, gw))


def _mem_attn_kernel(q_ref, kv_ref, o_ref):
    d = MEM_HEAD_DIM
    for h in range(MEM_HEADS):
        q = (q_ref[0, :, h * d:(h + 1) * d].astype(F32) * (d ** -0.5)).astype(BF16)
        k = kv_ref[0, :, h * d:(h + 1) * d].astype(BF16)
        v = kv_ref[0, :, MEM_WIDTH + h * d:MEM_WIDTH + (h + 1) * d].astype(BF16)
        sc = lax.dot_general(q, k, NT_DIMS, preferred_element_type=F32)
        e = jnp.exp(sc - jnp.max(sc, axis=-1, keepdims=True))
        pr = (e / jnp.sum(e, axis=-1, keepdims=True)).astype(BF16)
        o_ref[0, :, h * d:(h + 1) * d] = jnp.dot(
            pr, v, preferred_element_type=F32).astype(o_ref.dtype)


def memory_attention(proj3, q_block, mem_kv):
    b, s, _ = proj3.shape
    m = mem_kv.shape[1]
    tq = 512
    return pl.pallas_call(
        _mem_attn_kernel,
        grid=(b, s // tq),
        in_specs=[
            pl.BlockSpec((1, tq, MEM_WIDTH), lambda i, j: (i, j, q_block)),
            pl.BlockSpec((1, m, 2 * MEM_WIDTH), lambda i, j: (i, 0, 0)),
        ],
        out_specs=pl.BlockSpec((1, tq, MEM_WIDTH), lambda i, j: (i, j, 0)),
        out_shape=jax.ShapeDtypeStruct((b, s, MEM_WIDTH), BF16),
        compiler_params=_cp(("parallel", "parallel"), 32),
        name="memory_attention",
    )(proj3, mem_kv)


def _outproj_kernel(x_ref, a_ref, m_ref, wa_ref, wm_ref, o_ref):
    acc = jnp.dot(a_ref[...], wa_ref[...].astype(BF16), preferred_element_type=F32)
    acc = acc + jnp.dot(m_ref[...], wm_ref[...].astype(BF16), preferred_element_type=F32)
    o_ref[...] = x_ref[...] + acc


def out_projection(x, mix, mem_out, w_out, li):
    m, d = x.shape
    tm, tn = 2048, 512
    return pl.pallas_call(
        _outproj_kernel,
        grid=(m // tm, d // tn),
        in_specs=[
            pl.BlockSpec((tm, tn), lambda i, j: (i, j)),
            pl.BlockSpec((tm, MIX_WIDTH), lambda i, j: (i, 0)),
            pl.BlockSpec((tm, MEM_WIDTH), lambda i, j: (i, 0)),
            pl.BlockSpec((None, MIX_WIDTH, tn), lambda i, j: (li, 0, j)),
            pl.BlockSpec((None, MEM_WIDTH, tn), lambda i, j: (li, MIX_WIDTH // MEM_WIDTH, j)),
        ],
        out_specs=pl.BlockSpec((tm, tn), lambda i, j: (i, j)),
        out_shape=jax.ShapeDtypeStruct((m, d), F32),
        compiler_params=_cp(("parallel", "parallel"), 48),
        name="out_projection",
    )(x, mix, mem_out, w_out, w_out)


def _router_kernel(x_ref, g_ref, wr_ref, br_ref, r_ref, seg_ref):
    x = x_ref[...]
    ms = jnp.mean(x * x, axis=-1, keepdims=True)
    t = x * lax.rsqrt(ms + RMS_EPS) * g_ref[...]
    th = t.astype(BF16)
    tl = (t - th.astype(F32)).astype(BF16)
    w = wr_ref[...]
    wh = w.astype(BF16)
    wl = (w - wh.astype(F32)).astype(BF16)
    lg = (lax.dot_general(wh, th, NT_DIMS, preferred_element_type=F32)
          + lax.dot_general(wh, tl, NT_DIMS, preferred_element_type=F32)
          + lax.dot_general(wl, th, NT_DIMS, preferred_element_type=F32)) + br_ref[...]
    ng, epg = N_EXPERT_GROUPS, EXPERTS_PER_GROUP
    gl = [lg[j:j + 1] for j in range(ng)]
    el = [lg[ng + j:ng + j + 1] for j in range(N_EXPERTS)]

    def first_argmax(vals):
        top = functools.reduce(jnp.maximum, vals)
        idx = jnp.full(top.shape, len(vals) - 1, jnp.int32)
        for j in range(len(vals) - 2, -1, -1):
            idx = jnp.where(vals[j] >= top, j, idx)
        return top, idx

    gmax, gsel = first_argmax(gl)
    g_w = 1.0 / functools.reduce(lambda a, b: a + b, [jnp.exp(v - gmax) for v in gl])
    e_in = []
    for j in range(epg):
        v = el[(ng - 1) * epg + j]
        for gi in range(ng - 2, -1, -1):
            v = jnp.where(gsel == gi, el[gi * epg + j], v)
        e_in.append(v)
    emax = functools.reduce(jnp.maximum, e_in)
    pe = [jnp.exp(v - emax) for v in e_in]
    se = functools.reduce(lambda a, b: a + b, pe)
    prob = [v / se for v in pe]
    v1, i1 = first_argmax(prob)
    rest = [jnp.where(i1 == j, -1.0, prob[j]) for j in range(epg)]
    v2, i2 = first_argmax(rest)
    tot = v1 + v2
    e1 = gsel * epg + i1
    e2 = gsel * epg + i2

    tm = x.shape[0]
    eidx = lax.broadcasted_iota(jnp.int32, (N_EXPERTS, tm), 0)
    hit1 = eidx == e1
    hit2 = eidx == e2
    onehot = jnp.where(hit1 | hit2, 1.0, 0.0)
    tok_r = lax.broadcasted_iota(jnp.int32, (tm, tm), 0)
    tok_c = lax.broadcasted_iota(jnp.int32, (tm, tm), 1)
    earlier = jnp.where(tok_r < tok_c, 1.0, 0.0).astype(BF16)
    rank = jnp.dot(onehot.astype(BF16), earlier, preferred_element_type=F32)
    cnt = jnp.sum(onehot, axis=1, keepdims=True).astype(jnp.int32)
    seg = jnp.bitwise_and(cnt + (SEG_ALIGN - 1), -SEG_ALIGN)
    ex_r = lax.broadcasted_iota(jnp.int32, (N_EXPERTS, N_EXPERTS), 0)
    ex_c = lax.broadcasted_iota(jnp.int32, (N_EXPERTS, N_EXPERTS), 1)
    lower = jnp.where(ex_c < ex_r, 1.0, 0.0).astype(BF16)
    seg_lanes = jnp.broadcast_to(seg.astype(F32), (N_EXPERTS, 128))
    start = jnp.dot(lower, seg_lanes.astype(BF16), preferred_element_type=F32)[:, 0:1]
    slot = start + rank
    lpos1 = jnp.sum(jnp.where(hit1, slot, 0.0), axis=0, keepdims=True)
    lpos2 = jnp.sum(jnp.where(hit2, slot, 0.0), axis=0, keepdims=True)
    r_ref[...] = jnp.concatenate(
        [e1.astype(F32), e2.astype(F32), v1 / tot * g_w, v2 / tot * g_w, lpos1, lpos2,
         jnp.zeros((2, tm), F32)], axis=0)
    seg_ref[0] = jnp.broadcast_to(seg, (N_EXPERTS, 128))


def moe_router(x, g, w_group, b_group, w_expert, b_expert):
    m, d = x.shape
    tm = MOE_SUB
    nr = 32
    wr = jnp.zeros((nr, d), F32).at[:N_EXPERT_GROUPS].set(w_group.T)
    wr = wr.at[N_EXPERT_GROUPS:N_EXPERT_GROUPS + N_EXPERTS].set(w_expert.T)
    br = jnp.zeros((nr, 1), F32).at[:N_EXPERT_GROUPS, 0].set(b_group)
    br = br.at[N_EXPERT_GROUPS:N_EXPERT_GROUPS + N_EXPERTS, 0].set(b_expert)
    route, seg3 = pl.pallas_call(
        _router_kernel,
        grid=(m // tm,),
        in_specs=[
            pl.BlockSpec((tm, d), lambda i: (i, 0)),
            pl.BlockSpec((1, d), lambda i: (0, 0)),
            pl.BlockSpec((nr, d), lambda i: (0, 0)),
            pl.BlockSpec((nr, 1), lambda i: (0, 0)),
        ],
        out_specs=[pl.BlockSpec((8, tm), lambda i: (0, i)),
                   pl.BlockSpec((1, N_EXPERTS, 128), lambda i: (i, 0, 0))],
        out_shape=[jax.ShapeDtypeStruct((8, m), F32),
                   jax.ShapeDtypeStruct((m // tm, N_EXPERTS, 128), jnp.int32)],
        compiler_params=_cp(("parallel",), 40),
        name="moe_router",
    )(x, g.reshape(1, d), wr, br)
    return route, seg3[:, :, 0]


def moe_plan(seg, m):
    nsub = m // MOE_SUB
    tm = MOE_TILE
    loc_off = jnp.cumsum(seg, axis=1) - seg
    reg_off = jnp.cumsum(seg, axis=0) - seg
    length = jnp.sum(seg, axis=0)
    padded = (length + tm - 1) // tm * tm
    e_end = jnp.cumsum(padded)
    e_start = e_end - padded
    glob_off = e_start[None, :] + reg_off
    n_slots = (2 * m + nsub * N_EXPERTS * (SEG_ALIGN - 1) + N_EXPERTS * (tm - 1) + tm - 1) // tm * tm
    tile_start = jnp.arange(n_slots // tm, dtype=jnp.int32) * tm
    tile_expert = jnp.minimum(
        jnp.sum((tile_start[:, None] >= e_end[None, :]).astype(jnp.int32), axis=1), N_EXPERTS - 1)
    return dict(seg=seg.reshape(-1), loc_off=loc_off.reshape(-1), glob_off=glob_off.reshape(-1),
                n_slots=n_slots, tile_expert=tile_expert,
                n_used=(e_end[-1:] // tm).astype(jnp.int32),
                pad_start=jnp.concatenate([e_start + length, e_end[-1:]]).astype(jnp.int32),
                pad_len=jnp.concatenate([padded - length, n_slots - e_end[-1:]]).astype(jnp.int32))


def _aligned(v):
    return v if isinstance(v, int) else pl.multiple_of(v, SEG_ALIGN)


def _seg_copy(src_ref, dst_ref, sem, src_row, dst_row, n_rows):
    n_rows = _aligned(n_rows)
    return pltpu.make_async_copy(src_ref.at[pl.ds(_aligned(src_row), n_rows)],
                                 dst_ref.at[pl.ds(_aligned(dst_row), n_rows)], sem)


def _segment_copies(step, seg_ref, loc_ref, glob_ref, make):
    out = []
    for e in range(N_EXPERTS):
        k = step * N_EXPERTS + e
        out.append((seg_ref[k], make(loc_ref[k], glob_ref[k], seg_ref[k])))
    return out


def _start_all(copies):
    for n, cp in copies:
        @pl.when(n > 0)
        def _(cp=cp):
            cp.start()


def _wait_all(copies):
    for n, cp in copies:
        @pl.when(n > 0)
        def _(cp=cp):
            cp.wait()


def _dispatch_kernel(seg_ref, loc_ref, glob_ref, pst_ref, pln_ref, x_ref, g_ref, rt_ref,
                     xs_ref, cbuf_ref, zbuf_ref, sems):
    s = pl.program_id(0)
    n_steps = pl.num_programs(0)
    buf = s % 2
    d = x_ref.shape[1]

    def copies(step, b):
        return _segment_copies(
            step, seg_ref, loc_ref, glob_ref,
            lambda loc, glob, n: _seg_copy(cbuf_ref.at[b], xs_ref, sems.at[b], loc, glob, n))

    @pl.when(s >= 2)
    def _():
        _wait_all(copies(s - 2, buf))

    x = x_ref[...]
    ms = jnp.mean(x * x, axis=-1, keepdims=True)
    t = (x * lax.rsqrt(ms + RMS_EPS) * g_ref[...]).astype(BF16)
    rt = rt_ref[...]
    lp = rt[4:6].astype(jnp.int32)
    slot = lax.broadcasted_iota(jnp.int32, (MOE_CBUF, MOE_SUB), 0)
    hit0 = slot == lp[0:1]
    hit1 = slot == lp[1:2]
    onehot = jnp.where(hit0 | hit1, 1.0, 0.0).astype(BF16)
    cbuf_ref[buf, :, 0:d] = jnp.dot(onehot, t, preferred_element_type=F32).astype(BF16)
    wslot = jnp.sum(jnp.where(hit0, rt[2:3], 0.0) + jnp.where(hit1, rt[3:4], 0.0),
                    axis=1, keepdims=True)
    w_hi = wslot.astype(BF16)
    w_lo = (wslot - w_hi.astype(F32)).astype(BF16)
    half = MOE_WCOLS // 2
    cbuf_ref[buf, :, d:d + half] = jnp.broadcast_to(w_hi, (MOE_CBUF, half))
    cbuf_ref[buf, :, d + half:] = jnp.broadcast_to(w_lo, (MOE_CBUF, half))
    _start_all(copies(s, buf))

    @pl.when(s == 0)
    def _():
        zsem = sems.at[2]
        zbuf_ref[...] = jnp.zeros(zbuf_ref.shape, BF16)
        pads = [(pln_ref[e], _seg_copy(zbuf_ref, xs_ref, zsem, 0, pst_ref[e], pln_ref[e]))
                for e in range(N_EXPERTS)]
        _start_all(pads)
        _wait_all(pads)
        tail_tiles = pln_ref[N_EXPERTS] // MOE_TILE

        def tail_copy(k):
            return _seg_copy(zbuf_ref, xs_ref, zsem, 0, pst_ref[N_EXPERTS] + k * MOE_TILE, MOE_TILE)

        def start(k, carry):
            tail_copy(k).start()
            return carry

        def wait(k, carry):
            tail_copy(k).wait()
            return carry
        lax.fori_loop(0, tail_tiles, start, 0)
        lax.fori_loop(0, tail_tiles, wait, 0)

    @pl.when(s == n_steps - 1)
    def _():
        @pl.when(s >= 1)
        def _():
            _wait_all(copies(s - 1, 1 - buf))
        _wait_all(copies(s, buf))


def moe_dispatch(x, g, route, plan):
    m, d = x.shape
    dw = d + MOE_WCOLS
    return pl.pallas_call(
        _dispatch_kernel,
        grid_spec=pltpu.PrefetchScalarGridSpec(
            num_scalar_prefetch=5,
            grid=(m // MOE_SUB,),
            in_specs=[
                pl.BlockSpec((MOE_SUB, d), lambda i, *_: (i, 0)),
                pl.BlockSpec((1, d), lambda i, *_: (0, 0)),
                pl.BlockSpec((8, MOE_SUB), lambda i, *_: (0, i)),
            ],
            out_specs=pl.BlockSpec(memory_space=pl.ANY),
            scratch_shapes=[pltpu.VMEM((2, MOE_CBUF, dw), BF16), pltpu.VMEM((MOE_TILE, dw), BF16),
                            pltpu.SemaphoreType.DMA((3,))],
        ),
        out_shape=jax.ShapeDtypeStruct((plan["n_slots"], dw), BF16),
        compiler_params=pltpu.CompilerParams(dimension_semantics=("arbitrary",),
                                             vmem_limit_bytes=40 * MIB, has_side_effects=True),
        name="moe_dispatch",
    )(plan["seg"], plan["loc_off"], plan["glob_off"], plan["pad_start"], plan["pad_len"],
      x, g.reshape(1, d), route)


def _ffn_kernel(te_ref, nu_ref, x_ref, wg_ref, wu_ref, wd_ref, o_ref,
                wgb_ref, wub_ref, wdb_ref):
    i = pl.program_id(0)
    d = o_ref.shape[1]
    prev = te_ref[jnp.maximum(i - 1, 0)]

    @pl.when(i < nu_ref[0])
    def _():
        @pl.when((i == 0) | (te_ref[i] != prev))
        def _():
            wgb_ref[...] = wg_ref[...].astype(BF16)
            wub_ref[...] = wu_ref[...].astype(BF16)
            wdb_ref[...] = wd_ref[...].astype(BF16)

        x = x_ref[:, 0:d]
        half = MOE_WCOLS // 2
        w = x_ref[:, d:d + 1].astype(F32) + x_ref[:, d + half:d + half + 1].astype(F32)
        gate = jnp.dot(x, wgb_ref[...], preferred_element_type=F32)
        up = jnp.dot(x, wub_ref[...], preferred_element_type=F32)
        hid = (gate * _sigmoid(gate) * up * w).astype(BF16)
        o_ref[...] = jnp.dot(hid, wdb_ref[...], preferred_element_type=F32).astype(o_ref.dtype)

    @pl.when(i >= nu_ref[0])
    def _():
        o_ref[...] = jnp.zeros(o_ref.shape, o_ref.dtype)


def grouped_ffn(xs, plan, w_gate, w_up, w_down, li):
    p, dw = xs.shape
    d = dw - MOE_WCOLS
    f = D_EXPERT
    tm = MOE_TILE
    row = lambda i, te, nu: (jnp.minimum(i, nu[0] - 1), 0)
    wmap = lambda i, te, nu: (li, te[jnp.minimum(i, nu[0] - 1)], 0, 0)
    return pl.pallas_call(
        _ffn_kernel,
        grid_spec=pltpu.PrefetchScalarGridSpec(
            num_scalar_prefetch=2,
            grid=(p // tm,),
            in_specs=[
                pl.BlockSpec((tm, dw), row),
                pl.BlockSpec((None, None, d, f), wmap),
                pl.BlockSpec((None, None, d, f), wmap),
                pl.BlockSpec((None, None, f, d), wmap),
            ],
            out_specs=pl.BlockSpec((tm, d), lambda i, te, nu: (i, 0)),
            scratch_shapes=[pltpu.VMEM((d, f), BF16), pltpu.VMEM((d, f), BF16),
                            pltpu.VMEM((f, d), BF16)],
        ),
        out_shape=jax.ShapeDtypeStruct((p, d), BF16),
        compiler_params=_cp(("arbitrary",), 48),
        name="grouped_ffn",
    )(plan["tile_expert"], plan["n_used"], xs, w_gate, w_up, w_down)


def _combine_kernel(seg_ref, loc_ref, glob_ref, x_ref, lp_ref, ys_ref, *rest, final_norm):
    if final_norm:
        g_ref, o_ref, ybuf_ref, sems = rest
    else:
        o_ref, ybuf_ref, sems = rest
    s = pl.program_id(0)
    n_steps = pl.num_programs(0)
    buf = s % 2

    def copies(step, b):
        return _segment_copies(
            step, seg_ref, loc_ref, glob_ref,
            lambda loc, glob, n: _seg_copy(ys_ref, ybuf_ref.at[b], sems.at[b], glob, loc, n))

    def fetch(step, b):
        ybuf_ref[b] = jnp.zeros(ybuf_ref.shape[1:], BF16)
        _start_all(copies(step, b))

    @pl.when(s == 0)
    def _():
        fetch(s, buf)

    @pl.when(s + 1 < n_steps)
    def _():
        fetch(s + 1, 1 - buf)

    _wait_all(copies(s, buf))
    lp = lp_ref[...]
    slot = lax.broadcasted_iota(jnp.int32, (MOE_SUB, MOE_CBUF), 1)
    onehot = jnp.where((slot == lp[:, 0:1]) | (slot == lp[:, 1:2]), 1.0, 0.0).astype(BF16)
    out = x_ref[...] + jnp.dot(onehot, ybuf_ref[buf], preferred_element_type=F32)
    if final_norm:
        ms = jnp.mean(out * out, axis=-1, keepdims=True)
        out = out * lax.rsqrt(ms + RMS_EPS) * g_ref[...]
    o_ref[...] = out


def moe_combine(x, ys, route, plan, final_g=None):
    m, d = x.shape
    in_specs = [
        pl.BlockSpec((MOE_SUB, d), lambda i, *_: (i, 0)),
        pl.BlockSpec((MOE_SUB, 2), lambda i, *_: (i, 0)),
        pl.BlockSpec(memory_space=pl.ANY),
    ]
    args = [x, route[4:6].T.astype(jnp.int32), ys]
    if final_g is not None:
        in_specs.append(pl.BlockSpec((1, d), lambda i, *_: (0, 0)))
        args.append(final_g.reshape(1, d))
    return pl.pallas_call(
        functools.partial(_combine_kernel, final_norm=final_g is not None),
        grid_spec=pltpu.PrefetchScalarGridSpec(
            num_scalar_prefetch=3,
            grid=(m // MOE_SUB,),
            in_specs=in_specs,
            out_specs=pl.BlockSpec((MOE_SUB, d), lambda i, *_: (i, 0)),
            scratch_shapes=[pltpu.VMEM((2, MOE_CBUF, d), BF16), pltpu.SemaphoreType.DMA((2,))],
        ),
        out_shape=jax.ShapeDtypeStruct((m, d), F32),
        compiler_params=pltpu.CompilerParams(dimension_semantics=("arbitrary",),
                                             vmem_limit_bytes=40 * MIB),
        name="moe_combine",
    )(plan["seg"], plan["loc_off"], plan["glob_off"], *args)


def hier_moe_layer(x, norm_g, w_group, b_group, w_expert, b_expert, w_gate, w_up, w_down, li,
                   final_g=None):
    m, _ = x.shape
    route, seg = moe_router(x, norm_g, w_group, b_group, w_expert, b_expert)
    plan = moe_plan(seg, m)
    xs = moe_dispatch(x, norm_g, route, plan)
    ys = grouped_ffn(xs, plan, w_gate, w_up, w_down, li)
    return moe_combine(x, ys, route, plan, final_g)


def _rope_tables(seq, dim):
    pos = jnp.arange(seq, dtype=F32)
    inv = ROPE_THETA ** (-jnp.arange(0, dim, 2, dtype=F32) / dim)
    ang = pos[:, None] * inv[None, :]
    ang = jnp.concatenate([ang, ang], axis=-1)
    sign = jnp.concatenate([-jnp.ones((dim // 2,), F32), jnp.ones((dim // 2,), F32)])
    return jnp.cos(ang), jnp.sin(ang) * sign[None, :]


def _lambda_init(depth_idx):
    return 0.8 - 0.6 * math.exp(-0.3 * depth_idx)


def kernel(x, mem, norm_mix, norm_mem, norm_ffn, norm_final, w_out, w_mem_kv, pool_w_in, pool_w_grp, pool_scale, diff_w_in, diff_lambda, diff_subln, ssd_w_in, ssd_conv_w, ssd_conv_b, ssd_dt_bias, ssd_a_log, ssd_d, ssd_norm, moe_w_group, moe_b_group, moe_w_expert, moe_b_expert, moe_w_gate, moe_w_up, moe_w_down):
    b, s, d = x.shape
    m = b * s
    mem_len = mem.shape[1]
    cos, sin_signed = _rope_tables(s, DIFF_HEAD_DIM)
    xt = x.reshape(m, d)
    memt = mem.reshape(b * mem_len, d)
    for i in range(DEPTH):
        kind, slot = i % N_MIXERS, i // N_MIXERS
        mem_kv = norm_matmul(memt, norm_mem[i], w_mem_kv, i, 2 * MEM_WIDTH, b * mem_len, 512, F32)
        mem_kv = mem_kv.reshape(b, mem_len, 2 * MEM_WIDTH)
        if kind == 0:
            n_in = MIX_WIDTH + MEM_WIDTH
            proj = norm_matmul(xt, norm_mix[i], pool_w_in, slot, n_in, IN_TM, IN_TN, BF16)
            proj = proj.reshape(b, s, n_in)
            mix = pool_mixer(proj, pool_w_grp, slot, pool_scale[slot])
        elif kind == 1:
            n_in = 3 * MIX_WIDTH + MEM_WIDTH
            proj = norm_matmul(xt, norm_mix[i], diff_w_in, slot, n_in, IN_TM, IN_TN, BF16,
                               rope=(cos, sin_signed, MIX_WIDTH, MIX_WIDTH, s))
            proj = proj.reshape(b, s, n_in)
            mix = diff_attention(proj, diff_lambda[slot], diff_subln[slot], _lambda_init(i))
        else:
            w = ssd_w_in[slot]
            n_main = MIX_WIDTH + SSD_CONV_DIM
            w = jnp.concatenate(
                [w[:, :n_main], w[:, SSD_MIX_IN:], w[:, n_main:SSD_MIX_IN],
                 jnp.zeros((d, SSD_IN_PADDED - w.shape[1]), F32)], axis=1)
            n_in = SSD_IN_PADDED
            proj = norm_matmul(xt, norm_mix[i], w[None], 0, n_in, IN_TM, IN_TN, F32)
            proj = proj.reshape(b, s, n_in)
            xbc = ssd_conv(proj, ssd_conv_w[slot], ssd_conv_b[slot])
            dt_off = n_main + MEM_WIDTH
            dt_raw = proj[:, :, dt_off:dt_off + SSD_HEADS]
            mix = ssd_scan(xbc, proj, dt_raw, ssd_dt_bias[slot], ssd_a_log[slot], ssd_d[slot],
                           ssd_norm[slot])
        q_block = (MIX_WIDTH if kind == 0 else 3 * MIX_WIDTH if kind == 1
                   else MIX_WIDTH + SSD_CONV_DIM) // MEM_WIDTH
        mem_out = memory_attention(proj, q_block, mem_kv)
        xt = out_projection(xt, mix.reshape(m, MIX_WIDTH), mem_out.reshape(m, MEM_WIDTH), w_out, i)
        xt = hier_moe_layer(xt, norm_ffn[i], moe_w_group[i], moe_b_group[i], moe_w_expert[i],
                            moe_b_expert[i], moe_w_gate, moe_w_up, moe_w_down, i,
                            final_g=norm_final if i == DEPTH - 1 else None)
    return xt.reshape(b, s, d)
```

```python
import functools
import math

import jax
import jax.numpy as jnp
from jax import lax
from jax.experimental import pallas as pl
from jax.experimental.pallas import tpu as pltpu

F32 = jnp.float32
BF16 = jnp.bfloat16

D_MODEL = 2048
DEPTH = 4
N_MIXERS = 3
MIX_WIDTH = 1536
MEM_WIDTH = 512
MEM_HEADS = 4
MEM_HEAD_DIM = 128
POOL_WINDOWS = (2, 4, 8, 16)
POOL_GROUPS = 4
POOL_GROUP_DIM = 384
DIFF_HEAD_DIM = 128
DIFF_HEADS = 6
DIFF_V_DIM = 256
ROPE_THETA = 10000.0
SSD_HEAD_DIM = 64
SSD_HEADS = 24
SSD_GROUPS = 4
SSD_HEADS_PER_GROUP = 6
SSD_STATE = 128
SSD_CONV = 4
SSD_CHUNK = 128
SSD_CONV_DIM = 2560
SSD_MIX_IN = 4120
SSD_IN_PADDED = 5120
N_EXPERT_GROUPS = 4
EXPERTS_PER_GROUP = 4
N_EXPERTS = 16
D_EXPERT = 512
RMS_EPS = 1e-6

IN_TM, IN_TN = 2048, 256
MOE_TILE = 512
MOE_SUB = 512
SEG_ALIGN = 16
MOE_CBUF = 2 * MOE_SUB + 256
MOE_WCOLS = 256
MIB = 1024 * 1024

NT_DIMS = (((1,), (1,)), ((), ()))
TN_DIMS = (((0,), (0,)), ((), ()))


def _cp(sem, vmem_mib):
    return pltpu.CompilerParams(dimension_semantics=sem, vmem_limit_bytes=vmem_mib * MIB)


def _sigmoid(x):
    return 1.0 / (1.0 + jnp.exp(-x))


def _softplus(x):
    return jnp.maximum(x, 0.0) + jnp.log1p(jnp.exp(-jnp.abs(x)))


def _split3(v):
    hi = v.astype(BF16)
    r = v - hi.astype(F32)
    mid = r.astype(BF16)
    lo = (r - mid.astype(F32)).astype(BF16)
    return hi, mid, lo


def _norm_matmul_kernel(x_ref, g_ref, w_ref, *rest, rope_tiles):
    if rope_tiles is None:
        o_ref, h_ref = rest
    else:
        cos_ref, sin_ref, o_ref, h_ref = rest
    j = pl.program_id(1)

    @pl.when(j == 0)
    def _():
        x = x_ref[...]
        ms = jnp.mean(x * x, axis=-1, keepdims=True)
        h_ref[...] = (x * lax.rsqrt(ms + RMS_EPS) * g_ref[...]).astype(BF16)

    acc = jnp.dot(h_ref[...], w_ref[...].astype(BF16), preferred_element_type=F32)
    if rope_tiles is None:
        o_ref[...] = acc.astype(o_ref.dtype)
        return
    n_q, n_k = rope_tiles
    hd = DIFF_HEAD_DIM

    @pl.when(j < n_q + n_k)
    def _():
        scale = jnp.where(j < n_q, hd ** -0.5, 1.0)
        cos = cos_ref[...]
        sin = sin_ref[...]
        for c in range(o_ref.shape[1] // hd):
            x = acc[:, c * hd:(c + 1) * hd]
            r = x * cos + pltpu.roll(x, hd // 2, axis=1) * sin
            o_ref[:, c * hd:(c + 1) * hd] = (r * scale).astype(o_ref.dtype)

    @pl.when(j >= n_q + n_k)
    def _():
        o_ref[...] = acc.astype(o_ref.dtype)


def norm_matmul(x, g, w, li, n_cols, tm, tn, out_dtype, rope=None):
    m, k = x.shape
    in_specs = [
        pl.BlockSpec((tm, k), lambda i, j: (i, 0)),
        pl.BlockSpec((1, k), lambda i, j: (0, 0)),
        pl.BlockSpec((None, k, tn), lambda i, j: (li, 0, j)),
    ]
    args = [x, g.reshape(1, k), w]
    rope_tiles = None
    if rope is not None:
        cos, sin_signed, n_q, n_k, seq = rope
        rope_tiles = (n_q // tn, n_k // tn)
        pos_blocks = seq // tm
        tab = pl.BlockSpec((tm, DIFF_HEAD_DIM), lambda i, j: (i % pos_blocks, 0))
        in_specs += [tab, tab]
        args += [cos, sin_signed]
    return pl.pallas_call(
        functools.partial(_norm_matmul_kernel, rope_tiles=rope_tiles),
        grid=(m // tm, n_cols // tn),
        in_specs=in_specs,
        out_specs=pl.BlockSpec((tm, tn), lambda i, j: (i, j)),
        out_shape=jax.ShapeDtypeStruct((m, n_cols), out_dtype),
        scratch_shapes=[pltpu.VMEM((tm, k), BF16)],
        compiler_params=_cp(("parallel", "arbitrary"), 58),
        name="norm_matmul",
    )(*args)


def _pool_kernel(u_ref, w_ref, sc_ref, o_ref, pad_ref):
    grp = pl.program_id(1)
    s, c = u_ref.shape[1], u_ref.shape[2]
    rows = 256
    pad_ref[0:16, :] = jnp.zeros((16, c), F32)
    pad_ref[16:, :] = u_ref[0].astype(F32)
    wb = w_ref[...].astype(BF16)
    sc = sc_ref[0]

    for gi, win in enumerate(POOL_WINDOWS):
        @pl.when(grp == gi)
        def _(win=win):
            for r in range(s // rows):
                xh = pad_ref[r * rows:r * rows + rows + 16, :]
                acc = xh
                k = 1
                while k < win:
                    acc = acc + pltpu.roll(acc, k, axis=0)
                    k *= 2
                t = r * rows + lax.broadcasted_iota(jnp.int32, (rows, 1), 0)
                cnt = jnp.minimum(t + 1, win).astype(F32)
                mixed = (acc[16:, :] / cnt - xh[16:, :]).astype(BF16)
                o_ref[0, r * rows:(r + 1) * rows, :] = (
                    jnp.dot(mixed, wb, preferred_element_type=F32) * sc).astype(o_ref.dtype)


def pool_mixer(proj3, w_grp, li, scale):
    b, s, _ = proj3.shape
    c = POOL_GROUP_DIM
    return pl.pallas_call(
        _pool_kernel,
        grid=(b, POOL_GROUPS),
        in_specs=[
            pl.BlockSpec((1, s, c), lambda i, g: (i, 0, g)),
            pl.BlockSpec((None, None, c, c), lambda i, g: (li, g, 0, 0)),
            pl.BlockSpec((1, 1, c), lambda i, g: (g, 0, 0)),
        ],
        out_specs=pl.BlockSpec((1, s, c), lambda i, g: (i, 0, g)),
        out_shape=jax.ShapeDtypeStruct((b, s, MIX_WIDTH), BF16),
        scratch_shapes=[pltpu.VMEM((s + 16, c), F32)],
        compiler_params=_cp(("parallel", "parallel"), 40),
        name="pool_mixer",
    )(proj3, w_grp, scale.reshape(POOL_GROUPS, 1, c))


def _diff_attn_kernel(lam_ref, sub_ref, q_ref, k_ref, v_ref, o_ref, *, lambda_init):
    lam = lam_ref[...]
    s1 = jnp.sum(lam[0:1] * lam[1:2], axis=-1, keepdims=True)
    s2 = jnp.sum(lam[2:3] * lam[3:4], axis=-1, keepdims=True)
    lmbda = jnp.exp(s1) - jnp.exp(s2) + lambda_init
    s = q_ref.shape[1]
    tq = 512
    d = DIFF_HEAD_DIM
    for i in range(s // tq):
        kv = (i + 1) * tq
        q = q_ref[0, i * tq:(i + 1) * tq, :]
        row = i * tq + lax.broadcasted_iota(jnp.int32, (tq, kv), 0)
        col = lax.broadcasted_iota(jnp.int32, (tq, kv), 1)
        mask = col <= row
        outs = []
        for c in range(2):
            sc = lax.dot_general(q[:, c * d:(c + 1) * d], k_ref[0, 0:kv, c * d:(c + 1) * d],
                                 NT_DIMS, preferred_element_type=F32)
            sc = jnp.where(mask, sc, -jnp.inf)
            e = jnp.exp(sc - jnp.max(sc, axis=-1, keepdims=True))
            pv = jnp.dot(e.astype(BF16), v_ref[0, 0:kv, :], preferred_element_type=F32)
            outs.append(pv / jnp.sum(e, axis=-1, keepdims=True))
        o = outs[0] - lmbda * outs[1]
        ms = jnp.mean(o * o, axis=-1, keepdims=True)
        o_ref[0, i * tq:(i + 1) * tq, :] = (
            o * lax.rsqrt(ms + RMS_EPS) * sub_ref[...] * (1.0 - lambda_init)).astype(o_ref.dtype)


def diff_attention(proj3, lam, subln, lambda_init):
    b, s, _ = proj3.shape
    vd = DIFF_V_DIM
    nh = DIFF_HEADS
    return pl.pallas_call(
        functools.partial(_diff_attn_kernel, lambda_init=lambda_init),
        grid=(b, nh),
        in_specs=[
            pl.BlockSpec((4, DIFF_HEAD_DIM), lambda i, h: (0, 0)),
            pl.BlockSpec((1, vd), lambda i, h: (0, 0)),
            pl.BlockSpec((1, s, vd), lambda i, h: (i, 0, h)),
            pl.BlockSpec((1, s, vd), lambda i, h: (i, 0, nh + h)),
            pl.BlockSpec((1, s, vd), lambda i, h: (i, 0, 2 * nh + h)),
        ],
        out_specs=pl.BlockSpec((1, s, vd), lambda i, h: (i, 0, h)),
        out_shape=jax.ShapeDtypeStruct((b, s, MIX_WIDTH), BF16),
        compiler_params=_cp(("parallel", "parallel"), 48),
        name="diff_attention",
    )(lam, subln.reshape(1, vd), proj3, proj3, proj3)


def _conv_kernel(u_ref, w_ref, b_ref, o_ref, pad_ref):
    s, c = u_ref.shape[1], u_ref.shape[2]
    rows = 256
    pad_ref[0:8, :] = jnp.zeros((8, c), F32)
    pad_ref[8:, :] = u_ref[0]
    w = w_ref[...]
    bias = b_ref[...]
    for r in range(s // rows):
        xh = pad_ref[r * rows:r * rows + rows + 8, :]
        y = xh * w[3:4]
        for j in range(1, SSD_CONV):
            y = y + pltpu.roll(xh, j, axis=0) * w[SSD_CONV - 1 - j:SSD_CONV - j]
        y = y[8:, :] + bias
        o_ref[0, r * rows:(r + 1) * rows, :] = y * _sigmoid(y)


def ssd_conv(proj3, conv_w, conv_b):
    b, s, _ = proj3.shape
    tc = 512
    off = MIX_WIDTH // tc
    return pl.pallas_call(
        _conv_kernel,
        grid=(b, SSD_CONV_DIM // tc),
        in_specs=[
            pl.BlockSpec((1, s, tc), lambda i, j: (i, 0, off + j)),
            pl.BlockSpec((SSD_CONV, tc), lambda i, j: (0, j)),
            pl.BlockSpec((1, tc), lambda i, j: (0, j)),
        ],
        out_specs=pl.BlockSpec((1, s, tc), lambda i, j: (i, 0, j)),
        out_shape=jax.ShapeDtypeStruct((b, s, SSD_CONV_DIM), F32),
        scratch_shapes=[pltpu.VMEM((s + 8, tc), F32)],
        compiler_params=_cp(("parallel", "parallel"), 48),
        name="ssd_conv",
    )(proj3, conv_w, conv_b.reshape(1, SSD_CONV_DIM))


def _ssd_kernel(x_ref, b_ref, c_ref, z_ref, dtc_ref, dtr_ref, bc_ref, br_ref, ac_ref, ar_ref,
                d_ref, ng_ref, o_ref, state_ref, y_ref):
    @pl.when(pl.program_id(2) == 0)
    def _():
        state_ref[...] = jnp.zeros(state_ref.shape, F32)

    ln = SSD_CHUNK
    p = SSD_HEAD_DIM
    x = x_ref[0]
    bm = b_ref[0]
    cm = c_ref[0].astype(BF16)
    row = lax.broadcasted_iota(jnp.int32, (ln, ln), 0)
    col = lax.broadcasted_iota(jnp.int32, (ln, ln), 1)
    causal = col <= row
    ones_lower = jnp.where(causal, 1.0, 0.0).astype(BF16)
    ones_upper = jnp.where(row <= col, 1.0, 0.0).astype(BF16)

    dtc = _softplus(dtc_ref[0, 0] + bc_ref[0])
    da_c = dtc * (-jnp.exp(ac_ref[0]))
    acs_c = sum(jnp.dot(ones_lower, t, preferred_element_type=F32) for t in _split3(da_c))
    dtr = _softplus(dtr_ref[0, 0] + br_ref[0])
    da_r = dtr * (-jnp.exp(ar_ref[0]))
    acs_r = sum(jnp.dot(t, ones_upper, preferred_element_type=F32) for t in _split3(da_r))

    scores = lax.dot_general(cm, bm.astype(BF16), NT_DIMS, preferred_element_type=F32)
    dsk = d_ref[0]
    for h in range(SSD_HEADS_PER_GROUP):
        ac = acs_c[:, h:h + 1]
        ar = acs_r[h:h + 1, :]
        decay = jnp.exp(jnp.where(causal, ac - ar, -jnp.inf))
        xh = x[:, h * p:(h + 1) * p]
        xc = (xh * dtc[:, h:h + 1]).astype(BF16)
        y = jnp.dot((scores * decay).astype(BF16), xc, preferred_element_type=F32)
        prev = state_ref[h]
        y = y + jnp.dot(cm, prev.astype(BF16), preferred_element_type=F32) * jnp.exp(ac)
        a_last = ac[ln - 1:ln, :]
        bs = (bm * jnp.exp(a_last - ac)).astype(BF16)
        st = lax.dot_general(bs, xc, TN_DIMS, preferred_element_type=F32)
        state_ref[h] = prev * jnp.exp(a_last) + st
        y_ref[:, h * p:(h + 1) * p] = y + dsk[:, h:h + 1] * xh

    z = z_ref[0]
    yz = y_ref[...] * (z * _sigmoid(z))
    ms = jnp.mean(yz * yz, axis=-1, keepdims=True)
    o_ref[0] = (yz * lax.rsqrt(ms + RMS_EPS) * ng_ref[0]).astype(o_ref.dtype)


def ssd_scan(xbc, proj3, dt_raw, dt_bias, a_log, d_skip, norm_g):
    b, s, _ = xbc.shape
    ln, g, hg, n = SSD_CHUNK, SSD_GROUPS, SSD_HEADS_PER_GROUP, SSD_STATE
    gw = hg * SSD_HEAD_DIM
    dt4 = dt_raw.reshape(b, s, g, hg)
    dt_col = dt4.transpose(0, 2, 1, 3)
    dt_row = dt4.transpose(0, 2, 3, 1)
    col = lambda v: v.reshape(g, 1, hg)
    rowv = lambda v: v.reshape(g, hg, 1)
    pc = pl.BlockSpec((1, 1, hg), lambda i, j, c: (j, 0, 0))
    pr = pl.BlockSpec((1, hg, 1), lambda i, j, c: (j, 0, 0))
    return pl.pallas_call(
        _ssd_kernel,
        grid=(b, g, s // ln),
        in_specs=[
            pl.BlockSpec((1, ln, gw), lambda i, j, c: (i, c, j)),
            pl.BlockSpec((1, ln, n), lambda i, j, c: (i, c, MIX_WIDTH // n + j)),
            pl.BlockSpec((1, ln, n), lambda i, j, c: (i, c, MIX_WIDTH // n + g + j)),
            pl.BlockSpec((1, ln, gw), lambda i, j, c: (i, c, j)),
            pl.BlockSpec((1, 1, ln, hg), lambda i, j, c: (i, j, c, 0)),
            pl.BlockSpec((1, 1, hg, ln), lambda i, j, c: (i, j, 0, c)),
            pc, pr, pc, pr, pc,
            pl.BlockSpec((1, 1, gw), lambda i, j, c: (j, 0, 0)),
        ],
        out_specs=pl.BlockSpec((1, ln, gw), lambda i, j, c: (i, c, j)),
        out_shape=jax.ShapeDtypeStruct((b, s, MIX_WIDTH), BF16),
        scratch_shapes=[pltpu.VMEM((hg, n, SSD_HEAD_DIM), F32), pltpu.VMEM((ln, gw), F32)],
        compiler_params=_cp(("parallel", "parallel", "arbitrary"), 32),
        name="ssd_scan",
    )(xbc, xbc, xbc, proj3, dt_col, dt_row, col(dt_bias), rowv(dt_bias), col(a_log), rowv(a_log),
      col(d_skip), norm_g.reshape(g, 1, gw))


def _mem_attn_kernel(q_ref, kv_ref, o_ref):
    d = MEM_HEAD_DIM
    for h in range(MEM_HEADS):
        q = (q_ref[0, :, h * d:(h + 1) * d].astype(F32) * (d ** -0.5)).astype(BF16)
        k = kv_ref[0, :, h * d:(h + 1) * d].astype(BF16)
        v = kv_ref[0, :, MEM_WIDTH + h * d:MEM_WIDTH + (h + 1) * d].astype(BF16)
        sc = lax.dot_general(q, k, NT_DIMS, preferred_element_type=F32)
        e = jnp.exp(sc - jnp.max(sc, axis=-1, keepdims=True))
        pr = (e / jnp.sum(e, axis=-1, keepdims=True)).astype(BF16)
        o_ref[0, :, h * d:(h + 1) * d] = jnp.dot(
            pr, v, preferred_element_type=F32).astype(o_ref.dtype)


def memory_attention(proj3, q_block, mem_kv):
    b, s, _ = proj3.shape
    m = mem_kv.shape[1]
    tq = 512
    return pl.pallas_call(
        _mem_attn_kernel,
        grid=(b, s // tq),
        in_specs=[
            pl.BlockSpec((1, tq, MEM_WIDTH), lambda i, j: (i, j, q_block)),
            pl.BlockSpec((1, m, 2 * MEM_WIDTH), lambda i, j: (i, 0, 0)),
        ],
        out_specs=pl.BlockSpec((1, tq, MEM_WIDTH), lambda i, j: (i, j, 0)),
        out_shape=jax.ShapeDtypeStruct((b, s, MEM_WIDTH), BF16),
        compiler_params=_cp(("parallel", "parallel"), 32),
        name="memory_attention",
    )(proj3, mem_kv)


def _outproj_kernel(x_ref, a_ref, m_ref, wa_ref, wm_ref, o_ref):
    acc = jnp.dot(a_ref[...], wa_ref[...].astype(BF16), preferred_element_type=F32)
    acc = acc + jnp.dot(m_ref[...], wm_ref[...].astype(BF16), preferred_element_type=F32)
    o_ref[...] = x_ref[...] + acc


def out_projection(x, mix, mem_out, w_out, li):
    m, d = x.shape
    tm, tn = 2048, 512
    return pl.pallas_call(
        _outproj_kernel,
        grid=(m // tm, d // tn),
        in_specs=[
            pl.BlockSpec((tm, tn), lambda i, j: (i, j)),
            pl.BlockSpec((tm, MIX_WIDTH), lambda i, j: (i, 0)),
            pl.BlockSpec((tm, MEM_WIDTH), lambda i, j: (i, 0)),
            pl.BlockSpec((None, MIX_WIDTH, tn), lambda i, j: (li, 0, j)),
            pl.BlockSpec((None, MEM_WIDTH, tn), lambda i, j: (li, MIX_WIDTH // MEM_WIDTH, j)),
        ],
        out_specs=pl.BlockSpec((tm, tn), lambda i, j: (i, j)),
        out_shape=jax.ShapeDtypeStruct((m, d), F32),
        compiler_params=_cp(("parallel", "parallel"), 48),
        name="out_projection",
    )(x, mix, mem_out, w_out, w_out)


def _router_kernel(x_ref, g_ref, wr_ref, br_ref, r_ref, seg_ref):
    x = x_ref[...]
    ms = jnp.mean(x * x, axis=-1, keepdims=True)
    t = x * lax.rsqrt(ms + RMS_EPS) * g_ref[...]
    th = t.astype(BF16)
    tl = (t - th.astype(F32)).astype(BF16)
    w = wr_ref[...]
    wh = w.astype(BF16)
    wl = (w - wh.astype(F32)).astype(BF16)
    lg = (lax.dot_general(wh, th, NT_DIMS, preferred_element_type=F32)
          + lax.dot_general(wh, tl, NT_DIMS, preferred_element_type=F32)
          + lax.dot_general(wl, th, NT_DIMS, preferred_element_type=F32)) + br_ref[...]
    ng, epg = N_EXPERT_GROUPS, EXPERTS_PER_GROUP
    gl = [lg[j:j + 1] for j in range(ng)]
    el = [lg[ng + j:ng + j + 1] for j in range(N_EXPERTS)]

    def first_argmax(vals):
        top = functools.reduce(jnp.maximum, vals)
        idx = jnp.full(top.shape, len(vals) - 1, jnp.int32)
        for j in range(len(vals) - 2, -1, -1):
            idx = jnp.where(vals[j] >= top, j, idx)
        return top, idx

    gmax, gsel = first_argmax(gl)
    g_w = 1.0 / functools.reduce(lambda a, b: a + b, [jnp.exp(v - gmax) for v in gl])
    e_in = []
    for j in range(epg):
        v = el[(ng - 1) * epg + j]
        for gi in range(ng - 2, -1, -1):
            v = jnp.where(gsel == gi, el[gi * epg + j], v)
        e_in.append(v)
    emax = functools.reduce(jnp.maximum, e_in)
    pe = [jnp.exp(v - emax) for v in e_in]
    se = functools.reduce(lambda a, b: a + b, pe)
    prob = [v / se for v in pe]
    v1, i1 = first_argmax(prob)
    rest = [jnp.where(i1 == j, -1.0, prob[j]) for j in range(epg)]
    v2, i2 = first_argmax(rest)
    tot = v1 + v2
    e1 = gsel * epg + i1
    e2 = gsel * epg + i2

    tm = x.shape[0]
    eidx = lax.broadcasted_iota(jnp.int32, (N_EXPERTS, tm), 0)
    hit1 = eidx == e1
    hit2 = eidx == e2
    onehot = jnp.where(hit1 | hit2, 1.0, 0.0)
    tok_r = lax.broadcasted_iota(jnp.int32, (tm, tm), 0)
    tok_c = lax.broadcasted_iota(jnp.int32, (tm, tm), 1)
    earlier = jnp.where(tok_r < tok_c, 1.0, 0.0).astype(BF16)
    rank = jnp.dot(onehot.astype(BF16), earlier, preferred_element_type=F32)
    cnt = jnp.sum(onehot, axis=1, keepdims=True).astype(jnp.int32)
    seg = jnp.bitwise_and(cnt + (SEG_ALIGN - 1), -SEG_ALIGN)
    ex_r = lax.broadcasted_iota(jnp.int32, (N_EXPERTS, N_EXPERTS), 0)
    ex_c = lax.broadcasted_iota(jnp.int32, (N_EXPERTS, N_EXPERTS), 1)
    lower = jnp.where(ex_c < ex_r, 1.0, 0.0).astype(BF16)
    seg_lanes = jnp.broadcast_to(seg.astype(F32), (N_EXPERTS, 128))
    start = jnp.dot(lower, seg_lanes.astype(BF16), preferred_element_type=F32)[:, 0:1]
    slot = start + rank
    lpos1 = jnp.sum(jnp.where(hit1, slot, 0.0), axis=0, keepdims=True)
    lpos2 = jnp.sum(jnp.where(hit2, slot, 0.0), axis=0, keepdims=True)
    r_ref[...] = jnp.concatenate(
        [e1.astype(F32), e2.astype(F32), v1 / tot * g_w, v2 / tot * g_w, lpos1, lpos2,
         jnp.zeros((2, tm), F32)], axis=0)
    seg_ref[0] = jnp.broadcast_to(seg, (N_EXPERTS, 128))


def moe_router(x, g, w_group, b_group, w_expert, b_expert):
    m, d = x.shape
    tm = MOE_SUB
    nr = 32
    wr = jnp.zeros((nr, d), F32).at[:N_EXPERT_GROUPS].set(w_group.T)
    wr = wr.at[N_EXPERT_GROUPS:N_EXPERT_GROUPS + N_EXPERTS].set(w_expert.T)
    br = jnp.zeros((nr, 1), F32).at[:N_EXPERT_GROUPS, 0].set(b_group)
    br = br.at[N_EXPERT_GROUPS:N_EXPERT_GROUPS + N_EXPERTS, 0].set(b_expert)
    route, seg3 = pl.pallas_call(
        _router_kernel,
        grid=(m // tm,),
        in_specs=[
            pl.BlockSpec((tm, d), lambda i: (i, 0)),
            pl.BlockSpec((1, d), lambda i: (0, 0)),
            pl.BlockSpec((nr, d), lambda i: (0, 0)),
            pl.BlockSpec((nr, 1), lambda i: (0, 0)),
        ],
        out_specs=[pl.BlockSpec((8, tm), lambda i: (0, i)),
                   pl.BlockSpec((1, N_EXPERTS, 128), lambda i: (i, 0, 0))],
        out_shape=[jax.ShapeDtypeStruct((8, m), F32),
                   jax.ShapeDtypeStruct((m // tm, N_EXPERTS, 128), jnp.int32)],
        compiler_params=_cp(("parallel",), 40),
        name="moe_router",
    )(x, g.reshape(1, d), wr, br)
    return route, seg3[:, :, 0]


def moe_plan(seg, m):
    nsub = m // MOE_SUB
    tm = MOE_TILE
    loc_off = jnp.cumsum(seg, axis=1) - seg
    reg_off = jnp.cumsum(seg, axis=0) - seg
    length = jnp.sum(seg, axis=0)
    padded = (length + tm - 1) // tm * tm
    e_end = jnp.cumsum(padded)
    e_start = e_end - padded
    glob_off = e_start[None, :] + reg_off
    n_slots = (2 * m + nsub * N_EXPERTS * (SEG_ALIGN - 1) + N_EXPERTS * (tm - 1) + tm - 1) // tm * tm
    tile_start = jnp.arange(n_slots // tm, dtype=jnp.int32) * tm
    tile_expert = jnp.minimum(
        jnp.sum((tile_start[:, None] >= e_end[None, :]).astype(jnp.int32), axis=1), N_EXPERTS - 1)
    ar = jnp.arange(N_EXPERTS, dtype=jnp.int32)
    later = (ar[None, :] > ar[:, None]) & (padded[None, :] > 0)
    nxt = jnp.min(jnp.where(later, ar[None, :], N_EXPERTS), axis=1)
    nxt = jnp.where(nxt == N_EXPERTS, -1, nxt).astype(jnp.int32)
    tile_next = jnp.sum(jnp.where(tile_expert[:, None] == ar[None, :], nxt[None, :], 0), axis=1)
    return dict(seg=seg.reshape(-1), loc_off=loc_off.reshape(-1), glob_off=glob_off.reshape(-1),
                n_slots=n_slots, tile_expert=tile_expert, tile_next=tile_next,
                n_used=(e_end[-1:] // tm).astype(jnp.int32),
                pad_start=jnp.concatenate([e_start + length, e_end[-1:]]).astype(jnp.int32),
                pad_len=jnp.concatenate([padded - length, n_slots - e_end[-1:]]).astype(jnp.int32))


def _aligned(v):
    return v if isinstance(v, int) else pl.multiple_of(v, SEG_ALIGN)


def _seg_copy(src_ref, dst_ref, sem, src_row, dst_row, n_rows):
    n_rows = _aligned(n_rows)
    return pltpu.make_async_copy(src_ref.at[pl.ds(_aligned(src_row), n_rows)],
                                 dst_ref.at[pl.ds(_aligned(dst_row), n_rows)], sem)


def _segment_copies(step, seg_ref, loc_ref, glob_ref, make):
    out = []
    for e in range(N_EXPERTS):
        k = step * N_EXPERTS + e
        out.append((seg_ref[k], make(loc_ref[k], glob_ref[k], seg_ref[k])))
    return out


def _start_all(copies):
    for n, cp in copies:
        @pl.when(n > 0)
        def _(cp=cp):
            cp.start()


def _wait_all(copies):
    for n, cp in copies:
        @pl.when(n > 0)
        def _(cp=cp):
            cp.wait()


def _dispatch_kernel(seg_ref, loc_ref, glob_ref, pst_ref, pln_ref, x_ref, g_ref, rt_ref,
                     xs_ref, cbuf_ref, zbuf_ref, sems):
    s = pl.program_id(0)
    n_steps = pl.num_programs(0)
    buf = s % 2
    d = x_ref.shape[1]

    def copies(step, b):
        return _segment_copies(
            step, seg_ref, loc_ref, glob_ref,
            lambda loc, glob, n: _seg_copy(cbuf_ref.at[b], xs_ref, sems.at[b], loc, glob, n))

    @pl.when(s >= 2)
    def _():
        _wait_all(copies(s - 2, buf))

    x = x_ref[...]
    ms = jnp.mean(x * x, axis=-1, keepdims=True)
    t = (x * lax.rsqrt(ms + RMS_EPS) * g_ref[...]).astype(BF16)
    rt = rt_ref[...]
    lp = rt[4:6].astype(jnp.int32)
    slot = lax.broadcasted_iota(jnp.int32, (MOE_CBUF, MOE_SUB), 0)
    hit0 = slot == lp[0:1]
    hit1 = slot == lp[1:2]
    onehot = jnp.where(hit0 | hit1, 1.0, 0.0).astype(BF16)
    cbuf_ref[buf, :, 0:d] = jnp.dot(onehot, t, preferred_element_type=F32).astype(BF16)
    wslot = jnp.sum(jnp.where(hit0, rt[2:3], 0.0) + jnp.where(hit1, rt[3:4], 0.0),
                    axis=1, keepdims=True)
    w_hi = wslot.astype(BF16)
    w_lo = (wslot - w_hi.astype(F32)).astype(BF16)
    half = MOE_WCOLS // 2
    cbuf_ref[buf, :, d:d + half] = jnp.broadcast_to(w_hi, (MOE_CBUF, half))
    cbuf_ref[buf, :, d + half:] = jnp.broadcast_to(w_lo, (MOE_CBUF, half))
    _start_all(copies(s, buf))

    @pl.when(s == 0)
    def _():
        zsem = sems.at[2]
        zbuf_ref[...] = jnp.zeros(zbuf_ref.shape, BF16)
        pads = [(pln_ref[e], _seg_copy(zbuf_ref, xs_ref, zsem, 0, pst_ref[e], pln_ref[e]))
                for e in range(N_EXPERTS)]
        _start_all(pads)
        _wait_all(pads)
        tail_tiles = pln_ref[N_EXPERTS] // MOE_TILE

        def tail_copy(k):
            return _seg_copy(zbuf_ref, xs_ref, zsem, 0, pst_ref[N_EXPERTS] + k * MOE_TILE, MOE_TILE)

        def start(k, carry):
            tail_copy(k).start()
            return carry

        def wait(k, carry):
            tail_copy(k).wait()
            return carry
        lax.fori_loop(0, tail_tiles, start, 0)
        lax.fori_loop(0, tail_tiles, wait, 0)

    @pl.when(s == n_steps - 1)
    def _():
        @pl.when(s >= 1)
        def _():
            _wait_all(copies(s - 1, 1 - buf))
        _wait_all(copies(s, buf))


def moe_dispatch(x, g, route, plan):
    m, d = x.shape
    dw = d + MOE_WCOLS
    return pl.pallas_call(
        _dispatch_kernel,
        grid_spec=pltpu.PrefetchScalarGridSpec(
            num_scalar_prefetch=5,
            grid=(m // MOE_SUB,),
            in_specs=[
                pl.BlockSpec((MOE_SUB, d), lambda i, *_: (i, 0)),
                pl.BlockSpec((1, d), lambda i, *_: (0, 0)),
                pl.BlockSpec((8, MOE_SUB), lambda i, *_: (0, i)),
            ],
            out_specs=pl.BlockSpec(memory_space=pl.ANY),
            scratch_shapes=[pltpu.VMEM((2, MOE_CBUF, dw), BF16), pltpu.VMEM((MOE_TILE, dw), BF16),
                            pltpu.SemaphoreType.DMA((3,))],
        ),
        out_shape=jax.ShapeDtypeStruct((plan["n_slots"], dw), BF16),
        compiler_params=pltpu.CompilerParams(dimension_semantics=("arbitrary",),
                                             vmem_limit_bytes=40 * MIB, has_side_effects=True),
        name="moe_dispatch",
    )(plan["seg"], plan["loc_off"], plan["glob_off"], plan["pad_start"], plan["pad_len"],
      x, g.reshape(1, d), route)


def _ffn_kernel(te_ref, nx_ref, nu_ref, x_ref, wg_hbm, wu_hbm, wd_hbm, o_ref,
                wgs_ref, wus_ref, wds_ref, wgb_ref, wub_ref, wdb_ref, sem, *, li):
    i = pl.program_id(0)
    d = o_ref.shape[1]

    def weight_copies(e):
        return [pltpu.make_async_copy(wg_hbm.at[li, e], wgs_ref, sem),
                pltpu.make_async_copy(wu_hbm.at[li, e], wus_ref, sem),
                pltpu.make_async_copy(wd_hbm.at[li, e], wds_ref, sem)]

    @pl.when(i < nu_ref[0])
    def _():
        e = te_ref[i]

        @pl.when(i == 0)
        def _():
            for cp in weight_copies(e):
                cp.start()

        @pl.when((i == 0) | (e != te_ref[jnp.maximum(i - 1, 0)]))
        def _():
            for cp in weight_copies(e):
                cp.wait()
            wgb_ref[...] = wgs_ref[...].astype(BF16)
            wub_ref[...] = wus_ref[...].astype(BF16)
            wdb_ref[...] = wds_ref[...].astype(BF16)

            @pl.when(nx_ref[i] >= 0)
            def _():
                for cp in weight_copies(nx_ref[i]):
                    cp.start()

        x = x_ref[:, 0:d]
        half = MOE_WCOLS // 2
        w = x_ref[:, d:d + 1].astype(F32) + x_ref[:, d + half:d + half + 1].astype(F32)
        gate = jnp.dot(x, wgb_ref[...], preferred_element_type=F32)
        up = jnp.dot(x, wub_ref[...], preferred_element_type=F32)
        hid = (gate * _sigmoid(gate) * up * w).astype(BF16)
        o_ref[...] = jnp.dot(hid, wdb_ref[...], preferred_element_type=F32).astype(o_ref.dtype)

    @pl.when(i >= nu_ref[0])
    def _():
        o_ref[...] = jnp.zeros(o_ref.shape, o_ref.dtype)


def grouped_ffn(xs, plan, w_gate, w_up, w_down, li):
    p, dw = xs.shape
    d = dw - MOE_WCOLS
    f = D_EXPERT
    tm = MOE_TILE
    hbm = pl.BlockSpec(memory_space=pl.ANY)
    return pl.pallas_call(
        functools.partial(_ffn_kernel, li=li),
        grid_spec=pltpu.PrefetchScalarGridSpec(
            num_scalar_prefetch=3,
            grid=(p // tm,),
            in_specs=[
                pl.BlockSpec((tm, dw), lambda i, te, nx, nu: (jnp.minimum(i, nu[0] - 1), 0)),
                hbm, hbm, hbm,
            ],
            out_specs=pl.BlockSpec((tm, d), lambda i, te, nx, nu: (i, 0)),
            scratch_shapes=[pltpu.VMEM((d, f), F32), pltpu.VMEM((d, f), F32), pltpu.VMEM((f, d), F32),
                            pltpu.VMEM((d, f), BF16), pltpu.VMEM((d, f), BF16),
                            pltpu.VMEM((f, d), BF16), pltpu.SemaphoreType.DMA(())],
        ),
        out_shape=jax.ShapeDtypeStruct((p, d), BF16),
        compiler_params=_cp(("arbitrary",), 48),
        name="grouped_ffn",
    )(plan["tile_expert"], plan["tile_next"], plan["n_used"], xs, w_gate, w_up, w_down)


def _combine_kernel(seg_ref, loc_ref, glob_ref, x_ref, lp_ref, ys_ref, *rest, final_norm):
    if final_norm:
        g_ref, o_ref, ybuf_ref, sems = rest
    else:
        o_ref, ybuf_ref, sems = rest
    s = pl.program_id(0)
    n_steps = pl.num_programs(0)
    buf = s % 2

    def copies(step, b):
        return _segment_copies(
            step, seg_ref, loc_ref, glob_ref,
            lambda loc, glob, n: _seg_copy(ys_ref, ybuf_ref.at[b], sems.at[b], glob, loc, n))

    def fetch(step, b):
        ybuf_ref[b] = jnp.zeros(ybuf_ref.shape[1:], BF16)
        _start_all(copies(step, b))

    @pl.when(s == 0)
    def _():
        fetch(s, buf)

    @pl.when(s + 1 < n_steps)
    def _():
        fetch(s + 1, 1 - buf)

    _wait_all(copies(s, buf))
    lp = lp_ref[...]
    slot = lax.broadcasted_iota(jnp.int32, (MOE_SUB, MOE_CBUF), 1)
    onehot = jnp.where((slot == lp[:, 0:1]) | (slot == lp[:, 1:2]), 1.0, 0.0).astype(BF16)
    out = x_ref[...] + jnp.dot(onehot, ybuf_ref[buf], preferred_element_type=F32)
    if final_norm:
        ms = jnp.mean(out * out, axis=-1, keepdims=True)
        out = out * lax.rsqrt(ms + RMS_EPS) * g_ref[...]
    o_ref[...] = out


def moe_combine(x, ys, route, plan, final_g=None):
    m, d = x.shape
    in_specs = [
        pl.BlockSpec((MOE_SUB, d), lambda i, *_: (i, 0)),
        pl.BlockSpec((MOE_SUB, 2), lambda i, *_: (i, 0)),
        pl.BlockSpec(memory_space=pl.ANY),
    ]
    args = [x, route[4:6].T.astype(jnp.int32), ys]
    if final_g is not None:
        in_specs.append(pl.BlockSpec((1, d), lambda i, *_: (0, 0)))
        args.append(final_g.reshape(1, d))
    return pl.pallas_call(
        functools.partial(_combine_kernel, final_norm=final_g is not None),
        grid_spec=pltpu.PrefetchScalarGridSpec(
            num_scalar_prefetch=3,
            grid=(m // MOE_SUB,),
            in_specs=in_specs,
            out_specs=pl.BlockSpec((MOE_SUB, d), lambda i, *_: (i, 0)),
            scratch_shapes=[pltpu.VMEM((2, MOE_CBUF, d), BF16), pltpu.SemaphoreType.DMA((2,))],
        ),
        out_shape=jax.ShapeDtypeStruct((m, d), F32),
        compiler_params=pltpu.CompilerParams(dimension_semantics=("arbitrary",),
                                             vmem_limit_bytes=40 * MIB),
        name="moe_combine",
    )(plan["seg"], plan["loc_off"], plan["glob_off"], *args)


def hier_moe_layer(x, norm_g, w_group, b_group, w_expert, b_expert, w_gate, w_up, w_down, li,
                   final_g=None):
    m, _ = x.shape
    route, seg = moe_router(x, norm_g, w_group, b_group, w_expert, b_expert)
    plan = moe_plan(seg, m)
    xs = moe_dispatch(x, norm_g, route, plan)
    ys = grouped_ffn(xs, plan, w_gate, w_up, w_down, li)
    return moe_combine(x, ys, route, plan, final_g)


def _rope_tables(seq, dim):
    pos = jnp.arange(seq, dtype=F32)
    inv = ROPE_THETA ** (-jnp.arange(0, dim, 2, dtype=F32) / dim)
    ang = pos[:, None] * inv[None, :]
    ang = jnp.concatenate([ang, ang], axis=-1)
    sign = jnp.concatenate([-jnp.ones((dim // 2,), F32), jnp.ones((dim // 2,), F32)])
    return jnp.cos(ang), jnp.sin(ang) * sign[None, :]


def _lambda_init(depth_idx):
    return 0.8 - 0.6 * math.exp(-0.3 * depth_idx)


def kernel(x, mem, norm_mix, norm_mem, norm_ffn, norm_final, w_out, w_mem_kv, pool_w_in, pool_w_grp, pool_scale, diff_w_in, diff_lambda, diff_subln, ssd_w_in, ssd_conv_w, ssd_conv_b, ssd_dt_bias, ssd_a_log, ssd_d, ssd_norm, moe_w_group, moe_b_group, moe_w_expert, moe_b_expert, moe_w_gate, moe_w_up, moe_w_down):
    b, s, d = x.shape
    m = b * s
    mem_len = mem.shape[1]
    cos, sin_signed = _rope_tables(s, DIFF_HEAD_DIM)
    xt = x.reshape(m, d)
    memt = mem.reshape(b * mem_len, d)
    for i in range(DEPTH):
        kind, slot = i % N_MIXERS, i // N_MIXERS
        mem_kv = norm_matmul(memt, norm_mem[i], w_mem_kv, i, 2 * MEM_WIDTH, b * mem_len, 512, F32)
        mem_kv = mem_kv.reshape(b, mem_len, 2 * MEM_WIDTH)
        if kind == 0:
            n_in = MIX_WIDTH + MEM_WIDTH
            proj = norm_matmul(xt, norm_mix[i], pool_w_in, slot, n_in, IN_TM, IN_TN, BF16)
            proj = proj.reshape(b, s, n_in)
            mix = pool_mixer(proj, pool_w_grp, slot, pool_scale[slot])
        elif kind == 1:
            n_in = 3 * MIX_WIDTH + MEM_WIDTH
            proj = norm_matmul(xt, norm_mix[i], diff_w_in, slot, n_in, IN_TM, IN_TN, BF16,
                               rope=(cos, sin_signed, MIX_WIDTH, MIX_WIDTH, s))
            proj = proj.reshape(b, s, n_in)
            mix = diff_attention(proj, diff_lambda[slot], diff_subln[slot], _lambda_init(i))
        else:
            w = ssd_w_in[slot]
            n_main = MIX_WIDTH + SSD_CONV_DIM
            w = jnp.concatenate(
                [w[:, :n_main], w[:, SSD_MIX_IN:], w[:, n_main:SSD_MIX_IN],
                 jnp.zeros((d, SSD_IN_PADDED - w.shape[1]), F32)], axis=1)
            n_in = SSD_IN_PADDED
            proj = norm_matmul(xt, norm_mix[i], w[None], 0, n_in, IN_TM, IN_TN, F32)
            proj = proj.reshape(b, s, n_in)
            xbc = ssd_conv(proj, ssd_conv_w[slot], ssd_conv_b[slot])
            dt_off = n_main + MEM_WIDTH
            dt_raw = proj[:, :, dt_off:dt_off + SSD_HEADS]
            mix = ssd_scan(xbc, proj, dt_raw, ssd_dt_bias[slot], ssd_a_log[slot], ssd_d[slot],
                           ssd_norm[slot])
        q_block = (MIX_WIDTH if kind == 0 else 3 * MIX_WIDTH if kind == 1
                   else MIX_WIDTH + SSD_CONV_DIM) // MEM_WIDTH
        mem_out = memory_attention(proj, q_block, mem_kv)
        xt = out_projection(xt, mix.reshape(m, MIX_WIDTH), mem_out.reshape(m, MEM_WIDTH), w_out, i)
        xt = hier_moe_layer(xt, norm_ffn[i], moe_w_group[i], moe_b_group[i], moe_w_expert[i],
                            moe_b_expert[i], moe_w_gate, moe_w_up, moe_w_down, i,
                            final_g=norm_final if i == DEPTH - 1 else None)
    return xt.reshape(b, s, d)
```

```python
import functools
import math

import jax
import jax.numpy as jnp
from jax import lax
from jax.experimental import pallas as pl
from jax.experimental.pallas import tpu as pltpu

F32 = jnp.float32
BF16 = jnp.bfloat16

D_MODEL = 2048
DEPTH = 4
N_MIXERS = 3
MIX_WIDTH = 1536
MEM_WIDTH = 512
MEM_HEADS = 4
MEM_HEAD_DIM = 128
POOL_WINDOWS = (2, 4, 8, 16)
POOL_GROUPS = 4
POOL_GROUP_DIM = 384
DIFF_HEAD_DIM = 128
DIFF_HEADS = 6
DIFF_V_DIM = 256
ROPE_THETA = 10000.0
SSD_HEAD_DIM = 64
SSD_HEADS = 24
SSD_GROUPS = 4
SSD_HEADS_PER_GROUP = 6
SSD_STATE = 128
SSD_CONV = 4
SSD_CHUNK = 128
SSD_CONV_DIM = 2560
SSD_MIX_IN = 4120
SSD_TAIL = 768
N_EXPERT_GROUPS = 4
EXPERTS_PER_GROUP = 4
N_EXPERTS = 16
D_EXPERT = 512
RMS_EPS = 1e-6

IN_TM, IN_TN = 2048, 256
MOE_TILE = 512
MOE_SUB = 512
SEG_ALIGN = 16
MOE_CBUF = 2 * MOE_SUB + 256
MOE_WCOLS = 256
MIB = 1024 * 1024

NT_DIMS = (((1,), (1,)), ((), ()))
TN_DIMS = (((0,), (0,)), ((), ()))


def _cp(sem, vmem_mib):
    return pltpu.CompilerParams(dimension_semantics=sem, vmem_limit_bytes=vmem_mib * MIB)


def _sigmoid(x):
    return 1.0 / (1.0 + jnp.exp(-x))


def _softplus(x):
    return jnp.maximum(x, 0.0) + jnp.log1p(jnp.exp(-jnp.abs(x)))


def _split3(v):
    hi = v.astype(BF16)
    r = v - hi.astype(F32)
    mid = r.astype(BF16)
    lo = (r - mid.astype(F32)).astype(BF16)
    return hi, mid, lo


def _norm_matmul_kernel(x_ref, g_ref, w_ref, *rest, rope_tiles, main_tiles):
    if rope_tiles is not None:
        cos_ref, sin_ref, o_ref, h_ref = rest
    elif main_tiles is not None:
        wt_ref, o_ref, h_ref = rest
    else:
        o_ref, h_ref = rest
    j = pl.program_id(1)

    @pl.when(j == 0)
    def _():
        x = x_ref[...]
        ms = jnp.mean(x * x, axis=-1, keepdims=True)
        h_ref[...] = (x * lax.rsqrt(ms + RMS_EPS) * g_ref[...]).astype(BF16)

    def product(weights_ref):
        return jnp.dot(h_ref[...], weights_ref[...].astype(BF16), preferred_element_type=F32)

    if main_tiles is not None:
        @pl.when(j < main_tiles)
        def _():
            o_ref[...] = product(w_ref).astype(o_ref.dtype)

        @pl.when(j >= main_tiles)
        def _():
            o_ref[...] = product(wt_ref).astype(o_ref.dtype)
        return
    acc = product(w_ref)
    if rope_tiles is None:
        o_ref[...] = acc.astype(o_ref.dtype)
        return
    n_q, n_k = rope_tiles
    hd = DIFF_HEAD_DIM

    @pl.when(j < n_q + n_k)
    def _():
        scale = jnp.where(j < n_q, hd ** -0.5, 1.0)
        cos = cos_ref[...]
        sin = sin_ref[...]
        for c in range(o_ref.shape[1] // hd):
            x = acc[:, c * hd:(c + 1) * hd]
            r = x * cos + pltpu.roll(x, hd // 2, axis=1) * sin
            o_ref[:, c * hd:(c + 1) * hd] = (r * scale).astype(o_ref.dtype)

    @pl.when(j >= n_q + n_k)
    def _():
        o_ref[...] = acc.astype(o_ref.dtype)


def norm_matmul(x, g, w, li, n_cols, tm, tn, out_dtype, rope=None, tail=None):
    m, k = x.shape
    main_tiles = None if tail is None else n_cols // tn
    w_map = ((lambda i, j: (li, 0, j)) if tail is None
             else (lambda i, j: (li, 0, jnp.minimum(j, main_tiles - 1))))
    in_specs = [
        pl.BlockSpec((tm, k), lambda i, j: (i, 0)),
        pl.BlockSpec((1, k), lambda i, j: (0, 0)),
        pl.BlockSpec((None, k, tn), w_map),
    ]
    args = [x, g.reshape(1, k), w]
    if tail is not None:
        in_specs.append(pl.BlockSpec((k, tn), lambda i, j: (0, jnp.maximum(j - main_tiles, 0))))
        args.append(tail)
        n_cols = n_cols + tail.shape[1]
    rope_tiles = None
    if rope is not None:
        cos, sin_signed, n_q, n_k, seq = rope
        rope_tiles = (n_q // tn, n_k // tn)
        pos_blocks = seq // tm
        tab = pl.BlockSpec((tm, DIFF_HEAD_DIM), lambda i, j: (i % pos_blocks, 0))
        in_specs += [tab, tab]
        args += [cos, sin_signed]
    return pl.pallas_call(
        functools.partial(_norm_matmul_kernel, rope_tiles=rope_tiles, main_tiles=main_tiles),
        grid=(m // tm, n_cols // tn),
        in_specs=in_specs,
        out_specs=pl.BlockSpec((tm, tn), lambda i, j: (i, j)),
        out_shape=jax.ShapeDtypeStruct((m, n_cols), out_dtype),
        scratch_shapes=[pltpu.VMEM((tm, k), BF16)],
        compiler_params=_cp(("parallel", "arbitrary"), 58),
        name="norm_matmul",
    )(*args)


def _pool_kernel(u_ref, w_ref, sc_ref, o_ref, pad_ref):
    grp = pl.program_id(1)
    s, c = u_ref.shape[1], u_ref.shape[2]
    rows = 256
    pad_ref[0:16, :] = jnp.zeros((16, c), F32)
    pad_ref[16:, :] = u_ref[0].astype(F32)
    wb = w_ref[...].astype(BF16)
    sc = sc_ref[0]

    for gi, win in enumerate(POOL_WINDOWS):
        @pl.when(grp == gi)
        def _(win=win):
            for r in range(s // rows):
                xh = pad_ref[r * rows:r * rows + rows + 16, :]
                acc = xh
                k = 1
                while k < win:
                    acc = acc + pltpu.roll(acc, k, axis=0)
                    k *= 2
                t = r * rows + lax.broadcasted_iota(jnp.int32, (rows, 1), 0)
                cnt = jnp.minimum(t + 1, win).astype(F32)
                mixed = (acc[16:, :] / cnt - xh[16:, :]).astype(BF16)
                o_ref[0, r * rows:(r + 1) * rows, :] = (
                    jnp.dot(mixed, wb, preferred_element_type=F32) * sc).astype(o_ref.dtype)


def pool_mixer(proj3, w_grp, li, scale):
    b, s, _ = proj3.shape
    c = POOL_GROUP_DIM
    return pl.pallas_call(
        _pool_kernel,
        grid=(b, POOL_GROUPS),
        in_specs=[
            pl.BlockSpec((1, s, c), lambda i, g: (i, 0, g)),
            pl.BlockSpec((None, None, c, c), lambda i, g: (li, g, 0, 0)),
            pl.BlockSpec((1, 1, c), lambda i, g: (g, 0, 0)),
        ],
        out_specs=pl.BlockSpec((1, s, c), lambda i, g: (i, 0, g)),
        out_shape=jax.ShapeDtypeStruct((b, s, MIX_WIDTH), BF16),
        scratch_shapes=[pltpu.VMEM((s + 16, c), F32)],
        compiler_params=_cp(("parallel", "parallel"), 40),
        name="pool_mixer",
    )(proj3, w_grp, scale.reshape(POOL_GROUPS, 1, c))


def _diff_attn_kernel(lam_ref, sub_ref, q_ref, k_ref, v_ref, o_ref, *, lambda_init):
    lam = lam_ref[...]
    s1 = jnp.sum(lam[0:1] * lam[1:2], axis=-1, keepdims=True)
    s2 = jnp.sum(lam[2:3] * lam[3:4], axis=-1, keepdims=True)
    lmbda = jnp.exp(s1) - jnp.exp(s2) + lambda_init
    s = q_ref.shape[1]
    tq = 512
    d = DIFF_HEAD_DIM
    for i in range(s // tq):
        kv = (i + 1) * tq
        q = q_ref[0, i * tq:(i + 1) * tq, :]
        row = i * tq + lax.broadcasted_iota(jnp.int32, (tq, kv), 0)
        col = lax.broadcasted_iota(jnp.int32, (tq, kv), 1)
        mask = col <= row
        outs = []
        for c in range(2):
            sc = lax.dot_general(q[:, c * d:(c + 1) * d], k_ref[0, 0:kv, c * d:(c + 1) * d],
                                 NT_DIMS, preferred_element_type=F32)
            sc = jnp.where(mask, sc, -jnp.inf)
            e = jnp.exp(sc - jnp.max(sc, axis=-1, keepdims=True))
            pv = jnp.dot(e.astype(BF16), v_ref[0, 0:kv, :], preferred_element_type=F32)
            outs.append(pv / jnp.sum(e, axis=-1, keepdims=True))
        o = outs[0] - lmbda * outs[1]
        ms = jnp.mean(o * o, axis=-1, keepdims=True)
        o_ref[0, i * tq:(i + 1) * tq, :] = (
            o * lax.rsqrt(ms + RMS_EPS) * sub_ref[...] * (1.0 - lambda_init)).astype(o_ref.dtype)


def diff_attention(proj3, lam, subln, lambda_init):
    b, s, _ = proj3.shape
    vd = DIFF_V_DIM
    nh = DIFF_HEADS
    return pl.pallas_call(
        functools.partial(_diff_attn_kernel, lambda_init=lambda_init),
        grid=(b, nh),
        in_specs=[
            pl.BlockSpec((4, DIFF_HEAD_DIM), lambda i, h: (0, 0)),
            pl.BlockSpec((1, vd), lambda i, h: (0, 0)),
            pl.BlockSpec((1, s, vd), lambda i, h: (i, 0, h)),
            pl.BlockSpec((1, s, vd), lambda i, h: (i, 0, nh + h)),
            pl.BlockSpec((1, s, vd), lambda i, h: (i, 0, 2 * nh + h)),
        ],
        out_specs=pl.BlockSpec((1, s, vd), lambda i, h: (i, 0, h)),
        out_shape=jax.ShapeDtypeStruct((b, s, MIX_WIDTH), BF16),
        compiler_params=_cp(("parallel", "parallel"), 48),
        name="diff_attention",
    )(lam, subln.reshape(1, vd), proj3, proj3, proj3)


def _conv_kernel(u_ref, w_ref, b_ref, o_ref, pad_ref):
    s, c = u_ref.shape[1], u_ref.shape[2]
    rows = 256
    pad_ref[0:8, :] = jnp.zeros((8, c), F32)
    pad_ref[8:, :] = u_ref[0]
    w = w_ref[...]
    bias = b_ref[...]
    for r in range(s // rows):
        xh = pad_ref[r * rows:r * rows + rows + 8, :]
        y = xh * w[3:4]
        for j in range(1, SSD_CONV):
            y = y + pltpu.roll(xh, j, axis=0) * w[SSD_CONV - 1 - j:SSD_CONV - j]
        y = y[8:, :] + bias
        o_ref[0, r * rows:(r + 1) * rows, :] = y * _sigmoid(y)


def ssd_conv(proj3, conv_w, conv_b):
    b, s, _ = proj3.shape
    tc = 512
    off = MIX_WIDTH // tc
    return pl.pallas_call(
        _conv_kernel,
        grid=(b, SSD_CONV_DIM // tc),
        in_specs=[
            pl.BlockSpec((1, s, tc), lambda i, j: (i, 0, off + j)),
            pl.BlockSpec((SSD_CONV, tc), lambda i, j: (0, j)),
            pl.BlockSpec((1, tc), lambda i, j: (0, j)),
        ],
        out_specs=pl.BlockSpec((1, s, tc), lambda i, j: (i, 0, j)),
        out_shape=jax.ShapeDtypeStruct((b, s, SSD_CONV_DIM), F32),
        scratch_shapes=[pltpu.VMEM((s + 8, tc), F32)],
        compiler_params=_cp(("parallel", "parallel"), 48),
        name="ssd_conv",
    )(proj3, conv_w, conv_b.reshape(1, SSD_CONV_DIM))


def _dot3(a_f32, b_bf16):
    return sum(jnp.dot(t, b_bf16, preferred_element_type=F32) for t in _split3(a_f32))


def _ssd_kernel(x_ref, b_ref, c_ref, z_ref, dt_ref, bias_ref, alog_ref, dx_ref, ng_ref, o_ref,
                state_ref, y_ref):
    @pl.when(pl.program_id(2) == 0)
    def _():
        state_ref[...] = jnp.zeros(state_ref.shape, F32)

    ln = SSD_CHUNK
    hg = SSD_HEADS_PER_GROUP
    lanes = 2 * SSD_HEAD_DIM
    bm = b_ref[0]
    bm_t = bm.T.astype(BF16)
    cm = c_ref[0].astype(BF16)
    row = lax.broadcasted_iota(jnp.int32, (ln, ln), 0)
    col = lax.broadcasted_iota(jnp.int32, (ln, ln), 1)
    causal = col <= row
    ones_lower = jnp.where(causal, 1.0, 0.0).astype(BF16)

    dt = _softplus(dt_ref[0] + bias_ref[0])
    acs = sum(jnp.dot(ones_lower, t, preferred_element_type=F32)
              for t in _split3(dt * (-jnp.exp(alog_ref[0]))))
    acs_rows = acs.T
    sel_r = lax.broadcasted_iota(jnp.int32, (lanes, hg * lanes), 0)
    sel_c = lax.broadcasted_iota(jnp.int32, (lanes, hg * lanes), 1)
    spread = jnp.where(jnp.right_shift(sel_c, lanes.bit_length() - 1) == sel_r,
                       1.0, 0.0).astype(BF16)
    dt_all = _dot3(dt, spread)
    acs_all = _dot3(acs, spread)

    scores = lax.dot_general(cm, bm.astype(BF16), NT_DIMS, preferred_element_type=F32)
    low_half = lax.broadcasted_iota(jnp.int32, (ln, lanes), 1) < SSD_HEAD_DIM
    for k in range(hg // 2):
        heads = (2 * k, 2 * k + 1)
        tile = lambda a, h: a[:, h * lanes:(h + 1) * lanes]
        x = x_ref[0, :, k * lanes:(k + 1) * lanes]
        dt_p = jnp.where(low_half, tile(dt_all, heads[0]), tile(dt_all, heads[1]))
        acs_p = jnp.where(low_half, tile(acs_all, heads[0]), tile(acs_all, heads[1]))
        xc = x * dt_p
        prev = state_ref[k]
        y = jnp.dot(cm, prev.astype(BF16), preferred_element_type=F32) * jnp.exp(acs_p)
        for h, own in zip(heads, (low_half, ~low_half)):
            decay = jnp.exp(jnp.where(causal, tile(acs_all, h) - acs_rows[h:h + 1, :], -jnp.inf))
            y = y + jnp.dot((scores * decay).astype(BF16), jnp.where(own, xc, 0.0).astype(BF16),
                            preferred_element_type=F32)
        a_last = acs_p[ln - 1:ln, :]
        st = jnp.dot(bm_t, (xc * jnp.exp(a_last - acs_p)).astype(BF16),
                     preferred_element_type=F32)
        state_ref[k] = prev * jnp.exp(a_last) + st
        y_ref[:, k * lanes:(k + 1) * lanes] = y

    z = z_ref[0]
    yz = (y_ref[...] + dx_ref[0] * x_ref[0]) * (z * _sigmoid(z))
    ms = jnp.mean(yz * yz, axis=-1, keepdims=True)
    o_ref[0] = (yz * lax.rsqrt(ms + RMS_EPS) * ng_ref[0]).astype(o_ref.dtype)


def ssd_scan(xbc, proj3, dt_raw, dt_bias, a_log, d_skip, norm_g):
    b, s, _ = xbc.shape
    ln, g, hg, n = SSD_CHUNK, SSD_GROUPS, SSD_HEADS_PER_GROUP, SSD_STATE
    gw = hg * SSD_HEAD_DIM
    lanes = 2 * SSD_HEAD_DIM
    assert ln == lanes and n == lanes

    def head_lanes(v):
        v = v.reshape(v.shape[:-1] + (g, hg))
        v = jnp.pad(v, [(0, 0)] * (v.ndim - 1) + [(0, lanes - hg)])
        return v.reshape(v.shape[:-2] + (g * lanes,))

    per_group = pl.BlockSpec((1, 1, lanes), lambda i, j, c: (j, 0, 0))
    d_chan = jnp.repeat(d_skip, SSD_HEAD_DIM)
    return pl.pallas_call(
        _ssd_kernel,
        grid=(b, g, s // ln),
        in_specs=[
            pl.BlockSpec((1, ln, gw), lambda i, j, c: (i, c, j)),
            pl.BlockSpec((1, ln, n), lambda i, j, c: (i, c, MIX_WIDTH // n + j)),
            pl.BlockSpec((1, ln, n), lambda i, j, c: (i, c, MIX_WIDTH // n + g + j)),
            pl.BlockSpec((1, ln, gw), lambda i, j, c: (i, c, j)),
            pl.BlockSpec((1, ln, lanes), lambda i, j, c: (i, c, j)),
            per_group, per_group,
            pl.BlockSpec((1, 1, gw), lambda i, j, c: (j, 0, 0)),
            pl.BlockSpec((1, 1, gw), lambda i, j, c: (j, 0, 0)),
        ],
        out_specs=pl.BlockSpec((1, ln, gw), lambda i, j, c: (i, c, j)),
        out_shape=jax.ShapeDtypeStruct((b, s, MIX_WIDTH), BF16),
        scratch_shapes=[pltpu.VMEM((hg // 2, n, lanes), F32), pltpu.VMEM((ln, gw), F32)],
        compiler_params=_cp(("parallel", "parallel", "arbitrary"), 32),
        name="ssd_scan",
    )(xbc, xbc, xbc, proj3, head_lanes(dt_raw), head_lanes(dt_bias).reshape(g, 1, lanes),
      head_lanes(a_log).reshape(g, 1, lanes), d_chan.reshape(g, 1, gw), norm_g.reshape(g, 1, gw))


def _mem_attn_kernel(q_ref, kv_ref, o_ref):
    d = MEM_HEAD_DIM
    for h in range(MEM_HEADS):
        q = (q_ref[0, :, h * d:(h + 1) * d].astype(F32) * (d ** -0.5)).astype(BF16)
        k = kv_ref[0, :, h * d:(h + 1) * d].astype(BF16)
        v = kv_ref[0, :, MEM_WIDTH + h * d:MEM_WIDTH + (h + 1) * d].astype(BF16)
        sc = lax.dot_general(q, k, NT_DIMS, preferred_element_type=F32)
        e = jnp.exp(sc - jnp.max(sc, axis=-1, keepdims=True))
        pr = (e / jnp.sum(e, axis=-1, keepdims=True)).astype(BF16)
        o_ref[0, :, h * d:(h + 1) * d] = jnp.dot(
            pr, v, preferred_element_type=F32).astype(o_ref.dtype)


def memory_attention(proj3, q_block, mem_kv):
    b, s, _ = proj3.shape
    m = mem_kv.shape[1]
    tq = 512
    return pl.pallas_call(
        _mem_attn_kernel,
        grid=(b, s // tq),
        in_specs=[
            pl.BlockSpec((1, tq, MEM_WIDTH), lambda i, j: (i, j, q_block)),
            pl.BlockSpec((1, m, 2 * MEM_WIDTH), lambda i, j: (i, 0, 0)),
        ],
        out_specs=pl.BlockSpec((1, tq, MEM_WIDTH), lambda i, j: (i, j, 0)),
        out_shape=jax.ShapeDtypeStruct((b, s, MEM_WIDTH), BF16),
        compiler_params=_cp(("parallel", "parallel"), 32),
        name="memory_attention",
    )(proj3, mem_kv)


def _outproj_kernel(x_ref, a_ref, m_ref, wa_ref, wm_ref, o_ref):
    acc = jnp.dot(a_ref[...], wa_ref[...].astype(BF16), preferred_element_type=F32)
    acc = acc + jnp.dot(m_ref[...], wm_ref[...].astype(BF16), preferred_element_type=F32)
    o_ref[...] = x_ref[...] + acc


def out_projection(x, mix, mem_out, w_out, li):
    m, d = x.shape
    tm, tn = 2048, 512
    return pl.pallas_call(
        _outproj_kernel,
        grid=(m // tm, d // tn),
        in_specs=[
            pl.BlockSpec((tm, tn), lambda i, j: (i, j)),
            pl.BlockSpec((tm, MIX_WIDTH), lambda i, j: (i, 0)),
            pl.BlockSpec((tm, MEM_WIDTH), lambda i, j: (i, 0)),
            pl.BlockSpec((None, MIX_WIDTH, tn), lambda i, j: (li, 0, j)),
            pl.BlockSpec((None, MEM_WIDTH, tn), lambda i, j: (li, MIX_WIDTH // MEM_WIDTH, j)),
        ],
        out_specs=pl.BlockSpec((tm, tn), lambda i, j: (i, j)),
        out_shape=jax.ShapeDtypeStruct((m, d), F32),
        compiler_params=_cp(("parallel", "parallel"), 48),
        name="out_projection",
    )(x, mix, mem_out, w_out, w_out)


def _router_kernel(x_ref, g_ref, wr_ref, br_ref, r_ref, seg_ref):
    x = x_ref[...]
    ms = jnp.mean(x * x, axis=-1, keepdims=True)
    t = x * lax.rsqrt(ms + RMS_EPS) * g_ref[...]
    th = t.astype(BF16)
    tl = (t - th.astype(F32)).astype(BF16)
    w = wr_ref[...]
    wh = w.astype(BF16)
    wl = (w - wh.astype(F32)).astype(BF16)
    lg = (lax.dot_general(wh, th, NT_DIMS, preferred_element_type=F32)
          + lax.dot_general(wh, tl, NT_DIMS, preferred_element_type=F32)
          + lax.dot_general(wl, th, NT_DIMS, preferred_element_type=F32)) + br_ref[...]
    ng, epg = N_EXPERT_GROUPS, EXPERTS_PER_GROUP
    gl = [lg[j:j + 1] for j in range(ng)]
    el = [lg[ng + j:ng + j + 1] for j in range(N_EXPERTS)]

    def first_argmax(vals):
        top = functools.reduce(jnp.maximum, vals)
        idx = jnp.full(top.shape, len(vals) - 1, jnp.int32)
        for j in range(len(vals) - 2, -1, -1):
            idx = jnp.where(vals[j] >= top, j, idx)
        return top, idx

    gmax, gsel = first_argmax(gl)
    g_w = 1.0 / functools.reduce(lambda a, b: a + b, [jnp.exp(v - gmax) for v in gl])
    e_in = []
    for j in range(epg):
        v = el[(ng - 1) * epg + j]
        for gi in range(ng - 2, -1, -1):
            v = jnp.where(gsel == gi, el[gi * epg + j], v)
        e_in.append(v)
    emax = functools.reduce(jnp.maximum, e_in)
    pe = [jnp.exp(v - emax) for v in e_in]
    se = functools.reduce(lambda a, b: a + b, pe)
    prob = [v / se for v in pe]
    v1, i1 = first_argmax(prob)
    rest = [jnp.where(i1 == j, -1.0, prob[j]) for j in range(epg)]
    v2, i2 = first_argmax(rest)
    tot = v1 + v2
    e1 = gsel * epg + i1
    e2 = gsel * epg + i2

    tm = x.shape[0]
    eidx = lax.broadcasted_iota(jnp.int32, (N_EXPERTS, tm), 0)
    hit1 = eidx == e1
    hit2 = eidx == e2
    onehot = jnp.where(hit1 | hit2, 1.0, 0.0)
    tok_r = lax.broadcasted_iota(jnp.int32, (tm, tm), 0)
    tok_c = lax.broadcasted_iota(jnp.int32, (tm, tm), 1)
    earlier = jnp.where(tok_r < tok_c, 1.0, 0.0).astype(BF16)
    rank = jnp.dot(onehot.astype(BF16), earlier, preferred_element_type=F32)
    cnt = jnp.sum(onehot, axis=1, keepdims=True).astype(jnp.int32)
    seg = jnp.bitwise_and(cnt + (SEG_ALIGN - 1), -SEG_ALIGN)
    ex_r = lax.broadcasted_iota(jnp.int32, (N_EXPERTS, N_EXPERTS), 0)
    ex_c = lax.broadcasted_iota(jnp.int32, (N_EXPERTS, N_EXPERTS), 1)
    lower = jnp.where(ex_c < ex_r, 1.0, 0.0).astype(BF16)
    seg_lanes = jnp.broadcast_to(seg.astype(F32), (N_EXPERTS, 128))
    start = jnp.dot(lower, seg_lanes.astype(BF16), preferred_element_type=F32)[:, 0:1]
    slot = start + rank
    lpos1 = jnp.sum(jnp.where(hit1, slot, 0.0), axis=0, keepdims=True)
    lpos2 = jnp.sum(jnp.where(hit2, slot, 0.0), axis=0, keepdims=True)
    r_ref[...] = jnp.concatenate(
        [e1.astype(F32), e2.astype(F32), v1 / tot * g_w, v2 / tot * g_w, lpos1, lpos2,
         jnp.zeros((2, tm), F32)], axis=0)
    seg_ref[0] = jnp.broadcast_to(seg, (N_EXPERTS, 128))


def moe_router(x, g, w_group, b_group, w_expert, b_expert):
    m, d = x.shape
    tm = MOE_SUB
    nr = 32
    wr = jnp.zeros((nr, d), F32).at[:N_EXPERT_GROUPS].set(w_group.T)
    wr = wr.at[N_EXPERT_GROUPS:N_EXPERT_GROUPS + N_EXPERTS].set(w_expert.T)
    br = jnp.zeros((nr, 1), F32).at[:N_EXPERT_GROUPS, 0].set(b_group)
    br = br.at[N_EXPERT_GROUPS:N_EXPERT_GROUPS + N_EXPERTS, 0].set(b_expert)
    route, seg3 = pl.pallas_call(
        _router_kernel,
        grid=(m // tm,),
        in_specs=[
            pl.BlockSpec((tm, d), lambda i: (i, 0)),
            pl.BlockSpec((1, d), lambda i: (0, 0)),
            pl.BlockSpec((nr, d), lambda i: (0, 0)),
            pl.BlockSpec((nr, 1), lambda i: (0, 0)),
        ],
        out_specs=[pl.BlockSpec((8, tm), lambda i: (0, i)),
                   pl.BlockSpec((1, N_EXPERTS, 128), lambda i: (i, 0, 0))],
        out_shape=[jax.ShapeDtypeStruct((8, m), F32),
                   jax.ShapeDtypeStruct((m // tm, N_EXPERTS, 128), jnp.int32)],
        compiler_params=_cp(("parallel",), 40),
        name="moe_router",
    )(x, g.reshape(1, d), wr, br)
    return route, seg3[:, :, 0]


def moe_plan(seg, m):
    nsub = m // MOE_SUB
    tm = MOE_TILE
    loc_off = jnp.cumsum(seg, axis=1) - seg
    reg_off = jnp.cumsum(seg, axis=0) - seg
    length = jnp.sum(seg, axis=0)
    padded = (length + tm - 1) // tm * tm
    e_end = jnp.cumsum(padded)
    e_start = e_end - padded
    glob_off = e_start[None, :] + reg_off
    n_slots = (2 * m + nsub * N_EXPERTS * (SEG_ALIGN - 1) + N_EXPERTS * (tm - 1) + tm - 1) // tm * tm
    tile_start = jnp.arange(n_slots // tm, dtype=jnp.int32) * tm
    tile_expert = jnp.minimum(
        jnp.sum((tile_start[:, None] >= e_end[None, :]).astype(jnp.int32), axis=1), N_EXPERTS - 1)
    ar = jnp.arange(N_EXPERTS, dtype=jnp.int32)
    later = (ar[None, :] > ar[:, None]) & (padded[None, :] > 0)
    nxt = jnp.min(jnp.where(later, ar[None, :], N_EXPERTS), axis=1)
    nxt = jnp.where(nxt == N_EXPERTS, -1, nxt).astype(jnp.int32)
    tile_next = jnp.sum(jnp.where(tile_expert[:, None] == ar[None, :], nxt[None, :], 0), axis=1)
    return dict(seg=seg.reshape(-1), loc_off=loc_off.reshape(-1), glob_off=glob_off.reshape(-1),
                n_slots=n_slots, tile_expert=tile_expert, tile_next=tile_next,
                n_used=(e_end[-1:] // tm).astype(jnp.int32),
                pad_start=jnp.concatenate([e_start + length, e_end[-1:]]).astype(jnp.int32),
                pad_len=jnp.concatenate([padded - length, n_slots - e_end[-1:]]).astype(jnp.int32))


def _aligned(v):
    return v if isinstance(v, int) else pl.multiple_of(v, SEG_ALIGN)


def _seg_copy(src_ref, dst_ref, sem, src_row, dst_row, n_rows):
    n_rows = _aligned(n_rows)
    return pltpu.make_async_copy(src_ref.at[pl.ds(_aligned(src_row), n_rows)],
                                 dst_ref.at[pl.ds(_aligned(dst_row), n_rows)], sem)


def _segment_copies(step, seg_ref, loc_ref, glob_ref, make):
    out = []
    for e in range(N_EXPERTS):
        k = step * N_EXPERTS + e
        out.append((seg_ref[k], make(loc_ref[k], glob_ref[k], seg_ref[k])))
    return out


def _start_all(copies):
    for n, cp in copies:
        @pl.when(n > 0)
        def _(cp=cp):
            cp.start()


def _wait_all(copies):
    for n, cp in copies:
        @pl.when(n > 0)
        def _(cp=cp):
            cp.wait()


def _dispatch_kernel(seg_ref, loc_ref, glob_ref, pst_ref, pln_ref, x_ref, g_ref, rt_ref,
                     xs_ref, cbuf_ref, zbuf_ref, sems):
    s = pl.program_id(0)
    n_steps = pl.num_programs(0)
    buf = s % 2
    d = x_ref.shape[1]

    def copies(step, b):
        return _segment_copies(
            step, seg_ref, loc_ref, glob_ref,
            lambda loc, glob, n: _seg_copy(cbuf_ref.at[b], xs_ref, sems.at[b], loc, glob, n))

    @pl.when(s >= 2)
    def _():
        _wait_all(copies(s - 2, buf))

    x = x_ref[...]
    ms = jnp.mean(x * x, axis=-1, keepdims=True)
    t = (x * lax.rsqrt(ms + RMS_EPS) * g_ref[...]).astype(BF16)
    rt = rt_ref[...]
    lp = rt[4:6].astype(jnp.int32)
    slot = lax.broadcasted_iota(jnp.int32, (MOE_CBUF, MOE_SUB), 0)
    hit0 = slot == lp[0:1]
    hit1 = slot == lp[1:2]
    onehot = jnp.where(hit0 | hit1, 1.0, 0.0).astype(BF16)
    cbuf_ref[buf, :, 0:d] = jnp.dot(onehot, t, preferred_element_type=F32).astype(BF16)
    wslot = jnp.sum(jnp.where(hit0, rt[2:3], 0.0) + jnp.where(hit1, rt[3:4], 0.0),
                    axis=1, keepdims=True)
    w_hi = wslot.astype(BF16)
    w_lo = (wslot - w_hi.astype(F32)).astype(BF16)
    half = MOE_WCOLS // 2
    cbuf_ref[buf, :, d:d + half] = jnp.broadcast_to(w_hi, (MOE_CBUF, half))
    cbuf_ref[buf, :, d + half:] = jnp.broadcast_to(w_lo, (MOE_CBUF, half))
    _start_all(copies(s, buf))

    @pl.when(s == 0)
    def _():
        zsem = sems.at[2]
        zbuf_ref[...] = jnp.zeros(zbuf_ref.shape, BF16)
        pads = [(pln_ref[e], _seg_copy(zbuf_ref, xs_ref, zsem, 0, pst_ref[e], pln_ref[e]))
                for e in range(N_EXPERTS)]
        _start_all(pads)
        _wait_all(pads)
        tail_tiles = pln_ref[N_EXPERTS] // MOE_TILE

        def tail_copy(k):
            return _seg_copy(zbuf_ref, xs_ref, zsem, 0, pst_ref[N_EXPERTS] + k * MOE_TILE, MOE_TILE)

        def start(k, carry):
            tail_copy(k).start()
            return carry

        def wait(k, carry):
            tail_copy(k).wait()
            return carry
        lax.fori_loop(0, tail_tiles, start, 0)
        lax.fori_loop(0, tail_tiles, wait, 0)

    @pl.when(s == n_steps - 1)
    def _():
        @pl.when(s >= 1)
        def _():
            _wait_all(copies(s - 1, 1 - buf))
        _wait_all(copies(s, buf))


def moe_dispatch(x, g, route, plan):
    m, d = x.shape
    dw = d + MOE_WCOLS
    return pl.pallas_call(
        _dispatch_kernel,
        grid_spec=pltpu.PrefetchScalarGridSpec(
            num_scalar_prefetch=5,
            grid=(m // MOE_SUB,),
            in_specs=[
                pl.BlockSpec((MOE_SUB, d), lambda i, *_: (i, 0)),
                pl.BlockSpec((1, d), lambda i, *_: (0, 0)),
                pl.BlockSpec((8, MOE_SUB), lambda i, *_: (0, i)),
            ],
            out_specs=pl.BlockSpec(memory_space=pl.ANY),
            scratch_shapes=[pltpu.VMEM((2, MOE_CBUF, dw), BF16), pltpu.VMEM((MOE_TILE, dw), BF16),
                            pltpu.SemaphoreType.DMA((3,))],
        ),
        out_shape=jax.ShapeDtypeStruct((plan["n_slots"], dw), BF16),
        compiler_params=pltpu.CompilerParams(dimension_semantics=("arbitrary",),
                                             vmem_limit_bytes=40 * MIB, has_side_effects=True),
        name="moe_dispatch",
    )(plan["seg"], plan["loc_off"], plan["glob_off"], plan["pad_start"], plan["pad_len"],
      x, g.reshape(1, d), route)


def _ffn_kernel(te_ref, nx_ref, nu_ref, x_ref, wg_hbm, wu_hbm, wd_hbm, o_ref,
                wgs_ref, wus_ref, wds_ref, wgb_ref, wub_ref, wdb_ref, sem, *, li):
    i = pl.program_id(0)
    d = o_ref.shape[1]

    def weight_copies(e):
        return [pltpu.make_async_copy(wg_hbm.at[li, e], wgs_ref, sem),
                pltpu.make_async_copy(wu_hbm.at[li, e], wus_ref, sem),
                pltpu.make_async_copy(wd_hbm.at[li, e], wds_ref, sem)]

    @pl.when(i < nu_ref[0])
    def _():
        e = te_ref[i]

        @pl.when(i == 0)
        def _():
            for cp in weight_copies(e):
                cp.start()

        @pl.when((i == 0) | (e != te_ref[jnp.maximum(i - 1, 0)]))
        def _():
            for cp in weight_copies(e):
                cp.wait()
            wgb_ref[...] = wgs_ref[...].astype(BF16)
            wub_ref[...] = wus_ref[...].astype(BF16)
            wdb_ref[...] = wds_ref[...].astype(BF16)

            @pl.when(nx_ref[i] >= 0)
            def _():
                for cp in weight_copies(nx_ref[i]):
                    cp.start()

        x = x_ref[:, 0:d]
        half = MOE_WCOLS // 2
        w = x_ref[:, d:d + 1].astype(F32) + x_ref[:, d + half:d + half + 1].astype(F32)
        gate = jnp.dot(x, wgb_ref[...], preferred_element_type=F32)
        up = jnp.dot(x, wub_ref[...], preferred_element_type=F32)
        hid = (gate * _sigmoid(gate) * up * w).astype(BF16)
        o_ref[...] = jnp.dot(hid, wdb_ref[...], preferred_element_type=F32).astype(o_ref.dtype)

    @pl.when(i >= nu_ref[0])
    def _():
        o_ref[...] = jnp.zeros(o_ref.shape, o_ref.dtype)


def grouped_ffn(xs, plan, w_gate, w_up, w_down, li):
    p, dw = xs.shape
    d = dw - MOE_WCOLS
    f = D_EXPERT
    tm = MOE_TILE
    hbm = pl.BlockSpec(memory_space=pl.ANY)
    return pl.pallas_call(
        functools.partial(_ffn_kernel, li=li),
        grid_spec=pltpu.PrefetchScalarGridSpec(
            num_scalar_prefetch=3,
            grid=(p // tm,),
            in_specs=[
                pl.BlockSpec((tm, dw), lambda i, te, nx, nu: (jnp.minimum(i, nu[0] - 1), 0)),
                hbm, hbm, hbm,
            ],
            out_specs=pl.BlockSpec((tm, d), lambda i, te, nx, nu: (i, 0)),
            scratch_shapes=[pltpu.VMEM((d, f), F32), pltpu.VMEM((d, f), F32), pltpu.VMEM((f, d), F32),
                            pltpu.VMEM((d, f), BF16), pltpu.VMEM((d, f), BF16),
                            pltpu.VMEM((f, d), BF16), pltpu.SemaphoreType.DMA(())],
        ),
        out_shape=jax.ShapeDtypeStruct((p, d), BF16),
        compiler_params=_cp(("arbitrary",), 48),
        name="grouped_ffn",
    )(plan["tile_expert"], plan["tile_next"], plan["n_used"], xs, w_gate, w_up, w_down)


def _combine_kernel(seg_ref, loc_ref, glob_ref, x_ref, lp_ref, ys_ref, *rest, final_norm):
    if final_norm:
        g_ref, o_ref, ybuf_ref, sems = rest
    else:
        o_ref, ybuf_ref, sems = rest
    s = pl.program_id(0)
    n_steps = pl.num_programs(0)
    buf = s % 2

    def copies(step, b):
        return _segment_copies(
            step, seg_ref, loc_ref, glob_ref,
            lambda loc, glob, n: _seg_copy(ys_ref, ybuf_ref.at[b], sems.at[b], glob, loc, n))

    def fetch(step, b):
        ybuf_ref[b] = jnp.zeros(ybuf_ref.shape[1:], BF16)
        _start_all(copies(step, b))

    @pl.when(s == 0)
    def _():
        fetch(s, buf)

    @pl.when(s + 1 < n_steps)
    def _():
        fetch(s + 1, 1 - buf)

    _wait_all(copies(s, buf))
    lp = lp_ref[...]
    slot = lax.broadcasted_iota(jnp.int32, (MOE_SUB, MOE_CBUF), 1)
    onehot = jnp.where((slot == lp[:, 0:1]) | (slot == lp[:, 1:2]), 1.0, 0.0).astype(BF16)
    out = x_ref[...] + jnp.dot(onehot, ybuf_ref[buf], preferred_element_type=F32)
    if final_norm:
        ms = jnp.mean(out * out, axis=-1, keepdims=True)
        out = out * lax.rsqrt(ms + RMS_EPS) * g_ref[...]
    o_ref[...] = out


def moe_combine(x, ys, route, plan, final_g=None):
    m, d = x.shape
    in_specs = [
        pl.BlockSpec((MOE_SUB, d), lambda i, *_: (i, 0)),
        pl.BlockSpec((MOE_SUB, 2), lambda i, *_: (i, 0)),
        pl.BlockSpec(memory_space=pl.ANY),
    ]
    args = [x, route[4:6].T.astype(jnp.int32), ys]
    if final_g is not None:
        in_specs.append(pl.BlockSpec((1, d), lambda i, *_: (0, 0)))
        args.append(final_g.reshape(1, d))
    return pl.pallas_call(
        functools.partial(_combine_kernel, final_norm=final_g is not None),
        grid_spec=pltpu.PrefetchScalarGridSpec(
            num_scalar_prefetch=3,
            grid=(m // MOE_SUB,),
            in_specs=in_specs,
            out_specs=pl.BlockSpec((MOE_SUB, d), lambda i, *_: (i, 0)),
            scratch_shapes=[pltpu.VMEM((2, MOE_CBUF, d), BF16), pltpu.SemaphoreType.DMA((2,))],
        ),
        out_shape=jax.ShapeDtypeStruct((m, d), F32),
        compiler_params=pltpu.CompilerParams(dimension_semantics=("arbitrary",),
                                             vmem_limit_bytes=40 * MIB),
        name="moe_combine",
    )(plan["seg"], plan["loc_off"], plan["glob_off"], *args)


def hier_moe_layer(x, norm_g, w_group, b_group, w_expert, b_expert, w_gate, w_up, w_down, li,
                   final_g=None):
    m, _ = x.shape
    route, seg = moe_router(x, norm_g, w_group, b_group, w_expert, b_expert)
    plan = moe_plan(seg, m)
    xs = moe_dispatch(x, norm_g, route, plan)
    ys = grouped_ffn(xs, plan, w_gate, w_up, w_down, li)
    return moe_combine(x, ys, route, plan, final_g)


def _rope_tables(seq, dim):
    pos = jnp.arange(seq, dtype=F32)
    inv = ROPE_THETA ** (-jnp.arange(0, dim, 2, dtype=F32) / dim)
    ang = pos[:, None] * inv[None, :]
    ang = jnp.concatenate([ang, ang], axis=-1)
    sign = jnp.concatenate([-jnp.ones((dim // 2,), F32), jnp.ones((dim // 2,), F32)])
    return jnp.cos(ang), jnp.sin(ang) * sign[None, :]


def _lambda_init(depth_idx):
    return 0.8 - 0.6 * math.exp(-0.3 * depth_idx)


def kernel(x, mem, norm_mix, norm_mem, norm_ffn, norm_final, w_out, w_mem_kv, pool_w_in, pool_w_grp, pool_scale, diff_w_in, diff_lambda, diff_subln, ssd_w_in, ssd_conv_w, ssd_conv_b, ssd_dt_bias, ssd_a_log, ssd_d, ssd_norm, moe_w_group, moe_b_group, moe_w_expert, moe_b_expert, moe_w_gate, moe_w_up, moe_w_down):
    b, s, d = x.shape
    m = b * s
    mem_len = mem.shape[1]
    cos, sin_signed = _rope_tables(s, DIFF_HEAD_DIM)
    xt = x.reshape(m, d)
    memt = mem.reshape(b * mem_len, d)
    for i in range(DEPTH):
        kind, slot = i % N_MIXERS, i // N_MIXERS
        mem_kv = norm_matmul(memt, norm_mem[i], w_mem_kv, i, 2 * MEM_WIDTH, b * mem_len, 512, F32)
        mem_kv = mem_kv.reshape(b, mem_len, 2 * MEM_WIDTH)
        if kind == 0:
            n_in = MIX_WIDTH + MEM_WIDTH
            proj = norm_matmul(xt, norm_mix[i], pool_w_in, slot, n_in, IN_TM, IN_TN, BF16)
            proj = proj.reshape(b, s, n_in)
            mix = pool_mixer(proj, pool_w_grp, slot, pool_scale[slot])
        elif kind == 1:
            n_in = 3 * MIX_WIDTH + MEM_WIDTH
            proj = norm_matmul(xt, norm_mix[i], diff_w_in, slot, n_in, IN_TM, IN_TN, BF16,
                               rope=(cos, sin_signed, MIX_WIDTH, MIX_WIDTH, s))
            proj = proj.reshape(b, s, n_in)
            mix = diff_attention(proj, diff_lambda[slot], diff_subln[slot], _lambda_init(i))
        else:
            n_main = MIX_WIDTH + SSD_CONV_DIM
            w_tail = jnp.concatenate(
                [ssd_w_in[slot, :, SSD_MIX_IN:], ssd_w_in[slot, :, n_main:SSD_MIX_IN],
                 jnp.zeros((d, SSD_TAIL - MEM_WIDTH - SSD_HEADS), F32)], axis=1)
            n_in = n_main + SSD_TAIL
            proj = norm_matmul(xt, norm_mix[i], ssd_w_in, slot, n_main, IN_TM, IN_TN, F32,
                               tail=w_tail)
            proj = proj.reshape(b, s, n_in)
            xbc = ssd_conv(proj, ssd_conv_w[slot], ssd_conv_b[slot])
            dt_off = n_main + MEM_WIDTH
            dt_raw = proj[:, :, dt_off:dt_off + SSD_HEADS]
            mix = ssd_scan(xbc, proj, dt_raw, ssd_dt_bias[slot], ssd_a_log[slot], ssd_d[slot],
                           ssd_norm[slot])
        q_block = (MIX_WIDTH if kind == 0 else 3 * MIX_WIDTH if kind == 1
                   else MIX_WIDTH + SSD_CONV_DIM) // MEM_WIDTH
        mem_out = memory_attention(proj, q_block, mem_kv)
        xt = out_projection(xt, mix.reshape(m, MIX_WIDTH), mem_out.reshape(m, MEM_WIDTH), w_out, i)
        xt = hier_moe_layer(xt, norm_ffn[i], moe_w_group[i], moe_b_group[i], moe_w_expert[i],
                            moe_b_expert[i], moe_w_gate, moe_w_up, moe_w_down, i,
                            final_g=norm_final if i == DEPTH - 1 else None)
    return xt.reshape(b, s, d)
```

```python
import functools
import math

import jax
import jax.numpy as jnp
from jax import lax
from jax.experimental import pallas as pl
from jax.experimental.pallas import tpu as pltpu

F32 = jnp.float32
BF16 = jnp.bfloat16

D_MODEL = 2048
DEPTH = 4
N_MIXERS = 3
MIX_WIDTH = 1536
MEM_WIDTH = 512
MEM_HEADS = 4
MEM_HEAD_DIM = 128
POOL_WINDOWS = (2, 4, 8, 16)
POOL_GROUPS = 4
POOL_GROUP_DIM = 384
DIFF_HEAD_DIM = 128
DIFF_HEADS = 6
DIFF_V_DIM = 256
ROPE_THETA = 10000.0
SSD_HEAD_DIM = 64
SSD_HEADS = 24
SSD_GROUPS = 4
SSD_HEADS_PER_GROUP = 6
SSD_STATE = 128
SSD_CONV = 4
SSD_CHUNK = 128
SSD_CONV_DIM = 2560
SSD_MIX_IN = 4120
SSD_TAIL = 1024
N_EXPERT_GROUPS = 4
EXPERTS_PER_GROUP = 4
N_EXPERTS = 16
D_EXPERT = 512
RMS_EPS = 1e-6

IN_TM, IN_TN = 2048, 256
IN_TN_PRENORMED = 512
MOE_TILE = 512
MOE_SUB = 512
SEG_ALIGN = 16
MOE_CBUF = 2 * MOE_SUB + 256
MOE_WCOLS = 256
MIB = 1024 * 1024

NT_DIMS = (((1,), (1,)), ((), ()))
TN_DIMS = (((0,), (0,)), ((), ()))


def _cp(sem, vmem_mib):
    return pltpu.CompilerParams(dimension_semantics=sem, vmem_limit_bytes=vmem_mib * MIB)


def _sigmoid(x):
    return 1.0 / (1.0 + jnp.exp(-x))


def _softplus(x):
    return jnp.maximum(x, 0.0) + jnp.log1p(jnp.exp(-jnp.abs(x)))


def _split3(v):
    hi = v.astype(BF16)
    r = v - hi.astype(F32)
    mid = r.astype(BF16)
    lo = (r - mid.astype(F32)).astype(BF16)
    return hi, mid, lo


def _norm_matmul_kernel(*refs, rope_tiles, main_tiles, prenormed):
    refs = list(refs)
    x_ref = refs.pop(0)
    g_ref = None if prenormed else refs.pop(0)
    w_ref = refs.pop(0)
    if rope_tiles is not None:
        cos_ref, sin_ref = refs.pop(0), refs.pop(0)
    if main_tiles is not None:
        wt_ref = refs.pop(0)
    o_ref = refs.pop(0)
    j = pl.program_id(1)

    if prenormed:
        h_ref = x_ref
    else:
        h_ref = refs.pop(0)

        @pl.when(j == 0)
        def _():
            x = x_ref[...]
            ms = jnp.mean(x * x, axis=-1, keepdims=True)
            h_ref[...] = (x * lax.rsqrt(ms + RMS_EPS) * g_ref[...]).astype(BF16)

    def product(weights_ref):
        return jnp.dot(h_ref[...], weights_ref[...].astype(BF16), preferred_element_type=F32)

    if main_tiles is not None:
        @pl.when(j < main_tiles)
        def _():
            o_ref[...] = product(w_ref).astype(o_ref.dtype)

        @pl.when(j >= main_tiles)
        def _():
            o_ref[...] = product(wt_ref).astype(o_ref.dtype)
        return
    acc = product(w_ref)
    if rope_tiles is None:
        o_ref[...] = acc.astype(o_ref.dtype)
        return
    n_q, n_k = rope_tiles
    hd = DIFF_HEAD_DIM

    @pl.when(j < n_q + n_k)
    def _():
        scale = jnp.where(j < n_q, hd ** -0.5, 1.0)
        cos = cos_ref[...]
        sin = sin_ref[...]
        for c in range(o_ref.shape[1] // hd):
            x = acc[:, c * hd:(c + 1) * hd]
            r = x * cos + pltpu.roll(x, hd // 2, axis=1) * sin
            o_ref[:, c * hd:(c + 1) * hd] = (r * scale).astype(o_ref.dtype)

    @pl.when(j >= n_q + n_k)
    def _():
        o_ref[...] = acc.astype(o_ref.dtype)


def norm_matmul(x, g, w, li, n_cols, tm, tn, out_dtype, rope=None, tail=None):
    m, k = x.shape
    prenormed = g is None
    main_tiles = None if tail is None else n_cols // tn
    w_map = ((lambda i, j: (li, 0, j)) if tail is None
             else (lambda i, j: (li, 0, jnp.minimum(j, main_tiles - 1))))
    in_specs = [pl.BlockSpec((tm, k), lambda i, j: (i, 0))]
    args = [x]
    if not prenormed:
        in_specs.append(pl.BlockSpec((1, k), lambda i, j: (0, 0)))
        args.append(g.reshape(1, k))
    in_specs.append(pl.BlockSpec((None, k, tn), w_map))
    args.append(w)
    rope_tiles = None
    if rope is not None:
        cos, sin_signed, n_q, n_k, seq = rope
        rope_tiles = (n_q // tn, n_k // tn)
        pos_blocks = seq // tm
        tab = pl.BlockSpec((tm, DIFF_HEAD_DIM), lambda i, j: (i % pos_blocks, 0))
        in_specs += [tab, tab]
        args += [cos, sin_signed]
    if tail is not None:
        in_specs.append(pl.BlockSpec((k, tn), lambda i, j: (0, jnp.maximum(j - main_tiles, 0))))
        args.append(tail)
        n_cols = n_cols + tail.shape[1]
    return pl.pallas_call(
        functools.partial(_norm_matmul_kernel, rope_tiles=rope_tiles, main_tiles=main_tiles,
                          prenormed=prenormed),
        grid=(m // tm, n_cols // tn),
        in_specs=in_specs,
        out_specs=pl.BlockSpec((tm, tn), lambda i, j: (i, j)),
        out_shape=jax.ShapeDtypeStruct((m, n_cols), out_dtype),
        scratch_shapes=[] if prenormed else [pltpu.VMEM((tm, k), BF16)],
        compiler_params=_cp(("parallel", "parallel" if prenormed else "arbitrary"), 58),
        name="norm_matmul",
    )(*args)


def _pool_kernel(u_ref, w_ref, sc_ref, o_ref, pad_ref):
    grp = pl.program_id(1)
    s, c = u_ref.shape[1], u_ref.shape[2]
    rows = 256
    pad_ref[0:16, :] = jnp.zeros((16, c), F32)
    pad_ref[16:, :] = u_ref[0].astype(F32)
    wb = w_ref[...].astype(BF16)
    sc = sc_ref[0]

    for gi, win in enumerate(POOL_WINDOWS):
        @pl.when(grp == gi)
        def _(win=win):
            for r in range(s // rows):
                xh = pad_ref[r * rows:r * rows + rows + 16, :]
                acc = xh
                k = 1
                while k < win:
                    acc = acc + pltpu.roll(acc, k, axis=0)
                    k *= 2
                t = r * rows + lax.broadcasted_iota(jnp.int32, (rows, 1), 0)
                cnt = jnp.minimum(t + 1, win).astype(F32)
                mixed = (acc[16:, :] / cnt - xh[16:, :]).astype(BF16)
                o_ref[0, r * rows:(r + 1) * rows, :] = (
                    jnp.dot(mixed, wb, preferred_element_type=F32) * sc).astype(o_ref.dtype)


def pool_mixer(proj3, w_grp, li, scale):
    b, s, _ = proj3.shape
    c = POOL_GROUP_DIM
    return pl.pallas_call(
        _pool_kernel,
        grid=(b, POOL_GROUPS),
        in_specs=[
            pl.BlockSpec((1, s, c), lambda i, g: (i, 0, g)),
            pl.BlockSpec((None, None, c, c), lambda i, g: (li, g, 0, 0)),
            pl.BlockSpec((1, 1, c), lambda i, g: (g, 0, 0)),
        ],
        out_specs=pl.BlockSpec((1, s, c), lambda i, g: (i, 0, g)),
        out_shape=jax.ShapeDtypeStruct((b, s, MIX_WIDTH), BF16),
        scratch_shapes=[pltpu.VMEM((s + 16, c), F32)],
        compiler_params=_cp(("parallel", "parallel"), 40),
        name="pool_mixer",
    )(proj3, w_grp, scale.reshape(POOL_GROUPS, 1, c))


def _diff_attn_kernel(lam_ref, sub_ref, q_ref, k_ref, v_ref, o_ref, *, lambda_init):
    lam = lam_ref[...]
    s1 = jnp.sum(lam[0:1] * lam[1:2], axis=-1, keepdims=True)
    s2 = jnp.sum(lam[2:3] * lam[3:4], axis=-1, keepdims=True)
    lmbda = jnp.exp(s1) - jnp.exp(s2) + lambda_init
    s = q_ref.shape[1]
    tq = 512
    d = DIFF_HEAD_DIM
    for i in range(s // tq):
        kv = (i + 1) * tq
        q = q_ref[0, i * tq:(i + 1) * tq, :]
        row = i * tq + lax.broadcasted_iota(jnp.int32, (tq, kv), 0)
        col = lax.broadcasted_iota(jnp.int32, (tq, kv), 1)
        mask = col <= row
        outs = []
        for c in range(2):
            sc = lax.dot_general(q[:, c * d:(c + 1) * d], k_ref[0, 0:kv, c * d:(c + 1) * d],
                                 NT_DIMS, preferred_element_type=F32)
            sc = jnp.where(mask, sc, -jnp.inf)
            e = jnp.exp(sc - jnp.max(sc, axis=-1, keepdims=True))
            pv = jnp.dot(e.astype(BF16), v_ref[0, 0:kv, :], preferred_element_type=F32)
            outs.append(pv / jnp.sum(e, axis=-1, keepdims=True))
        o = outs[0] - lmbda * outs[1]
        ms = jnp.mean(o * o, axis=-1, keepdims=True)
        o_ref[0, i * tq:(i + 1) * tq, :] = (
            o * lax.rsqrt(ms + RMS_EPS) * sub_ref[...] * (1.0 - lambda_init)).astype(o_ref.dtype)


def diff_attention(proj3, lam, subln, lambda_init):
    b, s, _ = proj3.shape
    vd = DIFF_V_DIM
    nh = DIFF_HEADS
    return pl.pallas_call(
        functools.partial(_diff_attn_kernel, lambda_init=lambda_init),
        grid=(b, nh),
        in_specs=[
            pl.BlockSpec((4, DIFF_HEAD_DIM), lambda i, h: (0, 0)),
            pl.BlockSpec((1, vd), lambda i, h: (0, 0)),
            pl.BlockSpec((1, s, vd), lambda i, h: (i, 0, h)),
            pl.BlockSpec((1, s, vd), lambda i, h: (i, 0, nh + h)),
            pl.BlockSpec((1, s, vd), lambda i, h: (i, 0, 2 * nh + h)),
        ],
        out_specs=pl.BlockSpec((1, s, vd), lambda i, h: (i, 0, h)),
        out_shape=jax.ShapeDtypeStruct((b, s, MIX_WIDTH), BF16),
        compiler_params=_cp(("parallel", "parallel"), 48),
        name="diff_attention",
    )(lam, subln.reshape(1, vd), proj3, proj3, proj3)


CONV_HALO = 8


def _causal_conv_silu(u_ref, halo_ref, w_ref, b_ref):
    cur = u_ref[0]
    ext = jnp.concatenate([halo_ref[...], cur], axis=0)
    w = w_ref[...]
    y = ext * w[SSD_CONV - 1:SSD_CONV]
    for j in range(1, SSD_CONV):
        y = y + pltpu.roll(ext, j, axis=0) * w[SSD_CONV - 1 - j:SSD_CONV - j]
    halo_ref[...] = cur[cur.shape[0] - CONV_HALO:, :]
    y = y[CONV_HALO:, :] + b_ref[...]
    return y * _sigmoid(y)


def _dot3(a_f32, b_bf16):
    return sum(jnp.dot(t, b_bf16, preferred_element_type=F32) for t in _split3(a_f32))


def _ssd_kernel(x_ref, b_ref, c_ref, z_ref, dt_ref, wx_ref, wb_ref, wc_ref, cbx_ref, cbb_ref, cbc_ref,
                bias_ref, alog_ref, dx_ref, ng_ref, o_ref,
                state_ref, y_ref, xs_ref, hx_ref, hb_ref, hc_ref):
    @pl.when(pl.program_id(2) == 0)
    def _():
        state_ref[...] = jnp.zeros(state_ref.shape, F32)
        hx_ref[...] = jnp.zeros(hx_ref.shape, F32)
        hb_ref[...] = jnp.zeros(hb_ref.shape, F32)
        hc_ref[...] = jnp.zeros(hc_ref.shape, F32)

    ln = SSD_CHUNK
    hg = SSD_HEADS_PER_GROUP
    lanes = 2 * SSD_HEAD_DIM
    xs_ref[...] = _causal_conv_silu(x_ref, hx_ref, wx_ref, cbx_ref)
    bm = _causal_conv_silu(b_ref, hb_ref, wb_ref, cbb_ref)
    bm_t = bm.T.astype(BF16)
    cm = _causal_conv_silu(c_ref, hc_ref, wc_ref, cbc_ref).astype(BF16)
    row = lax.broadcasted_iota(jnp.int32, (ln, ln), 0)
    col = lax.broadcasted_iota(jnp.int32, (ln, ln), 1)
    causal = col <= row
    ones_lower = jnp.where(causal, 1.0, 0.0).astype(BF16)

    dt = _softplus(dt_ref[0] + bias_ref[0])
    acs = sum(jnp.dot(ones_lower, t, preferred_element_type=F32)
              for t in _split3(dt * (-jnp.exp(alog_ref[0]))))
    acs_rows = acs.T
    sel_r = lax.broadcasted_iota(jnp.int32, (lanes, hg * lanes), 0)
    sel_c = lax.broadcasted_iota(jnp.int32, (lanes, hg * lanes), 1)
    spread = jnp.where(jnp.right_shift(sel_c, lanes.bit_length() - 1) == sel_r,
                       1.0, 0.0).astype(BF16)
    dt_all = _dot3(dt, spread)
    acs_all = _dot3(acs, spread)

    scores = lax.dot_general(cm, bm.astype(BF16), NT_DIMS, preferred_element_type=F32)
    low_half = lax.broadcasted_iota(jnp.int32, (ln, lanes), 1) < SSD_HEAD_DIM
    for k in range(hg // 2):
        heads = (2 * k, 2 * k + 1)
        tile = lambda a, h: a[:, h * lanes:(h + 1) * lanes]
        x = xs_ref[:, k * lanes:(k + 1) * lanes]
        dt_p = jnp.where(low_half, tile(dt_all, heads[0]), tile(dt_all, heads[1]))
        acs_p = jnp.where(low_half, tile(acs_all, heads[0]), tile(acs_all, heads[1]))
        xc = x * dt_p
        prev = state_ref[k]
        y = jnp.dot(cm, prev.astype(BF16), preferred_element_type=F32) * jnp.exp(acs_p)
        for h, own in zip(heads, (low_half, ~low_half)):
            decay = jnp.exp(jnp.where(causal, tile(acs_all, h) - acs_rows[h:h + 1, :], -jnp.inf))
            y = y + jnp.dot((scores * decay).astype(BF16), jnp.where(own, xc, 0.0).astype(BF16),
                            preferred_element_type=F32)
        a_last = acs_p[ln - 1:ln, :]
        st = jnp.dot(bm_t, (xc * jnp.exp(a_last - acs_p)).astype(BF16),
                     preferred_element_type=F32)
        state_ref[k] = prev * jnp.exp(a_last) + st
        y_ref[:, k * lanes:(k + 1) * lanes] = y

    z = z_ref[0]
    yz = (y_ref[...] + dx_ref[0] * xs_ref[...]) * (z * _sigmoid(z))
    ms = jnp.mean(yz * yz, axis=-1, keepdims=True)
    o_ref[0] = (yz * lax.rsqrt(ms + RMS_EPS) * ng_ref[0]).astype(o_ref.dtype)


def ssd_scan(proj3, dt_raw, conv_w, conv_b, dt_bias, a_log, d_skip, norm_g):
    b, s, _ = proj3.shape
    ln, g, hg, n = SSD_CHUNK, SSD_GROUPS, SSD_HEADS_PER_GROUP, SSD_STATE
    gw = hg * SSD_HEAD_DIM
    lanes = 2 * SSD_HEAD_DIM
    assert ln == lanes and n == lanes

    def head_lanes(v):
        v = v.reshape(v.shape[:-1] + (g, hg))
        v = jnp.pad(v, [(0, 0)] * (v.ndim - 1) + [(0, lanes - hg)])
        return v.reshape(v.shape[:-2] + (g * lanes,))

    per_group = pl.BlockSpec((1, 1, lanes), lambda i, j, c: (j, 0, 0))
    d_chan = jnp.repeat(d_skip, SSD_HEAD_DIM)
    x_blk, b_blk = MIX_WIDTH // gw, (2 * MIX_WIDTH) // n
    c_blk = b_blk + g
    cw_b, cw_c = MIX_WIDTH // n, MIX_WIDTH // n + g
    conv_b2 = conv_b.reshape(1, SSD_CONV_DIM)
    return pl.pallas_call(
        _ssd_kernel,
        grid=(b, g, s // ln),
        in_specs=[
            pl.BlockSpec((1, ln, gw), lambda i, j, c: (i, c, x_blk + j)),
            pl.BlockSpec((1, ln, n), lambda i, j, c: (i, c, b_blk + j)),
            pl.BlockSpec((1, ln, n), lambda i, j, c: (i, c, c_blk + j)),
            pl.BlockSpec((1, ln, gw), lambda i, j, c: (i, c, j)),
            pl.BlockSpec((1, ln, lanes), lambda i, j, c: (i, c, j)),
            pl.BlockSpec((SSD_CONV, gw), lambda i, j, c: (0, j)),
            pl.BlockSpec((SSD_CONV, n), lambda i, j, c: (0, cw_b + j)),
            pl.BlockSpec((SSD_CONV, n), lambda i, j, c: (0, cw_c + j)),
            pl.BlockSpec((1, gw), lambda i, j, c: (0, j)),
            pl.BlockSpec((1, n), lambda i, j, c: (0, cw_b + j)),
            pl.BlockSpec((1, n), lambda i, j, c: (0, cw_c + j)),
            per_group, per_group,
            pl.BlockSpec((1, 1, gw), lambda i, j, c: (j, 0, 0)),
            pl.BlockSpec((1, 1, gw), lambda i, j, c: (j, 0, 0)),
        ],
        out_specs=pl.BlockSpec((1, ln, gw), lambda i, j, c: (i, c, j)),
        out_shape=jax.ShapeDtypeStruct((b, s, MIX_WIDTH), BF16),
        scratch_shapes=[pltpu.VMEM((hg // 2, n, lanes), F32), pltpu.VMEM((ln, gw), F32),
                        pltpu.VMEM((ln, gw), F32), pltpu.VMEM((CONV_HALO, gw), F32),
                        pltpu.VMEM((CONV_HALO, n), F32), pltpu.VMEM((CONV_HALO, n), F32)],
        compiler_params=_cp(("parallel", "parallel", "arbitrary"), 32),
        name="ssd_scan",
    )(proj3, proj3, proj3, proj3, head_lanes(dt_raw), conv_w, conv_w, conv_w,
      conv_b2, conv_b2, conv_b2, head_lanes(dt_bias).reshape(g, 1, lanes),
      head_lanes(a_log).reshape(g, 1, lanes), d_chan.reshape(g, 1, gw), norm_g.reshape(g, 1, gw))


def _mem_attn_kernel(q_ref, kv_ref, o_ref):
    d = MEM_HEAD_DIM
    for h in range(MEM_HEADS):
        q = (q_ref[0, :, h * d:(h + 1) * d].astype(F32) * (d ** -0.5)).astype(BF16)
        k = kv_ref[0, :, h * d:(h + 1) * d].astype(BF16)
        v = kv_ref[0, :, MEM_WIDTH + h * d:MEM_WIDTH + (h + 1) * d].astype(BF16)
        sc = lax.dot_general(q, k, NT_DIMS, preferred_element_type=F32)
        e = jnp.exp(sc - jnp.max(sc, axis=-1, keepdims=True))
        pr = (e / jnp.sum(e, axis=-1, keepdims=True)).astype(BF16)
        o_ref[0, :, h * d:(h + 1) * d] = jnp.dot(
            pr, v, preferred_element_type=F32).astype(o_ref.dtype)


def memory_attention(proj3, q_block, mem_kv):
    b, s, _ = proj3.shape
    m = mem_kv.shape[1]
    tq = 512
    return pl.pallas_call(
        _mem_attn_kernel,
        grid=(b, s // tq),
        in_specs=[
            pl.BlockSpec((1, tq, MEM_WIDTH), lambda i, j: (i, j, q_block)),
            pl.BlockSpec((1, m, 2 * MEM_WIDTH), lambda i, j: (i, 0, 0)),
        ],
        out_specs=pl.BlockSpec((1, tq, MEM_WIDTH), lambda i, j: (i, j, 0)),
        out_shape=jax.ShapeDtypeStruct((b, s, MEM_WIDTH), BF16),
        compiler_params=_cp(("parallel", "parallel"), 32),
        name="memory_attention",
    )(proj3, mem_kv)


def _outproj_kernel(x_ref, a_ref, m_ref, wa_ref, wm_ref, o_ref):
    acc = jnp.dot(a_ref[...], wa_ref[...].astype(BF16), preferred_element_type=F32)
    acc = acc + jnp.dot(m_ref[...], wm_ref[...].astype(BF16), preferred_element_type=F32)
    o_ref[...] = x_ref[...] + acc


def out_projection(x, mix, mem_out, w_out, li):
    m, d = x.shape
    tm, tn = 2048, 512
    return pl.pallas_call(
        _outproj_kernel,
        grid=(m // tm, d // tn),
        in_specs=[
            pl.BlockSpec((tm, tn), lambda i, j: (i, j)),
            pl.BlockSpec((tm, MIX_WIDTH), lambda i, j: (i, 0)),
            pl.BlockSpec((tm, MEM_WIDTH), lambda i, j: (i, 0)),
            pl.BlockSpec((None, MIX_WIDTH, tn), lambda i, j: (li, 0, j)),
            pl.BlockSpec((None, MEM_WIDTH, tn), lambda i, j: (li, MIX_WIDTH // MEM_WIDTH, j)),
        ],
        out_specs=pl.BlockSpec((tm, tn), lambda i, j: (i, j)),
        out_shape=jax.ShapeDtypeStruct((m, d), F32),
        compiler_params=_cp(("parallel", "parallel"), 48),
        name="out_projection",
    )(x, mix, mem_out, w_out, w_out)


def _router_kernel(x_ref, g_ref, wr_ref, br_ref, r_ref, seg_ref):
    x = x_ref[...]
    ms = jnp.mean(x * x, axis=-1, keepdims=True)
    t = x * lax.rsqrt(ms + RMS_EPS) * g_ref[...]
    th = t.astype(BF16)
    tl = (t - th.astype(F32)).astype(BF16)
    w = wr_ref[...]
    wh = w.astype(BF16)
    wl = (w - wh.astype(F32)).astype(BF16)
    lg = (lax.dot_general(wh, th, NT_DIMS, preferred_element_type=F32)
          + lax.dot_general(wh, tl, NT_DIMS, preferred_element_type=F32)
          + lax.dot_general(wl, th, NT_DIMS, preferred_element_type=F32)) + br_ref[...]
    ng, epg = N_EXPERT_GROUPS, EXPERTS_PER_GROUP
    gl = [lg[j:j + 1] for j in range(ng)]
    el = [lg[ng + j:ng + j + 1] for j in range(N_EXPERTS)]

    def first_argmax(vals):
        top = functools.reduce(jnp.maximum, vals)
        idx = jnp.full(top.shape, len(vals) - 1, jnp.int32)
        for j in range(len(vals) - 2, -1, -1):
            idx = jnp.where(vals[j] >= top, j, idx)
        return top, idx

    gmax, gsel = first_argmax(gl)
    g_w = 1.0 / functools.reduce(lambda a, b: a + b, [jnp.exp(v - gmax) for v in gl])
    e_in = []
    for j in range(epg):
        v = el[(ng - 1) * epg + j]
        for gi in range(ng - 2, -1, -1):
            v = jnp.where(gsel == gi, el[gi * epg + j], v)
        e_in.append(v)
    emax = functools.reduce(jnp.maximum, e_in)
    pe = [jnp.exp(v - emax) for v in e_in]
    se = functools.reduce(lambda a, b: a + b, pe)
    prob = [v / se for v in pe]
    v1, i1 = first_argmax(prob)
    rest = [jnp.where(i1 == j, -1.0, prob[j]) for j in range(epg)]
    v2, i2 = first_argmax(rest)
    tot = v1 + v2
    e1 = gsel * epg + i1
    e2 = gsel * epg + i2

    tm = x.shape[0]
    eidx = lax.broadcasted_iota(jnp.int32, (N_EXPERTS, tm), 0)
    hit1 = eidx == e1
    hit2 = eidx == e2
    onehot = jnp.where(hit1 | hit2, 1.0, 0.0)
    tok_r = lax.broadcasted_iota(jnp.int32, (tm, tm), 0)
    tok_c = lax.broadcasted_iota(jnp.int32, (tm, tm), 1)
    earlier = jnp.where(tok_r < tok_c, 1.0, 0.0).astype(BF16)
    rank = jnp.dot(onehot.astype(BF16), earlier, preferred_element_type=F32)
    cnt = jnp.sum(onehot, axis=1, keepdims=True).astype(jnp.int32)
    seg = jnp.bitwise_and(cnt + (SEG_ALIGN - 1), -SEG_ALIGN)
    ex_r = lax.broadcasted_iota(jnp.int32, (N_EXPERTS, N_EXPERTS), 0)
    ex_c = lax.broadcasted_iota(jnp.int32, (N_EXPERTS, N_EXPERTS), 1)
    lower = jnp.where(ex_c < ex_r, 1.0, 0.0).astype(BF16)
    seg_lanes = jnp.broadcast_to(seg.astype(F32), (N_EXPERTS, 128))
    start = jnp.dot(lower, seg_lanes.astype(BF16), preferred_element_type=F32)[:, 0:1]
    slot = start + rank
    lpos1 = jnp.sum(jnp.where(hit1, slot, 0.0), axis=0, keepdims=True)
    lpos2 = jnp.sum(jnp.where(hit2, slot, 0.0), axis=0, keepdims=True)
    r_ref[...] = jnp.concatenate(
        [e1.astype(F32), e2.astype(F32), v1 / tot * g_w, v2 / tot * g_w, lpos1, lpos2,
         jnp.zeros((2, tm), F32)], axis=0)
    seg_ref[0] = jnp.broadcast_to(seg, (N_EXPERTS, 128))


def moe_router(x, g, w_group, b_group, w_expert, b_expert):
    m, d = x.shape
    tm = MOE_SUB
    nr = 32
    wr = jnp.zeros((nr, d), F32).at[:N_EXPERT_GROUPS].set(w_group.T)
    wr = wr.at[N_EXPERT_GROUPS:N_EXPERT_GROUPS + N_EXPERTS].set(w_expert.T)
    br = jnp.zeros((nr, 1), F32).at[:N_EXPERT_GROUPS, 0].set(b_group)
    br = br.at[N_EXPERT_GROUPS:N_EXPERT_GROUPS + N_EXPERTS, 0].set(b_expert)
    route, seg3 = pl.pallas_call(
        _router_kernel,
        grid=(m // tm,),
        in_specs=[
            pl.BlockSpec((tm, d), lambda i: (i, 0)),
            pl.BlockSpec((1, d), lambda i: (0, 0)),
            pl.BlockSpec((nr, d), lambda i: (0, 0)),
            pl.BlockSpec((nr, 1), lambda i: (0, 0)),
        ],
        out_specs=[pl.BlockSpec((8, tm), lambda i: (0, i)),
                   pl.BlockSpec((1, N_EXPERTS, 128), lambda i: (i, 0, 0))],
        out_shape=[jax.ShapeDtypeStruct((8, m), F32),
                   jax.ShapeDtypeStruct((m // tm, N_EXPERTS, 128), jnp.int32)],
        compiler_params=_cp(("parallel",), 40),
        name="moe_router",
    )(x, g.reshape(1, d), wr, br)
    return route, seg3[:, :, 0]


def moe_plan(seg, m):
    nsub = m // MOE_SUB
    tm = MOE_TILE
    loc_off = jnp.cumsum(seg, axis=1) - seg
    reg_off = jnp.cumsum(seg, axis=0) - seg
    length = jnp.sum(seg, axis=0)
    padded = (length + tm - 1) // tm * tm
    e_end = jnp.cumsum(padded)
    e_start = e_end - padded
    glob_off = e_start[None, :] + reg_off
    n_slots = (2 * m + nsub * N_EXPERTS * (SEG_ALIGN - 1) + N_EXPERTS * (tm - 1) + tm - 1) // tm * tm
    tile_start = jnp.arange(n_slots // tm, dtype=jnp.int32) * tm
    tile_expert = jnp.minimum(
        jnp.sum((tile_start[:, None] >= e_end[None, :]).astype(jnp.int32), axis=1), N_EXPERTS - 1)
    ar = jnp.arange(N_EXPERTS, dtype=jnp.int32)
    later = (ar[None, :] > ar[:, None]) & (padded[None, :] > 0)
    nxt = jnp.min(jnp.where(later, ar[None, :], N_EXPERTS), axis=1)
    nxt = jnp.where(nxt == N_EXPERTS, -1, nxt).astype(jnp.int32)
    tile_next = jnp.sum(jnp.where(tile_expert[:, None] == ar[None, :], nxt[None, :], 0), axis=1)
    return dict(seg=seg.reshape(-1), loc_off=loc_off.reshape(-1), glob_off=glob_off.reshape(-1),
                n_slots=n_slots, tile_expert=tile_expert, tile_next=tile_next,
                n_used=(e_end[-1:] // tm).astype(jnp.int32),
                pad_start=jnp.concatenate([e_start + length, e_end[-1:]]).astype(jnp.int32),
                pad_len=jnp.concatenate([padded - length, n_slots - e_end[-1:]]).astype(jnp.int32))


def _aligned(v):
    return v if isinstance(v, int) else pl.multiple_of(v, SEG_ALIGN)


def _seg_copy(src_ref, dst_ref, sem, src_row, dst_row, n_rows):
    n_rows = _aligned(n_rows)
    return pltpu.make_async_copy(src_ref.at[pl.ds(_aligned(src_row), n_rows)],
                                 dst_ref.at[pl.ds(_aligned(dst_row), n_rows)], sem)


def _segment_copies(step, seg_ref, loc_ref, glob_ref, make):
    out = []
    for e in range(N_EXPERTS):
        k = step * N_EXPERTS + e
        out.append((seg_ref[k], make(loc_ref[k], glob_ref[k], seg_ref[k])))
    return out


def _start_all(copies):
    for n, cp in copies:
        @pl.when(n > 0)
        def _(cp=cp):
            cp.start()


def _wait_all(copies):
    for n, cp in copies:
        @pl.when(n > 0)
        def _(cp=cp):
            cp.wait()


def _dispatch_kernel(seg_ref, loc_ref, glob_ref, pst_ref, pln_ref, x_ref, g_ref, rt_ref,
                     xs_ref, cbuf_ref, zbuf_ref, sems):
    s = pl.program_id(0)
    n_steps = pl.num_programs(0)
    buf = s % 2
    d = x_ref.shape[1]

    def copies(step, b):
        return _segment_copies(
            step, seg_ref, loc_ref, glob_ref,
            lambda loc, glob, n: _seg_copy(cbuf_ref.at[b], xs_ref, sems.at[b], loc, glob, n))

    @pl.when(s >= 2)
    def _():
        _wait_all(copies(s - 2, buf))

    x = x_ref[...]
    ms = jnp.mean(x * x, axis=-1, keepdims=True)
    t = (x * lax.rsqrt(ms + RMS_EPS) * g_ref[...]).astype(BF16)
    rt = rt_ref[...]
    lp = rt[4:6].astype(jnp.int32)
    slot = lax.broadcasted_iota(jnp.int32, (MOE_CBUF, MOE_SUB), 0)
    hit0 = slot == lp[0:1]
    hit1 = slot == lp[1:2]
    onehot = jnp.where(hit0 | hit1, 1.0, 0.0).astype(BF16)
    cbuf_ref[buf, :, 0:d] = jnp.dot(onehot, t, preferred_element_type=F32).astype(BF16)
    wslot = jnp.sum(jnp.where(hit0, rt[2:3], 0.0) + jnp.where(hit1, rt[3:4], 0.0),
                    axis=1, keepdims=True)
    w_hi = wslot.astype(BF16)
    w_lo = (wslot - w_hi.astype(F32)).astype(BF16)
    half = MOE_WCOLS // 2
    cbuf_ref[buf, :, d:d + half] = jnp.broadcast_to(w_hi, (MOE_CBUF, half))
    cbuf_ref[buf, :, d + half:] = jnp.broadcast_to(w_lo, (MOE_CBUF, half))
    _start_all(copies(s, buf))

    @pl.when(s == 0)
    def _():
        zsem = sems.at[2]
        zbuf_ref[...] = jnp.zeros(zbuf_ref.shape, BF16)
        pads = [(pln_ref[e], _seg_copy(zbuf_ref, xs_ref, zsem, 0, pst_ref[e], pln_ref[e]))
                for e in range(N_EXPERTS)]
        _start_all(pads)
        _wait_all(pads)
        tail_tiles = pln_ref[N_EXPERTS] // MOE_TILE

        def tail_copy(k):
            return _seg_copy(zbuf_ref, xs_ref, zsem, 0, pst_ref[N_EXPERTS] + k * MOE_TILE, MOE_TILE)

        def start(k, carry):
            tail_copy(k).start()
            return carry

        def wait(k, carry):
            tail_copy(k).wait()
            return carry
        lax.fori_loop(0, tail_tiles, start, 0)
        lax.fori_loop(0, tail_tiles, wait, 0)

    @pl.when(s == n_steps - 1)
    def _():
        @pl.when(s >= 1)
        def _():
            _wait_all(copies(s - 1, 1 - buf))
        _wait_all(copies(s, buf))


def moe_dispatch(x, g, route, plan):
    m, d = x.shape
    dw = d + MOE_WCOLS
    return pl.pallas_call(
        _dispatch_kernel,
        grid_spec=pltpu.PrefetchScalarGridSpec(
            num_scalar_prefetch=5,
            grid=(m // MOE_SUB,),
            in_specs=[
                pl.BlockSpec((MOE_SUB, d), lambda i, *_: (i, 0)),
                pl.BlockSpec((1, d), lambda i, *_: (0, 0)),
                pl.BlockSpec((8, MOE_SUB), lambda i, *_: (0, i)),
            ],
            out_specs=pl.BlockSpec(memory_space=pl.ANY),
            scratch_shapes=[pltpu.VMEM((2, MOE_CBUF, dw), BF16), pltpu.VMEM((MOE_TILE, dw), BF16),
                            pltpu.SemaphoreType.DMA((3,))],
        ),
        out_shape=jax.ShapeDtypeStruct((plan["n_slots"], dw), BF16),
        compiler_params=pltpu.CompilerParams(dimension_semantics=("arbitrary",),
                                             vmem_limit_bytes=40 * MIB, has_side_effects=True),
        name="moe_dispatch",
    )(plan["seg"], plan["loc_off"], plan["glob_off"], plan["pad_start"], plan["pad_len"],
      x, g.reshape(1, d), route)


def _ffn_kernel(te_ref, nx_ref, nu_ref, x_ref, wg_hbm, wu_hbm, wd_hbm, o_ref,
                wgs_ref, wus_ref, wds_ref, wgb_ref, wub_ref, wdb_ref, sem, *, li):
    i = pl.program_id(0)
    d = o_ref.shape[1]

    def weight_copies(e):
        return [pltpu.make_async_copy(wg_hbm.at[li, e], wgs_ref, sem),
                pltpu.make_async_copy(wu_hbm.at[li, e], wus_ref, sem),
                pltpu.make_async_copy(wd_hbm.at[li, e], wds_ref, sem)]

    @pl.when(i < nu_ref[0])
    def _():
        e = te_ref[i]

        @pl.when(i == 0)
        def _():
            for cp in weight_copies(e):
                cp.start()

        @pl.when((i == 0) | (e != te_ref[jnp.maximum(i - 1, 0)]))
        def _():
            for cp in weight_copies(e):
                cp.wait()
            wgb_ref[...] = wgs_ref[...].astype(BF16)
            wub_ref[...] = wus_ref[...].astype(BF16)
            wdb_ref[...] = wds_ref[...].astype(BF16)

            @pl.when(nx_ref[i] >= 0)
            def _():
                for cp in weight_copies(nx_ref[i]):
                    cp.start()

        x = x_ref[:, 0:d]
        half = MOE_WCOLS // 2
        w = x_ref[:, d:d + 1].astype(F32) + x_ref[:, d + half:d + half + 1].astype(F32)
        gate = jnp.dot(x, wgb_ref[...], preferred_element_type=F32)
        up = jnp.dot(x, wub_ref[...], preferred_element_type=F32)
        hid = (gate * _sigmoid(gate) * up * w).astype(BF16)
        o_ref[...] = jnp.dot(hid, wdb_ref[...], preferred_element_type=F32).astype(o_ref.dtype)

    @pl.when(i >= nu_ref[0])
    def _():
        o_ref[...] = jnp.zeros(o_ref.shape, o_ref.dtype)


def grouped_ffn(xs, plan, w_gate, w_up, w_down, li):
    p, dw = xs.shape
    d = dw - MOE_WCOLS
    f = D_EXPERT
    tm = MOE_TILE
    hbm = pl.BlockSpec(memory_space=pl.ANY)
    return pl.pallas_call(
        functools.partial(_ffn_kernel, li=li),
        grid_spec=pltpu.PrefetchScalarGridSpec(
            num_scalar_prefetch=3,
            grid=(p // tm,),
            in_specs=[
                pl.BlockSpec((tm, dw), lambda i, te, nx, nu: (jnp.minimum(i, nu[0] - 1), 0)),
                hbm, hbm, hbm,
            ],
            out_specs=pl.BlockSpec((tm, d), lambda i, te, nx, nu: (i, 0)),
            scratch_shapes=[pltpu.VMEM((d, f), F32), pltpu.VMEM((d, f), F32), pltpu.VMEM((f, d), F32),
                            pltpu.VMEM((d, f), BF16), pltpu.VMEM((d, f), BF16),
                            pltpu.VMEM((f, d), BF16), pltpu.SemaphoreType.DMA(())],
        ),
        out_shape=jax.ShapeDtypeStruct((p, d), BF16),
        compiler_params=_cp(("arbitrary",), 48),
        name="grouped_ffn",
    )(plan["tile_expert"], plan["tile_next"], plan["n_used"], xs, w_gate, w_up, w_down)


def _combine_kernel(seg_ref, loc_ref, glob_ref, x_ref, lp_ref, ys_ref, g_ref, *rest, final_norm):
    if final_norm:
        o_ref, ybuf_ref, sems = rest
    else:
        o_ref, h_ref, ybuf_ref, sems = rest
    s = pl.program_id(0)
    n_steps = pl.num_programs(0)
    buf = s % 2

    def copies(step, b):
        return _segment_copies(
            step, seg_ref, loc_ref, glob_ref,
            lambda loc, glob, n: _seg_copy(ys_ref, ybuf_ref.at[b], sems.at[b], glob, loc, n))

    def fetch(step, b):
        ybuf_ref[b] = jnp.zeros(ybuf_ref.shape[1:], BF16)
        _start_all(copies(step, b))

    @pl.when(s == 0)
    def _():
        fetch(s, buf)

    @pl.when(s + 1 < n_steps)
    def _():
        fetch(s + 1, 1 - buf)

    _wait_all(copies(s, buf))
    lp = lp_ref[...]
    slot = lax.broadcasted_iota(jnp.int32, (MOE_SUB, MOE_CBUF), 1)
    onehot = jnp.where((slot == lp[:, 0:1]) | (slot == lp[:, 1:2]), 1.0, 0.0).astype(BF16)
    out = x_ref[...] + jnp.dot(onehot, ybuf_ref[buf], preferred_element_type=F32)
    ms = jnp.mean(out * out, axis=-1, keepdims=True)
    normed = out * lax.rsqrt(ms + RMS_EPS) * g_ref[...]
    if final_norm:
        o_ref[...] = normed
    else:
        o_ref[...] = out
        h_ref[...] = normed.astype(BF16)


def moe_combine(x, ys, route, plan, gain, final_norm):
    m, d = x.shape
    row = pl.BlockSpec((MOE_SUB, d), lambda i, *_: (i, 0))
    out_specs, out_shape = row, jax.ShapeDtypeStruct((m, d), F32)
    if not final_norm:
        out_specs, out_shape = [row, row], [out_shape, jax.ShapeDtypeStruct((m, d), BF16)]
    return pl.pallas_call(
        functools.partial(_combine_kernel, final_norm=final_norm),
        grid_spec=pltpu.PrefetchScalarGridSpec(
            num_scalar_prefetch=3,
            grid=(m // MOE_SUB,),
            in_specs=[
                row,
                pl.BlockSpec((MOE_SUB, 2), lambda i, *_: (i, 0)),
                pl.BlockSpec(memory_space=pl.ANY),
                pl.BlockSpec((1, d), lambda i, *_: (0, 0)),
            ],
            out_specs=out_specs,
            scratch_shapes=[pltpu.VMEM((2, MOE_CBUF, d), BF16), pltpu.SemaphoreType.DMA((2,))],
        ),
        out_shape=out_shape,
        compiler_params=pltpu.CompilerParams(dimension_semantics=("arbitrary",),
                                             vmem_limit_bytes=40 * MIB),
        name="moe_combine",
    )(plan["seg"], plan["loc_off"], plan["glob_off"], x, route[4:6].T.astype(jnp.int32), ys,
      gain.reshape(1, d))


def hier_moe_layer(x, norm_g, w_group, b_group, w_expert, b_expert, w_gate, w_up, w_down, li,
                   out_gain, final_norm):
    m, _ = x.shape
    route, seg = moe_router(x, norm_g, w_group, b_group, w_expert, b_expert)
    plan = moe_plan(seg, m)
    xs = moe_dispatch(x, norm_g, route, plan)
    ys = grouped_ffn(xs, plan, w_gate, w_up, w_down, li)
    return moe_combine(x, ys, route, plan, out_gain, final_norm)


def _rope_tables(seq, dim):
    pos = jnp.arange(seq, dtype=F32)
    inv = ROPE_THETA ** (-jnp.arange(0, dim, 2, dtype=F32) / dim)
    ang = pos[:, None] * inv[None, :]
    ang = jnp.concatenate([ang, ang], axis=-1)
    sign = jnp.concatenate([-jnp.ones((dim // 2,), F32), jnp.ones((dim // 2,), F32)])
    return jnp.cos(ang), jnp.sin(ang) * sign[None, :]


def _lambda_init(depth_idx):
    return 0.8 - 0.6 * math.exp(-0.3 * depth_idx)


def kernel(x, mem, norm_mix, norm_mem, norm_ffn, norm_final, w_out, w_mem_kv, pool_w_in, pool_w_grp, pool_scale, diff_w_in, diff_lambda, diff_subln, ssd_w_in, ssd_conv_w, ssd_conv_b, ssd_dt_bias, ssd_a_log, ssd_d, ssd_norm, moe_w_group, moe_b_group, moe_w_expert, moe_b_expert, moe_w_gate, moe_w_up, moe_w_down):
    b, s, d = x.shape
    m = b * s
    mem_len = mem.shape[1]
    cos, sin_signed = _rope_tables(s, DIFF_HEAD_DIM)
    xt = x.reshape(m, d)
    memt = mem.reshape(b * mem_len, d)
    for i in range(DEPTH):
        kind, slot = i % N_MIXERS, i // N_MIXERS
        mem_kv = norm_matmul(memt, norm_mem[i], w_mem_kv, i, 2 * MEM_WIDTH, b * mem_len, 512, F32)
        mem_kv = mem_kv.reshape(b, mem_len, 2 * MEM_WIDTH)
        if i == 0:
            lhs, gain, tn = xt, norm_mix[i], IN_TN
        else:
            lhs, gain, tn = h_next, None, IN_TN_PRENORMED
        if kind == 0:
            n_in = MIX_WIDTH + MEM_WIDTH
            proj = norm_matmul(lhs, gain, pool_w_in, slot, n_in, IN_TM, tn, BF16)
            proj = proj.reshape(b, s, n_in)
            mix = pool_mixer(proj, pool_w_grp, slot, pool_scale[slot])
        elif kind == 1:
            n_in = 3 * MIX_WIDTH + MEM_WIDTH
            proj = norm_matmul(lhs, gain, diff_w_in, slot, n_in, IN_TM, tn, BF16,
                               rope=(cos, sin_signed, MIX_WIDTH, MIX_WIDTH, s))
            proj = proj.reshape(b, s, n_in)
            mix = diff_attention(proj, diff_lambda[slot], diff_subln[slot], _lambda_init(i))
        else:
            n_main = MIX_WIDTH + SSD_CONV_DIM
            w_tail = jnp.concatenate(
                [ssd_w_in[slot, :, SSD_MIX_IN:], ssd_w_in[slot, :, n_main:SSD_MIX_IN],
                 jnp.zeros((d, SSD_TAIL - MEM_WIDTH - SSD_HEADS), F32)], axis=1)
            n_in = n_main + SSD_TAIL
            proj = norm_matmul(lhs, gain, ssd_w_in, slot, n_main, IN_TM, tn, F32, tail=w_tail)
            proj = proj.reshape(b, s, n_in)
            dt_off = n_main + MEM_WIDTH
            dt_raw = proj[:, :, dt_off:dt_off + SSD_HEADS]
            mix = ssd_scan(proj, dt_raw, ssd_conv_w[slot], ssd_conv_b[slot], ssd_dt_bias[slot],
                           ssd_a_log[slot], ssd_d[slot], ssd_norm[slot])
        q_block = (MIX_WIDTH if kind == 0 else 3 * MIX_WIDTH if kind == 1
                   else MIX_WIDTH + SSD_CONV_DIM) // MEM_WIDTH
        mem_out = memory_attention(proj, q_block, mem_kv)
        xt = out_projection(xt, mix.reshape(m, MIX_WIDTH), mem_out.reshape(m, MEM_WIDTH), w_out, i)
        last = i == DEPTH - 1
        res = hier_moe_layer(xt, norm_ffn[i], moe_w_group[i], moe_b_group[i], moe_w_expert[i],
                             moe_b_expert[i], moe_w_gate, moe_w_up, moe_w_down, i,
                             norm_final if last else norm_mix[i + 1], last)
        if last:
            xt = res
        else:
            xt, h_next = res
    return xt.reshape(b, s, d)
```

```python
import functools
import math

import jax
import jax.numpy as jnp
from jax import lax
from jax.experimental import pallas as pl
from jax.experimental.pallas import tpu as pltpu

F32 = jnp.float32
BF16 = jnp.bfloat16

D_MODEL = 2048
DEPTH = 4
N_MIXERS = 3
MIX_WIDTH = 1536
MEM_WIDTH = 512
MEM_HEADS = 4
MEM_HEAD_DIM = 128
POOL_WINDOWS = (2, 4, 8, 16)
POOL_GROUPS = 4
POOL_GROUP_DIM = 384
DIFF_HEAD_DIM = 128
DIFF_HEADS = 6
DIFF_V_DIM = 256
ROPE_THETA = 10000.0
SSD_HEAD_DIM = 64
SSD_HEADS = 24
SSD_GROUPS = 4
SSD_HEADS_PER_GROUP = 6
SSD_STATE = 128
SSD_CONV = 4
SSD_CHUNK = 128
SSD_CONV_DIM = 2560
SSD_MIX_IN = 4120
SSD_TAIL = 1024
N_EXPERT_GROUPS = 4
EXPERTS_PER_GROUP = 4
N_EXPERTS = 16
D_EXPERT = 512
RMS_EPS = 1e-6

IN_TM, IN_TN = 2048, 256
IN_TN_PRENORMED = 512
MOE_TILE = 512
MOE_SUB = 512
SEG_ALIGN = 16
MOE_CBUF = 2 * MOE_SUB + 256
MOE_WCOLS = 256
MIB = 1024 * 1024

NT_DIMS = (((1,), (1,)), ((), ()))
TN_DIMS = (((0,), (0,)), ((), ()))


def _cp(sem, vmem_mib):
    return pltpu.CompilerParams(dimension_semantics=sem, vmem_limit_bytes=vmem_mib * MIB)


def _sigmoid(x):
    return 1.0 / (1.0 + jnp.exp(-x))


def _softplus(x):
    return jnp.maximum(x, 0.0) + jnp.log1p(jnp.exp(-jnp.abs(x)))


def _split3(v):
    hi = v.astype(BF16)
    r = v - hi.astype(F32)
    mid = r.astype(BF16)
    lo = (r - mid.astype(F32)).astype(BF16)
    return hi, mid, lo


def _norm_matmul_kernel(*refs, rope_tiles, main_tiles, prenormed):
    refs = list(refs)
    x_ref = refs.pop(0)
    g_ref = None if prenormed else refs.pop(0)
    w_ref = refs.pop(0)
    if rope_tiles is not None:
        cos_ref, sin_ref = refs.pop(0), refs.pop(0)
    if main_tiles is not None:
        wt_ref = refs.pop(0)
    o_ref = refs.pop(0)
    j = pl.program_id(1)

    if prenormed:
        h_ref = x_ref
    else:
        h_ref = refs.pop(0)

        @pl.when(j == 0)
        def _():
            x = x_ref[...]
            ms = jnp.mean(x * x, axis=-1, keepdims=True)
            h_ref[...] = (x * lax.rsqrt(ms + RMS_EPS) * g_ref[...]).astype(BF16)

    def product(weights_ref):
        return jnp.dot(h_ref[...], weights_ref[...].astype(BF16), preferred_element_type=F32)

    if main_tiles is not None:
        @pl.when(j < main_tiles)
        def _():
            o_ref[...] = product(w_ref).astype(o_ref.dtype)

        @pl.when(j >= main_tiles)
        def _():
            o_ref[...] = product(wt_ref).astype(o_ref.dtype)
        return
    acc = product(w_ref)
    if rope_tiles is None:
        o_ref[...] = acc.astype(o_ref.dtype)
        return
    n_q, n_k = rope_tiles
    hd = DIFF_HEAD_DIM

    @pl.when(j < n_q + n_k)
    def _():
        scale = jnp.where(j < n_q, hd ** -0.5, 1.0)
        cos = cos_ref[...]
        sin = sin_ref[...]
        for c in range(o_ref.shape[1] // hd):
            x = acc[:, c * hd:(c + 1) * hd]
            r = x * cos + pltpu.roll(x, hd // 2, axis=1) * sin
            o_ref[:, c * hd:(c + 1) * hd] = (r * scale).astype(o_ref.dtype)

    @pl.when(j >= n_q + n_k)
    def _():
        o_ref[...] = acc.astype(o_ref.dtype)


def norm_matmul(x, g, w, li, n_cols, tm, tn, out_dtype, rope=None, tail=None):
    m, k = x.shape
    prenormed = g is None
    main_tiles = None if tail is None else n_cols // tn
    w_map = ((lambda i, j: (li, 0, j)) if tail is None
             else (lambda i, j: (li, 0, jnp.minimum(j, main_tiles - 1))))
    in_specs = [pl.BlockSpec((tm, k), lambda i, j: (i, 0))]
    args = [x]
    if not prenormed:
        in_specs.append(pl.BlockSpec((1, k), lambda i, j: (0, 0)))
        args.append(g.reshape(1, k))
    in_specs.append(pl.BlockSpec((None, k, tn), w_map))
    args.append(w)
    rope_tiles = None
    if rope is not None:
        cos, sin_signed, n_q, n_k, seq = rope
        rope_tiles = (n_q // tn, n_k // tn)
        pos_blocks = seq // tm
        tab = pl.BlockSpec((tm, DIFF_HEAD_DIM), lambda i, j: (i % pos_blocks, 0))
        in_specs += [tab, tab]
        args += [cos, sin_signed]
    if tail is not None:
        in_specs.append(pl.BlockSpec((k, tn), lambda i, j: (0, jnp.maximum(j - main_tiles, 0))))
        args.append(tail)
        n_cols = n_cols + tail.shape[1]
    return pl.pallas_call(
        functools.partial(_norm_matmul_kernel, rope_tiles=rope_tiles, main_tiles=main_tiles,
                          prenormed=prenormed),
        grid=(m // tm, n_cols // tn),
        in_specs=in_specs,
        out_specs=pl.BlockSpec((tm, tn), lambda i, j: (i, j)),
        out_shape=jax.ShapeDtypeStruct((m, n_cols), out_dtype),
        scratch_shapes=[] if prenormed else [pltpu.VMEM((tm, k), BF16)],
        compiler_params=_cp(("parallel", "parallel" if prenormed else "arbitrary"), 58),
        name="norm_matmul",
    )(*args)


def _mem_kv_kernel(x_ref, g_ref, w_ref, o_ref):
    x = x_ref[...]
    ms = jnp.mean(x * x, axis=-1, keepdims=True)
    h = (x * lax.rsqrt(ms + RMS_EPS) * g_ref[0]).astype(BF16)
    o_ref[0] = jnp.dot(h, w_ref[0].astype(BF16), preferred_element_type=F32)


def memory_kv(mem2, norm_mem, w_mem_kv):
    rows, k = mem2.shape
    layers, _, n = w_mem_kv.shape
    tn = 512
    return pl.pallas_call(
        _mem_kv_kernel,
        grid=(layers, n // tn),
        in_specs=[
            pl.BlockSpec((rows, k), lambda l, j: (0, 0)),
            pl.BlockSpec((1, 1, k), lambda l, j: (l, 0, 0)),
            pl.BlockSpec((1, k, tn), lambda l, j: (l, 0, j)),
        ],
        out_specs=pl.BlockSpec((1, rows, tn), lambda l, j: (l, 0, j)),
        out_shape=jax.ShapeDtypeStruct((layers, rows, n), F32),
        compiler_params=_cp(("parallel", "parallel"), 48),
        name="memory_kv",
    )(mem2, norm_mem.reshape(layers, 1, k), w_mem_kv)


def _pool_kernel(u_ref, w_ref, sc_ref, o_ref, pad_ref):
    grp = pl.program_id(1)
    s, c = u_ref.shape[1], u_ref.shape[2]
    rows = 256
    pad_ref[0:16, :] = jnp.zeros((16, c), F32)
    pad_ref[16:, :] = u_ref[0].astype(F32)
    wb = w_ref[...].astype(BF16)
    sc = sc_ref[0]

    for gi, win in enumerate(POOL_WINDOWS):
        @pl.when(grp == gi)
        def _(win=win):
            for r in range(s // rows):
                xh = pad_ref[r * rows:r * rows + rows + 16, :]
                acc = xh
                k = 1
                while k < win:
                    acc = acc + pltpu.roll(acc, k, axis=0)
                    k *= 2
                t = r * rows + lax.broadcasted_iota(jnp.int32, (rows, 1), 0)
                cnt = jnp.minimum(t + 1, win).astype(F32)
                mixed = (acc[16:, :] / cnt - xh[16:, :]).astype(BF16)
                o_ref[0, r * rows:(r + 1) * rows, :] = (
                    jnp.dot(mixed, wb, preferred_element_type=F32) * sc).astype(o_ref.dtype)


def pool_mixer(proj3, w_grp, li, scale):
    b, s, _ = proj3.shape
    c = POOL_GROUP_DIM
    return pl.pallas_call(
        _pool_kernel,
        grid=(b, POOL_GROUPS),
        in_specs=[
            pl.BlockSpec((1, s, c), lambda i, g: (i, 0, g)),
            pl.BlockSpec((None, None, c, c), lambda i, g: (li, g, 0, 0)),
            pl.BlockSpec((1, 1, c), lambda i, g: (g, 0, 0)),
        ],
        out_specs=pl.BlockSpec((1, s, c), lambda i, g: (i, 0, g)),
        out_shape=jax.ShapeDtypeStruct((b, s, MIX_WIDTH), BF16),
        scratch_shapes=[pltpu.VMEM((s + 16, c), F32)],
        compiler_params=_cp(("parallel", "parallel"), 40),
        name="pool_mixer",
    )(proj3, w_grp, scale.reshape(POOL_GROUPS, 1, c))


def _diff_attn_kernel(lam_ref, sub_ref, q_ref, k_ref, v_ref, o_ref, *, lambda_init):
    lam = lam_ref[...]
    s1 = jnp.sum(lam[0:1] * lam[1:2], axis=-1, keepdims=True)
    s2 = jnp.sum(lam[2:3] * lam[3:4], axis=-1, keepdims=True)
    lmbda = jnp.exp(s1) - jnp.exp(s2) + lambda_init
    s = q_ref.shape[1]
    tq = 512
    d = DIFF_HEAD_DIM
    for i in range(s // tq):
        kv = (i + 1) * tq
        q = q_ref[0, i * tq:(i + 1) * tq, :]
        row = i * tq + lax.broadcasted_iota(jnp.int32, (tq, kv), 0)
        col = lax.broadcasted_iota(jnp.int32, (tq, kv), 1)
        mask = col <= row
        outs = []
        for c in range(2):
            sc = lax.dot_general(q[:, c * d:(c + 1) * d], k_ref[0, 0:kv, c * d:(c + 1) * d],
                                 NT_DIMS, preferred_element_type=F32)
            sc = jnp.where(mask, sc, -jnp.inf)
            e = jnp.exp(sc - jnp.max(sc, axis=-1, keepdims=True))
            pv = jnp.dot(e.astype(BF16), v_ref[0, 0:kv, :], preferred_element_type=F32)
            outs.append(pv / jnp.sum(e, axis=-1, keepdims=True))
        o = outs[0] - lmbda * outs[1]
        ms = jnp.mean(o * o, axis=-1, keepdims=True)
        o_ref[0, i * tq:(i + 1) * tq, :] = (
            o * lax.rsqrt(ms + RMS_EPS) * sub_ref[...] * (1.0 - lambda_init)).astype(o_ref.dtype)


def diff_attention(proj3, lam, subln, lambda_init):
    b, s, _ = proj3.shape
    vd = DIFF_V_DIM
    nh = DIFF_HEADS
    return pl.pallas_call(
        functools.partial(_diff_attn_kernel, lambda_init=lambda_init),
        grid=(b, nh),
        in_specs=[
            pl.BlockSpec((4, DIFF_HEAD_DIM), lambda i, h: (0, 0)),
            pl.BlockSpec((1, vd), lambda i, h: (0, 0)),
            pl.BlockSpec((1, s, vd), lambda i, h: (i, 0, h)),
            pl.BlockSpec((1, s, vd), lambda i, h: (i, 0, nh + h)),
            pl.BlockSpec((1, s, vd), lambda i, h: (i, 0, 2 * nh + h)),
        ],
        out_specs=pl.BlockSpec((1, s, vd), lambda i, h: (i, 0, h)),
        out_shape=jax.ShapeDtypeStruct((b, s, MIX_WIDTH), BF16),
        compiler_params=_cp(("parallel", "parallel"), 48),
        name="diff_attention",
    )(lam, subln.reshape(1, vd), proj3, proj3, proj3)


CONV_HALO = 8


def _causal_conv_silu(u_ref, halo_ref, w_ref, b_ref):
    cur = u_ref[0]
    ext = jnp.concatenate([halo_ref[...], cur], axis=0)
    w = w_ref[...]
    y = ext * w[SSD_CONV - 1:SSD_CONV]
    for j in range(1, SSD_CONV):
        y = y + pltpu.roll(ext, j, axis=0) * w[SSD_CONV - 1 - j:SSD_CONV - j]
    halo_ref[...] = cur[cur.shape[0] - CONV_HALO:, :]
    y = y[CONV_HALO:, :] + b_ref[...]
    return y * _sigmoid(y)


def _dot3(a_f32, b_bf16):
    return sum(jnp.dot(t, b_bf16, preferred_element_type=F32) for t in _split3(a_f32))


def _ssd_kernel(x_ref, b_ref, c_ref, z_ref, dt_ref, wx_ref, wb_ref, wc_ref, cbx_ref, cbb_ref, cbc_ref,
                bias_ref, alog_ref, dx_ref, ng_ref, o_ref,
                state_ref, y_ref, xs_ref, hx_ref, hb_ref, hc_ref):
    @pl.when(pl.program_id(2) == 0)
    def _():
        state_ref[...] = jnp.zeros(state_ref.shape, F32)
        hx_ref[...] = jnp.zeros(hx_ref.shape, F32)
        hb_ref[...] = jnp.zeros(hb_ref.shape, F32)
        hc_ref[...] = jnp.zeros(hc_ref.shape, F32)

    ln = SSD_CHUNK
    hg = SSD_HEADS_PER_GROUP
    lanes = 2 * SSD_HEAD_DIM
    xs_ref[...] = _causal_conv_silu(x_ref, hx_ref, wx_ref, cbx_ref)
    bm = _causal_conv_silu(b_ref, hb_ref, wb_ref, cbb_ref)
    bm_t = bm.T.astype(BF16)
    cm = _causal_conv_silu(c_ref, hc_ref, wc_ref, cbc_ref).astype(BF16)
    row = lax.broadcasted_iota(jnp.int32, (ln, ln), 0)
    col = lax.broadcasted_iota(jnp.int32, (ln, ln), 1)
    causal = col <= row
    ones_lower = jnp.where(causal, 1.0, 0.0).astype(BF16)

    dt = _softplus(dt_ref[0] + bias_ref[0])
    acs = sum(jnp.dot(ones_lower, t, preferred_element_type=F32)
              for t in _split3(dt * (-jnp.exp(alog_ref[0]))))
    acs_rows = acs.T
    sel_r = lax.broadcasted_iota(jnp.int32, (lanes, hg * lanes), 0)
    sel_c = lax.broadcasted_iota(jnp.int32, (lanes, hg * lanes), 1)
    spread = jnp.where(jnp.right_shift(sel_c, lanes.bit_length() - 1) == sel_r,
                       1.0, 0.0).astype(BF16)
    dt_all = _dot3(dt, spread)
    acs_all = _dot3(acs, spread)

    scores = lax.dot_general(cm, bm.astype(BF16), NT_DIMS, preferred_element_type=F32)
    low_half = lax.broadcasted_iota(jnp.int32, (ln, lanes), 1) < SSD_HEAD_DIM
    for k in range(hg // 2):
        heads = (2 * k, 2 * k + 1)
        tile = lambda a, h: a[:, h * lanes:(h + 1) * lanes]
        x = xs_ref[:, k * lanes:(k + 1) * lanes]
        dt_p = jnp.where(low_half, tile(dt_all, heads[0]), tile(dt_all, heads[1]))
        acs_p = jnp.where(low_half, tile(acs_all, heads[0]), tile(acs_all, heads[1]))
        xc = x * dt_p
        prev = state_ref[k]
        y = jnp.dot(cm, prev.astype(BF16), preferred_element_type=F32) * jnp.exp(acs_p)
        for h, own in zip(heads, (low_half, ~low_half)):
            decay = jnp.exp(jnp.where(causal, tile(acs_all, h) - acs_rows[h:h + 1, :], -jnp.inf))
            y = y + jnp.dot((scores * decay).astype(BF16), jnp.where(own, xc, 0.0).astype(BF16),
                            preferred_element_type=F32)
        a_last = acs_p[ln - 1:ln, :]
        st = jnp.dot(bm_t, (xc * jnp.exp(a_last - acs_p)).astype(BF16),
                     preferred_element_type=F32)
        state_ref[k] = prev * jnp.exp(a_last) + st
        y_ref[:, k * lanes:(k + 1) * lanes] = y

    z = z_ref[0]
    yz = (y_ref[...] + dx_ref[0] * xs_ref[...]) * (z * _sigmoid(z))
    ms = jnp.mean(yz * yz, axis=-1, keepdims=True)
    o_ref[0] = (yz * lax.rsqrt(ms + RMS_EPS) * ng_ref[0]).astype(o_ref.dtype)


def ssd_scan(proj3, dt_blk, conv_w, conv_b, dt_bias, a_log, d_skip, norm_g):
    b, s, _ = proj3.shape
    ln, g, hg, n = SSD_CHUNK, SSD_GROUPS, SSD_HEADS_PER_GROUP, SSD_STATE
    gw = hg * SSD_HEAD_DIM
    lanes = 2 * SSD_HEAD_DIM
    assert ln == lanes and n == lanes

    def head_lanes(v):
        v = v.reshape(v.shape[:-1] + (g, hg))
        v = jnp.pad(v, [(0, 0)] * (v.ndim - 1) + [(0, lanes - hg)])
        return v.reshape(v.shape[:-2] + (g * lanes,))

    per_group = pl.BlockSpec((1, 1, lanes), lambda i, j, c: (j, 0, 0))
    d_chan = jnp.repeat(d_skip, SSD_HEAD_DIM)
    x_blk, b_blk = MIX_WIDTH // gw, (2 * MIX_WIDTH) // n
    c_blk = b_blk + g
    cw_b, cw_c = MIX_WIDTH // n, MIX_WIDTH // n + g
    conv_b2 = conv_b.reshape(1, SSD_CONV_DIM)
    return pl.pallas_call(
        _ssd_kernel,
        grid=(b, g, s // ln),
        in_specs=[
            pl.BlockSpec((1, ln, gw), lambda i, j, c: (i, c, x_blk + j)),
            pl.BlockSpec((1, ln, n), lambda i, j, c: (i, c, b_blk + j)),
            pl.BlockSpec((1, ln, n), lambda i, j, c: (i, c, c_blk + j)),
            pl.BlockSpec((1, ln, gw), lambda i, j, c: (i, c, j)),
            pl.BlockSpec((1, ln, lanes), lambda i, j, c: (i, c, dt_blk + j)),
            pl.BlockSpec((SSD_CONV, gw), lambda i, j, c: (0, j)),
            pl.BlockSpec((SSD_CONV, n), lambda i, j, c: (0, cw_b + j)),
            pl.BlockSpec((SSD_CONV, n), lambda i, j, c: (0, cw_c + j)),
            pl.BlockSpec((1, gw), lambda i, j, c: (0, j)),
            pl.BlockSpec((1, n), lambda i, j, c: (0, cw_b + j)),
            pl.BlockSpec((1, n), lambda i, j, c: (0, cw_c + j)),
            per_group, per_group,
            pl.BlockSpec((1, 1, gw), lambda i, j, c: (j, 0, 0)),
            pl.BlockSpec((1, 1, gw), lambda i, j, c: (j, 0, 0)),
        ],
        out_specs=pl.BlockSpec((1, ln, gw), lambda i, j, c: (i, c, j)),
        out_shape=jax.ShapeDtypeStruct((b, s, MIX_WIDTH), BF16),
        scratch_shapes=[pltpu.VMEM((hg // 2, n, lanes), F32), pltpu.VMEM((ln, gw), F32),
                        pltpu.VMEM((ln, gw), F32), pltpu.VMEM((CONV_HALO, gw), F32),
                        pltpu.VMEM((CONV_HALO, n), F32), pltpu.VMEM((CONV_HALO, n), F32)],
        compiler_params=_cp(("parallel", "parallel", "arbitrary"), 32),
        name="ssd_scan",
    )(proj3, proj3, proj3, proj3, proj3, conv_w, conv_w, conv_w,
      conv_b2, conv_b2, conv_b2, head_lanes(dt_bias).reshape(g, 1, lanes),
      head_lanes(a_log).reshape(g, 1, lanes), d_chan.reshape(g, 1, gw), norm_g.reshape(g, 1, gw))


def _mem_attn_kernel(q_ref, kv_ref, o_ref):
    d = MEM_HEAD_DIM
    for h in range(MEM_HEADS):
        q = (q_ref[0, :, h * d:(h + 1) * d].astype(F32) * (d ** -0.5)).astype(BF16)
        k = kv_ref[0, :, h * d:(h + 1) * d].astype(BF16)
        v = kv_ref[0, :, MEM_WIDTH + h * d:MEM_WIDTH + (h + 1) * d].astype(BF16)
        sc = lax.dot_general(q, k, NT_DIMS, preferred_element_type=F32)
        e = jnp.exp(sc - jnp.max(sc, axis=-1, keepdims=True))
        pr = (e / jnp.sum(e, axis=-1, keepdims=True)).astype(BF16)
        o_ref[0, :, h * d:(h + 1) * d] = jnp.dot(
            pr, v, preferred_element_type=F32).astype(o_ref.dtype)


def memory_attention(proj3, q_block, mem_kv, li):
    b, s, _ = proj3.shape
    m = mem_kv.shape[1]
    tq = 512
    return pl.pallas_call(
        _mem_attn_kernel,
        grid=(b, s // tq),
        in_specs=[
            pl.BlockSpec((1, tq, MEM_WIDTH), lambda i, j: (i, j, q_block)),
            pl.BlockSpec((1, m, 2 * MEM_WIDTH), lambda i, j: (li * b + i, 0, 0)),
        ],
        out_specs=pl.BlockSpec((1, tq, MEM_WIDTH), lambda i, j: (i, j, 0)),
        out_shape=jax.ShapeDtypeStruct((b, s, MEM_WIDTH), BF16),
        compiler_params=_cp(("parallel", "parallel"), 32),
        name="memory_attention",
    )(proj3, mem_kv)


def _outproj_kernel(x_ref, a_ref, m_ref, wa_ref, wm_ref, o_ref):
    acc = jnp.dot(a_ref[...], wa_ref[...].astype(BF16), preferred_element_type=F32)
    acc = acc + jnp.dot(m_ref[...], wm_ref[...].astype(BF16), preferred_element_type=F32)
    o_ref[...] = x_ref[...] + acc


def out_projection(x, mix, mem_out, w_out, li):
    m, d = x.shape
    tm, tn = 2048, 512
    return pl.pallas_call(
        _outproj_kernel,
        grid=(m // tm, d // tn),
        in_specs=[
            pl.BlockSpec((tm, tn), lambda i, j: (i, j)),
            pl.BlockSpec((tm, MIX_WIDTH), lambda i, j: (i, 0)),
            pl.BlockSpec((tm, MEM_WIDTH), lambda i, j: (i, 0)),
            pl.BlockSpec((None, MIX_WIDTH, tn), lambda i, j: (li, 0, j)),
            pl.BlockSpec((None, MEM_WIDTH, tn), lambda i, j: (li, MIX_WIDTH // MEM_WIDTH, j)),
        ],
        out_specs=pl.BlockSpec((tm, tn), lambda i, j: (i, j)),
        out_shape=jax.ShapeDtypeStruct((m, d), F32),
        compiler_params=_cp(("parallel", "parallel"), 48),
        name="out_projection",
    )(x, mix, mem_out, w_out, w_out)


def _router_kernel(x_ref, g_ref, wr_ref, br_ref, r_ref, seg_ref, t_ref):
    x = x_ref[...]
    ms = jnp.mean(x * x, axis=-1, keepdims=True)
    t = x * lax.rsqrt(ms + RMS_EPS) * g_ref[...]
    th = t.astype(BF16)
    t_ref[...] = th
    tl = (t - th.astype(F32)).astype(BF16)
    w = wr_ref[...]
    wh = w.astype(BF16)
    wl = (w - wh.astype(F32)).astype(BF16)
    lg = (lax.dot_general(wh, th, NT_DIMS, preferred_element_type=F32)
          + lax.dot_general(wh, tl, NT_DIMS, preferred_element_type=F32)
          + lax.dot_general(wl, th, NT_DIMS, preferred_element_type=F32)) + br_ref[...]
    ng, epg = N_EXPERT_GROUPS, EXPERTS_PER_GROUP
    gl = [lg[j:j + 1] for j in range(ng)]
    el = [lg[ng + j:ng + j + 1] for j in range(N_EXPERTS)]

    def first_argmax(vals):
        top = functools.reduce(jnp.maximum, vals)
        idx = jnp.full(top.shape, len(vals) - 1, jnp.int32)
        for j in range(len(vals) - 2, -1, -1):
            idx = jnp.where(vals[j] >= top, j, idx)
        return top, idx

    gmax, gsel = first_argmax(gl)
    g_w = 1.0 / functools.reduce(lambda a, b: a + b, [jnp.exp(v - gmax) for v in gl])
    e_in = []
    for j in range(epg):
        v = el[(ng - 1) * epg + j]
        for gi in range(ng - 2, -1, -1):
            v = jnp.where(gsel == gi, el[gi * epg + j], v)
        e_in.append(v)
    emax = functools.reduce(jnp.maximum, e_in)
    pe = [jnp.exp(v - emax) for v in e_in]
    se = functools.reduce(lambda a, b: a + b, pe)
    prob = [v / se for v in pe]
    v1, i1 = first_argmax(prob)
    rest = [jnp.where(i1 == j, -1.0, prob[j]) for j in range(epg)]
    v2, i2 = first_argmax(rest)
    tot = v1 + v2
    e1 = gsel * epg + i1
    e2 = gsel * epg + i2

    tm = x.shape[0]
    eidx = lax.broadcasted_iota(jnp.int32, (N_EXPERTS, tm), 0)
    hit1 = eidx == e1
    hit2 = eidx == e2
    onehot = jnp.where(hit1 | hit2, 1.0, 0.0)
    tok_r = lax.broadcasted_iota(jnp.int32, (tm, tm), 0)
    tok_c = lax.broadcasted_iota(jnp.int32, (tm, tm), 1)
    earlier = jnp.where(tok_r < tok_c, 1.0, 0.0).astype(BF16)
    rank = jnp.dot(onehot.astype(BF16), earlier, preferred_element_type=F32)
    cnt = jnp.sum(onehot, axis=1, keepdims=True).astype(jnp.int32)
    seg = jnp.bitwise_and(cnt + (SEG_ALIGN - 1), -SEG_ALIGN)
    ex_r = lax.broadcasted_iota(jnp.int32, (N_EXPERTS, N_EXPERTS), 0)
    ex_c = lax.broadcasted_iota(jnp.int32, (N_EXPERTS, N_EXPERTS), 1)
    lower = jnp.where(ex_c < ex_r, 1.0, 0.0).astype(BF16)
    seg_lanes = jnp.broadcast_to(seg.astype(F32), (N_EXPERTS, 128))
    start = jnp.dot(lower, seg_lanes.astype(BF16), preferred_element_type=F32)[:, 0:1]
    slot = start + rank
    lpos1 = jnp.sum(jnp.where(hit1, slot, 0.0), axis=0, keepdims=True)
    lpos2 = jnp.sum(jnp.where(hit2, slot, 0.0), axis=0, keepdims=True)
    r_ref[...] = jnp.concatenate(
        [e1.astype(F32), e2.astype(F32), v1 / tot * g_w, v2 / tot * g_w, lpos1, lpos2,
         jnp.zeros((2, tm), F32)], axis=0)
    seg_ref[0] = jnp.broadcast_to(seg, (N_EXPERTS, 128))


def moe_router(x, g, w_group, b_group, w_expert, b_expert):
    m, d = x.shape
    tm = MOE_SUB
    nr = 32
    wr = jnp.zeros((nr, d), F32).at[:N_EXPERT_GROUPS].set(w_group.T)
    wr = wr.at[N_EXPERT_GROUPS:N_EXPERT_GROUPS + N_EXPERTS].set(w_expert.T)
    br = jnp.zeros((nr, 1), F32).at[:N_EXPERT_GROUPS, 0].set(b_group)
    br = br.at[N_EXPERT_GROUPS:N_EXPERT_GROUPS + N_EXPERTS, 0].set(b_expert)
    route, seg3, t = pl.pallas_call(
        _router_kernel,
        grid=(m // tm,),
        in_specs=[
            pl.BlockSpec((tm, d), lambda i: (i, 0)),
            pl.BlockSpec((1, d), lambda i: (0, 0)),
            pl.BlockSpec((nr, d), lambda i: (0, 0)),
            pl.BlockSpec((nr, 1), lambda i: (0, 0)),
        ],
        out_specs=[pl.BlockSpec((8, tm), lambda i: (0, i)),
                   pl.BlockSpec((1, N_EXPERTS, 128), lambda i: (i, 0, 0)),
                   pl.BlockSpec((tm, d), lambda i: (i, 0))],
        out_shape=[jax.ShapeDtypeStruct((8, m), F32),
                   jax.ShapeDtypeStruct((m // tm, N_EXPERTS, 128), jnp.int32),
                   jax.ShapeDtypeStruct((m, d), BF16)],
        compiler_params=_cp(("parallel",), 40),
        name="moe_router",
    )(x, g.reshape(1, d), wr, br)
    return route, seg3[:, :, 0], t


def moe_plan(seg, m):
    nsub = m // MOE_SUB
    tm = MOE_TILE
    loc_off = jnp.cumsum(seg, axis=1) - seg
    reg_off = jnp.cumsum(seg, axis=0) - seg
    length = jnp.sum(seg, axis=0)
    padded = (length + tm - 1) // tm * tm
    e_end = jnp.cumsum(padded)
    e_start = e_end - padded
    glob_off = e_start[None, :] + reg_off
    n_slots = (2 * m + nsub * N_EXPERTS * (SEG_ALIGN - 1) + N_EXPERTS * (tm - 1) + tm - 1) // tm * tm
    tile_start = jnp.arange(n_slots // tm, dtype=jnp.int32) * tm
    tile_expert = jnp.minimum(
        jnp.sum((tile_start[:, None] >= e_end[None, :]).astype(jnp.int32), axis=1), N_EXPERTS - 1)
    ar = jnp.arange(N_EXPERTS, dtype=jnp.int32)
    later = (ar[None, :] > ar[:, None]) & (padded[None, :] > 0)
    nxt = jnp.min(jnp.where(later, ar[None, :], N_EXPERTS), axis=1)
    nxt = jnp.where(nxt == N_EXPERTS, -1, nxt).astype(jnp.int32)
    tile_next = jnp.sum(jnp.where(tile_expert[:, None] == ar[None, :], nxt[None, :], 0), axis=1)
    return dict(seg=seg.reshape(-1), loc_off=loc_off.reshape(-1), glob_off=glob_off.reshape(-1),
                n_slots=n_slots, tile_expert=tile_expert, tile_next=tile_next,
                n_used=(e_end[-1:] // tm).astype(jnp.int32),
                pad_start=jnp.concatenate([e_start + length, e_end[-1:]]).astype(jnp.int32),
                pad_len=jnp.concatenate([padded - length, n_slots - e_end[-1:]]).astype(jnp.int32))


def _aligned(v):
    return v if isinstance(v, int) else pl.multiple_of(v, SEG_ALIGN)


def _seg_copy(src_ref, dst_ref, sem, src_row, dst_row, n_rows):
    n_rows = _aligned(n_rows)
    return pltpu.make_async_copy(src_ref.at[pl.ds(_aligned(src_row), n_rows)],
                                 dst_ref.at[pl.ds(_aligned(dst_row), n_rows)], sem)


def _segment_copies(step, seg_ref, loc_ref, glob_ref, make):
    out = []
    for e in range(N_EXPERTS):
        k = step * N_EXPERTS + e
        out.append((seg_ref[k], make(loc_ref[k], glob_ref[k], seg_ref[k])))
    return out


def _start_all(copies):
    for n, cp in copies:
        @pl.when(n > 0)
        def _(cp=cp):
            cp.start()


def _wait_all(copies):
    for n, cp in copies:
        @pl.when(n > 0)
        def _(cp=cp):
            cp.wait()


def _dispatch_kernel(seg_ref, loc_ref, glob_ref, pst_ref, pln_ref, t_ref, rt_ref,
                     xs_ref, cbuf_ref, zbuf_ref, sems):
    s = pl.program_id(0)
    n_steps = pl.num_programs(0)
    buf = s % 2
    d = t_ref.shape[1]

    def copies(step, b):
        return _segment_copies(
            step, seg_ref, loc_ref, glob_ref,
            lambda loc, glob, n: _seg_copy(cbuf_ref.at[b], xs_ref, sems.at[b], loc, glob, n))

    @pl.when(s >= 2)
    def _():
        _wait_all(copies(s - 2, buf))

    t = t_ref[...]
    rt = rt_ref[...]
    lp = rt[4:6].astype(jnp.int32)
    slot = lax.broadcasted_iota(jnp.int32, (MOE_CBUF, MOE_SUB), 0)
    hit0 = slot == lp[0:1]
    hit1 = slot == lp[1:2]
    onehot = jnp.where(hit0 | hit1, 1.0, 0.0).astype(BF16)
    cbuf_ref[buf, :, 0:d] = jnp.dot(onehot, t, preferred_element_type=F32).astype(BF16)
    wslot = jnp.sum(jnp.where(hit0, rt[2:3], 0.0) + jnp.where(hit1, rt[3:4], 0.0),
                    axis=1, keepdims=True)
    w_hi = wslot.astype(BF16)
    w_lo = (wslot - w_hi.astype(F32)).astype(BF16)
    half = MOE_WCOLS // 2
    cbuf_ref[buf, :, d:d + half] = jnp.broadcast_to(w_hi, (MOE_CBUF, half))
    cbuf_ref[buf, :, d + half:] = jnp.broadcast_to(w_lo, (MOE_CBUF, half))
    _start_all(copies(s, buf))

    @pl.when(s == 0)
    def _():
        zsem = sems.at[2]
        zbuf_ref[...] = jnp.zeros(zbuf_ref.shape, BF16)
        pads = [(pln_ref[e], _seg_copy(zbuf_ref, xs_ref, zsem, 0, pst_ref[e], pln_ref[e]))
                for e in range(N_EXPERTS)]
        _start_all(pads)
        _wait_all(pads)
        tail_tiles = pln_ref[N_EXPERTS] // MOE_TILE

        def tail_copy(k):
            return _seg_copy(zbuf_ref, xs_ref, zsem, 0, pst_ref[N_EXPERTS] + k * MOE_TILE, MOE_TILE)

        def start(k, carry):
            tail_copy(k).start()
            return carry

        def wait(k, carry):
            tail_copy(k).wait()
            return carry
        lax.fori_loop(0, tail_tiles, start, 0)
        lax.fori_loop(0, tail_tiles, wait, 0)

    @pl.when(s == n_steps - 1)
    def _():
        @pl.when(s >= 1)
        def _():
            _wait_all(copies(s - 1, 1 - buf))
        _wait_all(copies(s, buf))


def moe_dispatch(t, route, plan):
    m, d = t.shape
    dw = d + MOE_WCOLS
    return pl.pallas_call(
        _dispatch_kernel,
        grid_spec=pltpu.PrefetchScalarGridSpec(
            num_scalar_prefetch=5,
            grid=(m // MOE_SUB,),
            in_specs=[
                pl.BlockSpec((MOE_SUB, d), lambda i, *_: (i, 0)),
                pl.BlockSpec((8, MOE_SUB), lambda i, *_: (0, i)),
            ],
            out_specs=pl.BlockSpec(memory_space=pl.ANY),
            scratch_shapes=[pltpu.VMEM((2, MOE_CBUF, dw), BF16), pltpu.VMEM((MOE_TILE, dw), BF16),
                            pltpu.SemaphoreType.DMA((3,))],
        ),
        out_shape=jax.ShapeDtypeStruct((plan["n_slots"], dw), BF16),
        compiler_params=pltpu.CompilerParams(dimension_semantics=("arbitrary",),
                                             vmem_limit_bytes=40 * MIB, has_side_effects=True),
        name="moe_dispatch",
    )(plan["seg"], plan["loc_off"], plan["glob_off"], plan["pad_start"], plan["pad_len"],
      t, route)


def _ffn_kernel(te_ref, nx_ref, nu_ref, x_ref, wg_hbm, wu_hbm, wd_hbm, o_ref,
                wgs_ref, wus_ref, wds_ref, wgb_ref, wub_ref, wdb_ref, sem, *, li):
    i = pl.program_id(0)
    d = o_ref.shape[1]

    def weight_copies(e):
        return [pltpu.make_async_copy(wg_hbm.at[li, e], wgs_ref, sem),
                pltpu.make_async_copy(wu_hbm.at[li, e], wus_ref, sem),
                pltpu.make_async_copy(wd_hbm.at[li, e], wds_ref, sem)]

    @pl.when(i < nu_ref[0])
    def _():
        e = te_ref[i]

        @pl.when(i == 0)
        def _():
            for cp in weight_copies(e):
                cp.start()

        @pl.when((i == 0) | (e != te_ref[jnp.maximum(i - 1, 0)]))
        def _():
            for cp in weight_copies(e):
                cp.wait()
            wgb_ref[...] = wgs_ref[...].astype(BF16)
            wub_ref[...] = wus_ref[...].astype(BF16)
            wdb_ref[...] = wds_ref[...].astype(BF16)

            @pl.when(nx_ref[i] >= 0)
            def _():
                for cp in weight_copies(nx_ref[i]):
                    cp.start()

        x = x_ref[:, 0:d]
        half = MOE_WCOLS // 2
        w = x_ref[:, d:d + 1].astype(F32) + x_ref[:, d + half:d + half + 1].astype(F32)
        gate = jnp.dot(x, wgb_ref[...], preferred_element_type=F32)
        up = jnp.dot(x, wub_ref[...], preferred_element_type=F32)
        hid = (gate * _sigmoid(gate) * up * w).astype(BF16)
        o_ref[...] = jnp.dot(hid, wdb_ref[...], preferred_element_type=F32).astype(o_ref.dtype)

    @pl.when(i >= nu_ref[0])
    def _():
        o_ref[...] = jnp.zeros(o_ref.shape, o_ref.dtype)


def grouped_ffn(xs, plan, w_gate, w_up, w_down, li):
    p, dw = xs.shape
    d = dw - MOE_WCOLS
    f = D_EXPERT
    tm = MOE_TILE
    hbm = pl.BlockSpec(memory_space=pl.ANY)
    return pl.pallas_call(
        functools.partial(_ffn_kernel, li=li),
        grid_spec=pltpu.PrefetchScalarGridSpec(
            num_scalar_prefetch=3,
            grid=(p // tm,),
            in_specs=[
                pl.BlockSpec((tm, dw), lambda i, te, nx, nu: (jnp.minimum(i, nu[0] - 1), 0)),
                hbm, hbm, hbm,
            ],
            out_specs=pl.BlockSpec((tm, d), lambda i, te, nx, nu: (i, 0)),
            scratch_shapes=[pltpu.VMEM((d, f), F32), pltpu.VMEM((d, f), F32), pltpu.VMEM((f, d), F32),
                            pltpu.VMEM((d, f), BF16), pltpu.VMEM((d, f), BF16),
                            pltpu.VMEM((f, d), BF16), pltpu.SemaphoreType.DMA(())],
        ),
        out_shape=jax.ShapeDtypeStruct((p, d), BF16),
        compiler_params=_cp(("arbitrary",), 48),
        name="grouped_ffn",
    )(plan["tile_expert"], plan["tile_next"], plan["n_used"], xs, w_gate, w_up, w_down)


def _combine_kernel(seg_ref, loc_ref, glob_ref, x_ref, lp_ref, ys_ref, g_ref, *rest, final_norm):
    if final_norm:
        o_ref, ybuf_ref, sems = rest
    else:
        o_ref, h_ref, ybuf_ref, sems = rest
    s = pl.program_id(0)
    n_steps = pl.num_programs(0)
    buf = s % 2

    def copies(step, b):
        return _segment_copies(
            step, seg_ref, loc_ref, glob_ref,
            lambda loc, glob, n: _seg_copy(ys_ref, ybuf_ref.at[b], sems.at[b], glob, loc, n))

    def fetch(step, b):
        ybuf_ref[b] = jnp.zeros(ybuf_ref.shape[1:], BF16)
        _start_all(copies(step, b))

    @pl.when(s == 0)
    def _():
        fetch(s, buf)

    @pl.when(s + 1 < n_steps)
    def _():
        fetch(s + 1, 1 - buf)

    _wait_all(copies(s, buf))
    lp = lp_ref[...]
    slot = lax.broadcasted_iota(jnp.int32, (MOE_SUB, MOE_CBUF), 1)
    onehot = jnp.where((slot == lp[:, 0:1]) | (slot == lp[:, 1:2]), 1.0, 0.0).astype(BF16)
    out = x_ref[...] + jnp.dot(onehot, ybuf_ref[buf], preferred_element_type=F32)
    ms = jnp.mean(out * out, axis=-1, keepdims=True)
    normed = out * lax.rsqrt(ms + RMS_EPS) * g_ref[...]
    if final_norm:
        o_ref[...] = normed
    else:
        o_ref[...] = out
        h_ref[...] = normed.astype(BF16)


def moe_combine(x, ys, route, plan, gain, final_norm):
    m, d = x.shape
    row = pl.BlockSpec((MOE_SUB, d), lambda i, *_: (i, 0))
    out_specs, out_shape = row, jax.ShapeDtypeStruct((m, d), F32)
    if not final_norm:
        out_specs, out_shape = [row, row], [out_shape, jax.ShapeDtypeStruct((m, d), BF16)]
    return pl.pallas_call(
        functools.partial(_combine_kernel, final_norm=final_norm),
        grid_spec=pltpu.PrefetchScalarGridSpec(
            num_scalar_prefetch=3,
            grid=(m // MOE_SUB,),
            in_specs=[
                row,
                pl.BlockSpec((MOE_SUB, 2), lambda i, *_: (i, 0)),
                pl.BlockSpec(memory_space=pl.ANY),
                pl.BlockSpec((1, d), lambda i, *_: (0, 0)),
            ],
            out_specs=out_specs,
            scratch_shapes=[pltpu.VMEM((2, MOE_CBUF, d), BF16), pltpu.SemaphoreType.DMA((2,))],
        ),
        out_shape=out_shape,
        compiler_params=pltpu.CompilerParams(dimension_semantics=("arbitrary",),
                                             vmem_limit_bytes=40 * MIB),
        name="moe_combine",
    )(plan["seg"], plan["loc_off"], plan["glob_off"], x, route[4:6].T.astype(jnp.int32), ys,
      gain.reshape(1, d))


def hier_moe_layer(x, norm_g, w_group, b_group, w_expert, b_expert, w_gate, w_up, w_down, li,
                   out_gain, final_norm):
    m, _ = x.shape
    route, seg, t = moe_router(x, norm_g, w_group, b_group, w_expert, b_expert)
    plan = moe_plan(seg, m)
    xs = moe_dispatch(t, route, plan)
    ys = grouped_ffn(xs, plan, w_gate, w_up, w_down, li)
    return moe_combine(x, ys, route, plan, out_gain, final_norm)


def _rope_tables(seq, dim):
    pos = jnp.arange(seq, dtype=F32)
    inv = ROPE_THETA ** (-jnp.arange(0, dim, 2, dtype=F32) / dim)
    ang = pos[:, None] * inv[None, :]
    ang = jnp.concatenate([ang, ang], axis=-1)
    sign = jnp.concatenate([-jnp.ones((dim // 2,), F32), jnp.ones((dim // 2,), F32)])
    return jnp.cos(ang), jnp.sin(ang) * sign[None, :]


def _lambda_init(depth_idx):
    return 0.8 - 0.6 * math.exp(-0.3 * depth_idx)


def kernel(x, mem, norm_mix, norm_mem, norm_ffn, norm_final, w_out, w_mem_kv, pool_w_in, pool_w_grp, pool_scale, diff_w_in, diff_lambda, diff_subln, ssd_w_in, ssd_conv_w, ssd_conv_b, ssd_dt_bias, ssd_a_log, ssd_d, ssd_norm, moe_w_group, moe_b_group, moe_w_expert, moe_b_expert, moe_w_gate, moe_w_up, moe_w_down):
    b, s, d = x.shape
    m = b * s
    mem_len = mem.shape[1]
    cos, sin_signed = _rope_tables(s, DIFF_HEAD_DIM)
    xt = x.reshape(m, d)
    mem_kv_all = memory_kv(mem.reshape(b * mem_len, d), norm_mem, w_mem_kv)
    mem_kv_all = mem_kv_all.reshape(DEPTH * b, mem_len, 2 * MEM_WIDTH)
    for i in range(DEPTH):
        kind, slot = i % N_MIXERS, i // N_MIXERS
        if i == 0:
            lhs, gain, tn = xt, norm_mix[i], IN_TN
        else:
            lhs, gain, tn = h_next, None, IN_TN_PRENORMED
        if kind == 0:
            n_in = MIX_WIDTH + MEM_WIDTH
            proj = norm_matmul(lhs, gain, pool_w_in, slot, n_in, IN_TM, tn, BF16)
            proj = proj.reshape(b, s, n_in)
            mix = pool_mixer(proj, pool_w_grp, slot, pool_scale[slot])
        elif kind == 1:
            n_in = 3 * MIX_WIDTH + MEM_WIDTH
            proj = norm_matmul(lhs, gain, diff_w_in, slot, n_in, IN_TM, tn, BF16,
                               rope=(cos, sin_signed, MIX_WIDTH, MIX_WIDTH, s))
            proj = proj.reshape(b, s, n_in)
            mix = diff_attention(proj, diff_lambda[slot], diff_subln[slot], _lambda_init(i))
        else:
            n_main = MIX_WIDTH + SSD_CONV_DIM
            lanes = 2 * SSD_HEAD_DIM
            w_dt = ssd_w_in[slot, :, n_main:SSD_MIX_IN].reshape(d, SSD_GROUPS, SSD_HEADS_PER_GROUP)
            w_dt = jnp.pad(w_dt, ((0, 0), (0, 0), (0, lanes - SSD_HEADS_PER_GROUP)))
            w_tail = jnp.concatenate(
                [ssd_w_in[slot, :, SSD_MIX_IN:], w_dt.reshape(d, SSD_GROUPS * lanes)], axis=1)
            n_in = n_main + SSD_TAIL
            proj = norm_matmul(lhs, gain, ssd_w_in, slot, n_main, IN_TM, tn, F32, tail=w_tail)
            proj = proj.reshape(b, s, n_in)
            mix = ssd_scan(proj, (n_main + MEM_WIDTH) // lanes, ssd_conv_w[slot], ssd_conv_b[slot],
                           ssd_dt_bias[slot], ssd_a_log[slot], ssd_d[slot], ssd_norm[slot])
        q_block = (MIX_WIDTH if kind == 0 else 3 * MIX_WIDTH if kind == 1
                   else MIX_WIDTH + SSD_CONV_DIM) // MEM_WIDTH
        mem_out = memory_attention(proj, q_block, mem_kv_all, i)
        xt = out_projection(xt, mix.reshape(m, MIX_WIDTH), mem_out.reshape(m, MEM_WIDTH), w_out, i)
        last = i == DEPTH - 1
        res = hier_moe_layer(xt, norm_ffn[i], moe_w_group[i], moe_b_group[i], moe_w_expert[i],
                             moe_b_expert[i], moe_w_gate, moe_w_up, moe_w_down, i,
                             norm_final if last else norm_mix[i + 1], last)
        if last:
            xt = res
        else:
            xt, h_next = res
    return xt.reshape(b, s, d)
```

```python
import functools
import math

import jax
import jax.numpy as jnp
from jax import lax
from jax.experimental import pallas as pl
from jax.experimental.pallas import tpu as pltpu

F32 = jnp.float32
BF16 = jnp.bfloat16

D_MODEL = 2048
DEPTH = 4
N_MIXERS = 3
MIX_WIDTH = 1536
MEM_WIDTH = 512
MEM_HEADS = 4
MEM_HEAD_DIM = 128
POOL_WINDOWS = (2, 4, 8, 16)
POOL_GROUPS = 4
POOL_GROUP_DIM = 384
DIFF_HEAD_DIM = 128
DIFF_HEADS = 6
DIFF_V_DIM = 256
ROPE_THETA = 10000.0
SSD_HEAD_DIM = 64
SSD_HEADS = 24
SSD_GROUPS = 4
SSD_HEADS_PER_GROUP = 6
SSD_STATE = 128
SSD_CONV = 4
SSD_CHUNK = 128
SSD_CONV_DIM = 2560
SSD_MIX_IN = 4120
SSD_TAIL = 1024
N_EXPERT_GROUPS = 4
EXPERTS_PER_GROUP = 4
N_EXPERTS = 16
D_EXPERT = 512
RMS_EPS = 1e-6

IN_TM, IN_TN = 2048, 256
IN_TN_PRENORMED = 512
MOE_TILE = 512
MOE_SUB = 512
SEG_ALIGN = 16
MOE_CBUF = 2 * MOE_SUB + 256
MOE_WCOLS = 256
MIB = 1024 * 1024

NT_DIMS = (((1,), (1,)), ((), ()))
TN_DIMS = (((0,), (0,)), ((), ()))


def _cp(sem, vmem_mib):
    return pltpu.CompilerParams(dimension_semantics=sem, vmem_limit_bytes=vmem_mib * MIB)


def _sigmoid(x):
    return 1.0 / (1.0 + jnp.exp(-x))


def _softplus(x):
    return jnp.maximum(x, 0.0) + jnp.log1p(jnp.exp(-jnp.abs(x)))


def _split3(v):
    hi = v.astype(BF16)
    r = v - hi.astype(F32)
    mid = r.astype(BF16)
    lo = (r - mid.astype(F32)).astype(BF16)
    return hi, mid, lo


def _norm_matmul_kernel(*refs, rope_tiles, main_tiles, prenormed):
    refs = list(refs)
    x_ref = refs.pop(0)
    g_ref = None if prenormed else refs.pop(0)
    w_ref = refs.pop(0)
    if rope_tiles is not None:
        cos_ref, sin_ref = refs.pop(0), refs.pop(0)
    if main_tiles is not None:
        wt_ref = refs.pop(0)
    o_ref = refs.pop(0)
    j = pl.program_id(1)

    if prenormed:
        h_ref = x_ref
    else:
        h_ref = refs.pop(0)

        @pl.when(j == 0)
        def _():
            x = x_ref[...]
            ms = jnp.mean(x * x, axis=-1, keepdims=True)
            h_ref[...] = (x * lax.rsqrt(ms + RMS_EPS) * g_ref[...]).astype(BF16)

    def product(weights_ref):
        return jnp.dot(h_ref[...], weights_ref[...].astype(BF16), preferred_element_type=F32)

    if main_tiles is not None:
        @pl.when(j < main_tiles)
        def _():
            o_ref[...] = product(w_ref).astype(o_ref.dtype)

        @pl.when(j >= main_tiles)
        def _():
            o_ref[...] = product(wt_ref).astype(o_ref.dtype)
        return
    acc = product(w_ref)
    if rope_tiles is None:
        o_ref[...] = acc.astype(o_ref.dtype)
        return
    n_q, n_k = rope_tiles
    hd = DIFF_HEAD_DIM

    @pl.when(j < n_q + n_k)
    def _():
        scale = jnp.where(j < n_q, hd ** -0.5, 1.0)
        cos = cos_ref[...]
        sin = sin_ref[...]
        for c in range(o_ref.shape[1] // hd):
            x = acc[:, c * hd:(c + 1) * hd]
            r = x * cos + pltpu.roll(x, hd // 2, axis=1) * sin
            o_ref[:, c * hd:(c + 1) * hd] = (r * scale).astype(o_ref.dtype)

    @pl.when(j >= n_q + n_k)
    def _():
        o_ref[...] = acc.astype(o_ref.dtype)


def norm_matmul(x, g, w, li, n_cols, tm, tn, out_dtype, rope=None, tail=None):
    m, k = x.shape
    prenormed = g is None
    main_tiles = None if tail is None else n_cols // tn
    w_map = ((lambda i, j: (li, 0, j)) if tail is None
             else (lambda i, j: (li, 0, jnp.minimum(j, main_tiles - 1))))
    in_specs = [pl.BlockSpec((tm, k), lambda i, j: (i, 0))]
    args = [x]
    if not prenormed:
        in_specs.append(pl.BlockSpec((1, k), lambda i, j: (0, 0)))
        args.append(g.reshape(1, k))
    in_specs.append(pl.BlockSpec((None, k, tn), w_map))
    args.append(w)
    rope_tiles = None
    if rope is not None:
        cos, sin_signed, n_q, n_k, seq = rope
        rope_tiles = (n_q // tn, n_k // tn)
        pos_blocks = seq // tm
        tab = pl.BlockSpec((tm, DIFF_HEAD_DIM), lambda i, j: (i % pos_blocks, 0))
        in_specs += [tab, tab]
        args += [cos, sin_signed]
    if tail is not None:
        in_specs.append(pl.BlockSpec((k, tn), lambda i, j: (0, jnp.maximum(j - main_tiles, 0))))
        args.append(tail)
        n_cols = n_cols + tail.shape[1]
    return pl.pallas_call(
        functools.partial(_norm_matmul_kernel, rope_tiles=rope_tiles, main_tiles=main_tiles,
                          prenormed=prenormed),
        grid=(m // tm, n_cols // tn),
        in_specs=in_specs,
        out_specs=pl.BlockSpec((tm, tn), lambda i, j: (i, j)),
        out_shape=jax.ShapeDtypeStruct((m, n_cols), out_dtype),
        scratch_shapes=[] if prenormed else [pltpu.VMEM((tm, k), BF16)],
        compiler_params=_cp(("parallel", "parallel" if prenormed else "arbitrary"), 58),
        name="norm_matmul",
    )(*args)


def _mem_kv_kernel(x_ref, g_ref, w_ref, o_ref):
    x = x_ref[...]
    ms = jnp.mean(x * x, axis=-1, keepdims=True)
    h = (x * lax.rsqrt(ms + RMS_EPS) * g_ref[0]).astype(BF16)
    o_ref[0] = jnp.dot(h, w_ref[0].astype(BF16), preferred_element_type=F32)


def memory_kv(mem2, norm_mem, w_mem_kv):
    rows, k = mem2.shape
    layers, _, n = w_mem_kv.shape
    tn = 512
    return pl.pallas_call(
        _mem_kv_kernel,
        grid=(layers, n // tn),
        in_specs=[
            pl.BlockSpec((rows, k), lambda l, j: (0, 0)),
            pl.BlockSpec((1, 1, k), lambda l, j: (l, 0, 0)),
            pl.BlockSpec((1, k, tn), lambda l, j: (l, 0, j)),
        ],
        out_specs=pl.BlockSpec((1, rows, tn), lambda l, j: (l, 0, j)),
        out_shape=jax.ShapeDtypeStruct((layers, rows, n), F32),
        compiler_params=_cp(("parallel", "parallel"), 48),
        name="memory_kv",
    )(mem2, norm_mem.reshape(layers, 1, k), w_mem_kv)


def _pool_kernel(u_ref, w_ref, sc_ref, o_ref, pad_ref):
    grp = pl.program_id(1)
    s, c = u_ref.shape[1], u_ref.shape[2]
    rows = 256
    pad_ref[0:16, :] = jnp.zeros((16, c), F32)
    pad_ref[16:, :] = u_ref[0].astype(F32)
    wb = w_ref[...].astype(BF16)
    sc = sc_ref[0]

    for gi, win in enumerate(POOL_WINDOWS):
        @pl.when(grp == gi)
        def _(win=win):
            for r in range(s // rows):
                xh = pad_ref[r * rows:r * rows + rows + 16, :]
                acc = xh
                k = 1
                while k < win:
                    acc = acc + pltpu.roll(acc, k, axis=0)
                    k *= 2
                t = r * rows + lax.broadcasted_iota(jnp.int32, (rows, 1), 0)
                cnt = jnp.minimum(t + 1, win).astype(F32)
                mixed = (acc[16:, :] / cnt - xh[16:, :]).astype(BF16)
                o_ref[0, r * rows:(r + 1) * rows, :] = (
                    jnp.dot(mixed, wb, preferred_element_type=F32) * sc).astype(o_ref.dtype)


def pool_mixer(proj3, w_grp, li, scale):
    b, s, _ = proj3.shape
    c = POOL_GROUP_DIM
    return pl.pallas_call(
        _pool_kernel,
        grid=(b, POOL_GROUPS),
        in_specs=[
            pl.BlockSpec((1, s, c), lambda i, g: (i, 0, g)),
            pl.BlockSpec((None, None, c, c), lambda i, g: (li, g, 0, 0)),
            pl.BlockSpec((1, 1, c), lambda i, g: (g, 0, 0)),
        ],
        out_specs=pl.BlockSpec((1, s, c), lambda i, g: (i, 0, g)),
        out_shape=jax.ShapeDtypeStruct((b, s, MIX_WIDTH), BF16),
        scratch_shapes=[pltpu.VMEM((s + 16, c), F32)],
        compiler_params=_cp(("parallel", "parallel"), 40),
        name="pool_mixer",
    )(proj3, w_grp, scale.reshape(POOL_GROUPS, 1, c))


def _diff_attn_kernel(lam_ref, sub_ref, q_ref, k_ref, v_ref, o_ref, *, lambda_init):
    lam = lam_ref[...]
    s1 = jnp.sum(lam[0:1] * lam[1:2], axis=-1, keepdims=True)
    s2 = jnp.sum(lam[2:3] * lam[3:4], axis=-1, keepdims=True)
    lmbda = jnp.exp(s1) - jnp.exp(s2) + lambda_init
    s = q_ref.shape[1]
    tq = 512
    d = DIFF_HEAD_DIM
    for i in range(s // tq):
        kv = (i + 1) * tq
        q = q_ref[0, i * tq:(i + 1) * tq, :]
        row = i * tq + lax.broadcasted_iota(jnp.int32, (tq, kv), 0)
        col = lax.broadcasted_iota(jnp.int32, (tq, kv), 1)
        mask = col <= row
        outs = []
        for c in range(2):
            sc = lax.dot_general(q[:, c * d:(c + 1) * d], k_ref[0, 0:kv, c * d:(c + 1) * d],
                                 NT_DIMS, preferred_element_type=F32)
            sc = jnp.where(mask, sc, -jnp.inf)
            e = jnp.exp(sc - jnp.max(sc, axis=-1, keepdims=True))
            pv = jnp.dot(e.astype(BF16), v_ref[0, 0:kv, :], preferred_element_type=F32)
            outs.append(pv / jnp.sum(e, axis=-1, keepdims=True))
        o = outs[0] - lmbda * outs[1]
        ms = jnp.mean(o * o, axis=-1, keepdims=True)
        o_ref[0, i * tq:(i + 1) * tq, :] = (
            o * lax.rsqrt(ms + RMS_EPS) * sub_ref[...] * (1.0 - lambda_init)).astype(o_ref.dtype)


def diff_attention(proj3, lam, subln, lambda_init):
    b, s, _ = proj3.shape
    vd = DIFF_V_DIM
    nh = DIFF_HEADS
    return pl.pallas_call(
        functools.partial(_diff_attn_kernel, lambda_init=lambda_init),
        grid=(b, nh),
        in_specs=[
            pl.BlockSpec((4, DIFF_HEAD_DIM), lambda i, h: (0, 0)),
            pl.BlockSpec((1, vd), lambda i, h: (0, 0)),
            pl.BlockSpec((1, s, vd), lambda i, h: (i, 0, h)),
            pl.BlockSpec((1, s, vd), lambda i, h: (i, 0, nh + h)),
            pl.BlockSpec((1, s, vd), lambda i, h: (i, 0, 2 * nh + h)),
        ],
        out_specs=pl.BlockSpec((1, s, vd), lambda i, h: (i, 0, h)),
        out_shape=jax.ShapeDtypeStruct((b, s, MIX_WIDTH), BF16),
        compiler_params=_cp(("parallel", "parallel"), 48),
        name="diff_attention",
    )(lam, subln.reshape(1, vd), proj3, proj3, proj3)


CONV_HALO = 8
SSD_STEP_CHUNKS = 2


def _causal_conv_silu(u_ref, halo_ref, w_ref, b_ref):
    cur = u_ref[0]
    ext = jnp.concatenate([halo_ref[...], cur], axis=0)
    w = w_ref[...]
    y = ext * w[SSD_CONV - 1:SSD_CONV]
    for j in range(1, SSD_CONV):
        y = y + pltpu.roll(ext, j, axis=0) * w[SSD_CONV - 1 - j:SSD_CONV - j]
    halo_ref[...] = cur[cur.shape[0] - CONV_HALO:, :]
    y = y[CONV_HALO:, :] + b_ref[...]
    return y * _sigmoid(y)


def _dot3(a_f32, b_bf16):
    return sum(jnp.dot(t, b_bf16, preferred_element_type=F32) for t in _split3(a_f32))


def _ssd_kernel(x_ref, b_ref, c_ref, z_ref, dt_ref, wx_ref, wb_ref, wc_ref, cbx_ref, cbb_ref, cbc_ref,
                bias_ref, alog_ref, dx_ref, ng_ref, o_ref,
                state_ref, y_ref, xs_ref, hx_ref, hb_ref, hc_ref):
    @pl.when(pl.program_id(2) == 0)
    def _():
        state_ref[...] = jnp.zeros(state_ref.shape, F32)
        hx_ref[...] = jnp.zeros(hx_ref.shape, F32)
        hb_ref[...] = jnp.zeros(hb_ref.shape, F32)
        hc_ref[...] = jnp.zeros(hc_ref.shape, F32)

    ln = SSD_CHUNK
    hg = SSD_HEADS_PER_GROUP
    lanes = 2 * SSD_HEAD_DIM
    xs_ref[...] = _causal_conv_silu(x_ref, hx_ref, wx_ref, cbx_ref)
    bm_blk = _causal_conv_silu(b_ref, hb_ref, wb_ref, cbb_ref)
    cm_blk = _causal_conv_silu(c_ref, hc_ref, wc_ref, cbc_ref).astype(BF16)
    row = lax.broadcasted_iota(jnp.int32, (ln, ln), 0)
    col = lax.broadcasted_iota(jnp.int32, (ln, ln), 1)
    causal = col <= row
    ones_lower = jnp.where(causal, 1.0, 0.0).astype(BF16)
    sel_r = lax.broadcasted_iota(jnp.int32, (lanes, hg * lanes), 0)
    sel_c = lax.broadcasted_iota(jnp.int32, (lanes, hg * lanes), 1)
    spread = jnp.where(jnp.right_shift(sel_c, lanes.bit_length() - 1) == sel_r,
                       1.0, 0.0).astype(BF16)
    low_half = lax.broadcasted_iota(jnp.int32, (ln, lanes), 1) < SSD_HEAD_DIM
    dt_blk = _softplus(dt_ref[0] + bias_ref[0])
    neg_a = -jnp.exp(alog_ref[0])

    for cc in range(x_ref.shape[1] // ln):
        rows = slice(cc * ln, (cc + 1) * ln)
        bm = bm_blk[rows]
        bm_t = bm.T.astype(BF16)
        cm = cm_blk[rows]
        dt = dt_blk[rows]
        acs = sum(jnp.dot(ones_lower, t, preferred_element_type=F32) for t in _split3(dt * neg_a))
        acs_rows = acs.T
        dt_all = _dot3(dt, spread)
        acs_all = _dot3(acs, spread)
        scores = lax.dot_general(cm, bm.astype(BF16), NT_DIMS, preferred_element_type=F32)
        for k in range(hg // 2):
            heads = (2 * k, 2 * k + 1)
            tile = lambda a, h: a[:, h * lanes:(h + 1) * lanes]
            x = xs_ref[rows, k * lanes:(k + 1) * lanes]
            dt_p = jnp.where(low_half, tile(dt_all, heads[0]), tile(dt_all, heads[1]))
            acs_p = jnp.where(low_half, tile(acs_all, heads[0]), tile(acs_all, heads[1]))
            xc = x * dt_p
            prev = state_ref[k]
            y = jnp.dot(cm, prev.astype(BF16), preferred_element_type=F32) * jnp.exp(acs_p)
            for h, own in zip(heads, (low_half, ~low_half)):
                decay = jnp.exp(
                    jnp.where(causal, tile(acs_all, h) - acs_rows[h:h + 1, :], -jnp.inf))
                y = y + jnp.dot((scores * decay).astype(BF16),
                                jnp.where(own, xc, 0.0).astype(BF16), preferred_element_type=F32)
            a_last = acs_p[ln - 1:ln, :]
            st = jnp.dot(bm_t, (xc * jnp.exp(a_last - acs_p)).astype(BF16),
                         preferred_element_type=F32)
            state_ref[k] = prev * jnp.exp(a_last) + st
            y_ref[rows, k * lanes:(k + 1) * lanes] = y

    z = z_ref[0]
    yz = (y_ref[...] + dx_ref[0] * xs_ref[...]) * (z * _sigmoid(z))
    ms = jnp.mean(yz * yz, axis=-1, keepdims=True)
    o_ref[0] = (yz * lax.rsqrt(ms + RMS_EPS) * ng_ref[0]).astype(o_ref.dtype)


def ssd_scan(proj3, dt_blk, conv_w, conv_b, dt_bias, a_log, d_skip, norm_g):
    b, s, _ = proj3.shape
    ln, g, hg, n = SSD_CHUNK, SSD_GROUPS, SSD_HEADS_PER_GROUP, SSD_STATE
    gw = hg * SSD_HEAD_DIM
    lanes = 2 * SSD_HEAD_DIM
    assert ln == lanes and n == lanes

    def head_lanes(v):
        v = v.reshape(v.shape[:-1] + (g, hg))
        v = jnp.pad(v, [(0, 0)] * (v.ndim - 1) + [(0, lanes - hg)])
        return v.reshape(v.shape[:-2] + (g * lanes,))

    per_group = pl.BlockSpec((1, 1, lanes), lambda i, j, c: (j, 0, 0))
    d_chan = jnp.repeat(d_skip, SSD_HEAD_DIM)
    x_blk, b_blk = MIX_WIDTH // gw, (2 * MIX_WIDTH) // n
    c_blk = b_blk + g
    cw_b, cw_c = MIX_WIDTH // n, MIX_WIDTH // n + g
    conv_b2 = conv_b.reshape(1, SSD_CONV_DIM)
    rows = SSD_STEP_CHUNKS * ln
    return pl.pallas_call(
        _ssd_kernel,
        grid=(b, g, s // rows),
        in_specs=[
            pl.BlockSpec((1, rows, gw), lambda i, j, c: (i, c, x_blk + j)),
            pl.BlockSpec((1, rows, n), lambda i, j, c: (i, c, b_blk + j)),
            pl.BlockSpec((1, rows, n), lambda i, j, c: (i, c, c_blk + j)),
            pl.BlockSpec((1, rows, gw), lambda i, j, c: (i, c, j)),
            pl.BlockSpec((1, rows, lanes), lambda i, j, c: (i, c, dt_blk + j)),
            pl.BlockSpec((SSD_CONV, gw), lambda i, j, c: (0, j)),
            pl.BlockSpec((SSD_CONV, n), lambda i, j, c: (0, cw_b + j)),
            pl.BlockSpec((SSD_CONV, n), lambda i, j, c: (0, cw_c + j)),
            pl.BlockSpec((1, gw), lambda i, j, c: (0, j)),
            pl.BlockSpec((1, n), lambda i, j, c: (0, cw_b + j)),
            pl.BlockSpec((1, n), lambda i, j, c: (0, cw_c + j)),
            per_group, per_group,
            pl.BlockSpec((1, 1, gw), lambda i, j, c: (j, 0, 0)),
            pl.BlockSpec((1, 1, gw), lambda i, j, c: (j, 0, 0)),
        ],
        out_specs=pl.BlockSpec((1, rows, gw), lambda i, j, c: (i, c, j)),
        out_shape=jax.ShapeDtypeStruct((b, s, MIX_WIDTH), BF16),
        scratch_shapes=[pltpu.VMEM((hg // 2, n, lanes), F32), pltpu.VMEM((rows, gw), F32),
                        pltpu.VMEM((rows, gw), F32), pltpu.VMEM((CONV_HALO, gw), F32),
                        pltpu.VMEM((CONV_HALO, n), F32), pltpu.VMEM((CONV_HALO, n), F32)],
        compiler_params=_cp(("parallel", "parallel", "arbitrary"), 32),
        name="ssd_scan",
    )(proj3, proj3, proj3, proj3, proj3, conv_w, conv_w, conv_w,
      conv_b2, conv_b2, conv_b2, head_lanes(dt_bias).reshape(g, 1, lanes),
      head_lanes(a_log).reshape(g, 1, lanes), d_chan.reshape(g, 1, gw), norm_g.reshape(g, 1, gw))


def _memory_attention(q_ref, kv_ref, o_ref):
    d = MEM_HEAD_DIM
    for h in range(MEM_HEADS):
        q = (q_ref[0, :, h * d:(h + 1) * d].astype(F32) * (d ** -0.5)).astype(BF16)
        k = kv_ref[0, :, h * d:(h + 1) * d].astype(BF16)
        v = kv_ref[0, :, MEM_WIDTH + h * d:MEM_WIDTH + (h + 1) * d].astype(BF16)
        sc = lax.dot_general(q, k, NT_DIMS, preferred_element_type=F32)
        e = jnp.exp(sc - jnp.max(sc, axis=-1, keepdims=True))
        pr = (e / jnp.sum(e, axis=-1, keepdims=True)).astype(BF16)
        o_ref[:, h * d:(h + 1) * d] = jnp.dot(
            pr, v, preferred_element_type=F32).astype(o_ref.dtype)


def _outproj_kernel(x_ref, a_ref, q_ref, kv_ref, wa_ref, wm_ref, o_ref, mo_ref):
    @pl.when(pl.program_id(1) == 0)
    def _():
        _memory_attention(q_ref, kv_ref, mo_ref)

    acc = jnp.dot(a_ref[...], wa_ref[...].astype(BF16), preferred_element_type=F32)
    acc = acc + jnp.dot(mo_ref[...], wm_ref[...].astype(BF16), preferred_element_type=F32)
    o_ref[...] = x_ref[...] + acc


def out_projection(x, mix, proj3, q_block, mem_kv, w_out, li):
    m, d = x.shape
    b, s, _ = proj3.shape
    mem_len = mem_kv.shape[1]
    tm, tn = s, 512
    return pl.pallas_call(
        _outproj_kernel,
        grid=(m // tm, d // tn),
        in_specs=[
            pl.BlockSpec((tm, tn), lambda i, j: (i, j)),
            pl.BlockSpec((tm, MIX_WIDTH), lambda i, j: (i, 0)),
            pl.BlockSpec((1, s, MEM_WIDTH), lambda i, j: (i, 0, q_block)),
            pl.BlockSpec((1, mem_len, 2 * MEM_WIDTH), lambda i, j: (li * b + i, 0, 0)),
            pl.BlockSpec((None, MIX_WIDTH, tn), lambda i, j: (li, 0, j)),
            pl.BlockSpec((None, MEM_WIDTH, tn), lambda i, j: (li, MIX_WIDTH // MEM_WIDTH, j)),
        ],
        out_specs=pl.BlockSpec((tm, tn), lambda i, j: (i, j)),
        out_shape=jax.ShapeDtypeStruct((m, d), F32),
        scratch_shapes=[pltpu.VMEM((tm, MEM_WIDTH), BF16)],
        compiler_params=_cp(("parallel", "arbitrary"), 58),
        name="out_projection",
    )(x, mix, proj3, mem_kv, w_out, w_out)


def _router_kernel(x_ref, g_ref, wr_ref, br_ref, r_ref, seg_ref, t_ref):
    x = x_ref[...]
    ms = jnp.mean(x * x, axis=-1, keepdims=True)
    t = x * lax.rsqrt(ms + RMS_EPS) * g_ref[...]
    th = t.astype(BF16)
    t_ref[...] = th
    tl = (t - th.astype(F32)).astype(BF16)
    w = wr_ref[...]
    wh = w.astype(BF16)
    wl = (w - wh.astype(F32)).astype(BF16)
    lg = (lax.dot_general(wh, th, NT_DIMS, preferred_element_type=F32)
          + lax.dot_general(wh, tl, NT_DIMS, preferred_element_type=F32)
          + lax.dot_general(wl, th, NT_DIMS, preferred_element_type=F32)) + br_ref[...]
    ng, epg = N_EXPERT_GROUPS, EXPERTS_PER_GROUP
    gl = [lg[j:j + 1] for j in range(ng)]
    el = [lg[ng + j:ng + j + 1] for j in range(N_EXPERTS)]

    def first_argmax(vals):
        top = functools.reduce(jnp.maximum, vals)
        idx = jnp.full(top.shape, len(vals) - 1, jnp.int32)
        for j in range(len(vals) - 2, -1, -1):
            idx = jnp.where(vals[j] >= top, j, idx)
        return top, idx

    gmax, gsel = first_argmax(gl)
    g_w = 1.0 / functools.reduce(lambda a, b: a + b, [jnp.exp(v - gmax) for v in gl])
    e_in = []
    for j in range(epg):
        v = el[(ng - 1) * epg + j]
        for gi in range(ng - 2, -1, -1):
            v = jnp.where(gsel == gi, el[gi * epg + j], v)
        e_in.append(v)
    emax = functools.reduce(jnp.maximum, e_in)
    pe = [jnp.exp(v - emax) for v in e_in]
    se = functools.reduce(lambda a, b: a + b, pe)
    prob = [v / se for v in pe]
    v1, i1 = first_argmax(prob)
    rest = [jnp.where(i1 == j, -1.0, prob[j]) for j in range(epg)]
    v2, i2 = first_argmax(rest)
    tot = v1 + v2
    e1 = gsel * epg + i1
    e2 = gsel * epg + i2

    tm = x.shape[0]
    eidx = lax.broadcasted_iota(jnp.int32, (N_EXPERTS, tm), 0)
    hit1 = eidx == e1
    hit2 = eidx == e2
    onehot = jnp.where(hit1 | hit2, 1.0, 0.0)
    tok_r = lax.broadcasted_iota(jnp.int32, (tm, tm), 0)
    tok_c = lax.broadcasted_iota(jnp.int32, (tm, tm), 1)
    earlier = jnp.where(tok_r < tok_c, 1.0, 0.0).astype(BF16)
    rank = jnp.dot(onehot.astype(BF16), earlier, preferred_element_type=F32)
    cnt = jnp.sum(onehot, axis=1, keepdims=True).astype(jnp.int32)
    seg = jnp.bitwise_and(cnt + (SEG_ALIGN - 1), -SEG_ALIGN)
    ex_r = lax.broadcasted_iota(jnp.int32, (N_EXPERTS, N_EXPERTS), 0)
    ex_c = lax.broadcasted_iota(jnp.int32, (N_EXPERTS, N_EXPERTS), 1)
    lower = jnp.where(ex_c < ex_r, 1.0, 0.0).astype(BF16)
    seg_lanes = jnp.broadcast_to(seg.astype(F32), (N_EXPERTS, 128))
    start = jnp.dot(lower, seg_lanes.astype(BF16), preferred_element_type=F32)[:, 0:1]
    slot = start + rank
    lpos1 = jnp.sum(jnp.where(hit1, slot, 0.0), axis=0, keepdims=True)
    lpos2 = jnp.sum(jnp.where(hit2, slot, 0.0), axis=0, keepdims=True)
    r_ref[...] = jnp.concatenate(
        [e1.astype(F32), e2.astype(F32), v1 / tot * g_w, v2 / tot * g_w, lpos1, lpos2,
         jnp.zeros((2, tm), F32)], axis=0)
    seg_ref[0] = jnp.broadcast_to(seg, (N_EXPERTS, 128))


def moe_router(x, g, w_group, b_group, w_expert, b_expert):
    m, d = x.shape
    tm = MOE_SUB
    nr = 32
    wr = jnp.zeros((nr, d), F32).at[:N_EXPERT_GROUPS].set(w_group.T)
    wr = wr.at[N_EXPERT_GROUPS:N_EXPERT_GROUPS + N_EXPERTS].set(w_expert.T)
    br = jnp.zeros((nr, 1), F32).at[:N_EXPERT_GROUPS, 0].set(b_group)
    br = br.at[N_EXPERT_GROUPS:N_EXPERT_GROUPS + N_EXPERTS, 0].set(b_expert)
    route, seg3, t = pl.pallas_call(
        _router_kernel,
        grid=(m // tm,),
        in_specs=[
            pl.BlockSpec((tm, d), lambda i: (i, 0)),
            pl.BlockSpec((1, d), lambda i: (0, 0)),
            pl.BlockSpec((nr, d), lambda i: (0, 0)),
            pl.BlockSpec((nr, 1), lambda i: (0, 0)),
        ],
        out_specs=[pl.BlockSpec((8, tm), lambda i: (0, i)),
                   pl.BlockSpec((1, N_EXPERTS, 128), lambda i: (i, 0, 0)),
                   pl.BlockSpec((tm, d), lambda i: (i, 0))],
        out_shape=[jax.ShapeDtypeStruct((8, m), F32),
                   jax.ShapeDtypeStruct((m // tm, N_EXPERTS, 128), jnp.int32),
                   jax.ShapeDtypeStruct((m, d), BF16)],
        compiler_params=_cp(("parallel",), 40),
        name="moe_router",
    )(x, g.reshape(1, d), wr, br)
    return route, seg3[:, :, 0], t


def moe_plan(seg, m):
    nsub = m // MOE_SUB
    tm = MOE_TILE
    loc_off = jnp.cumsum(seg, axis=1) - seg
    reg_off = jnp.cumsum(seg, axis=0) - seg
    length = jnp.sum(seg, axis=0)
    padded = (length + tm - 1) // tm * tm
    e_end = jnp.cumsum(padded)
    e_start = e_end - padded
    glob_off = e_start[None, :] + reg_off
    n_slots = (2 * m + nsub * N_EXPERTS * (SEG_ALIGN - 1) + N_EXPERTS * (tm - 1) + tm - 1) // tm * tm
    tile_start = jnp.arange(n_slots // tm, dtype=jnp.int32) * tm
    tile_expert = jnp.minimum(
        jnp.sum((tile_start[:, None] >= e_end[None, :]).astype(jnp.int32), axis=1), N_EXPERTS - 1)
    ar = jnp.arange(N_EXPERTS, dtype=jnp.int32)
    later = (ar[None, :] > ar[:, None]) & (padded[None, :] > 0)
    nxt = jnp.min(jnp.where(later, ar[None, :], N_EXPERTS), axis=1)
    nxt = jnp.where(nxt == N_EXPERTS, -1, nxt).astype(jnp.int32)
    tile_next = jnp.sum(jnp.where(tile_expert[:, None] == ar[None, :], nxt[None, :], 0), axis=1)
    return dict(seg=seg.reshape(-1), loc_off=loc_off.reshape(-1), glob_off=glob_off.reshape(-1),
                n_slots=n_slots, tile_expert=tile_expert, tile_next=tile_next,
                n_used=(e_end[-1:] // tm).astype(jnp.int32),
                pad_start=jnp.concatenate([e_start + length, e_end[-1:]]).astype(jnp.int32),
                pad_len=jnp.concatenate([padded - length, n_slots - e_end[-1:]]).astype(jnp.int32))


def _aligned(v):
    return v if isinstance(v, int) else pl.multiple_of(v, SEG_ALIGN)


def _seg_copy(src_ref, dst_ref, sem, src_row, dst_row, n_rows):
    n_rows = _aligned(n_rows)
    return pltpu.make_async_copy(src_ref.at[pl.ds(_aligned(src_row), n_rows)],
                                 dst_ref.at[pl.ds(_aligned(dst_row), n_rows)], sem)


def _segment_copies(step, seg_ref, loc_ref, glob_ref, make):
    out = []
    for e in range(N_EXPERTS):
        k = step * N_EXPERTS + e
        out.append((seg_ref[k], make(loc_ref[k], glob_ref[k], seg_ref[k])))
    return out


def _start_all(copies):
    for n, cp in copies:
        @pl.when(n > 0)
        def _(cp=cp):
            cp.start()


def _wait_all(copies):
    for n, cp in copies:
        @pl.when(n > 0)
        def _(cp=cp):
            cp.wait()


def _dispatch_kernel(seg_ref, loc_ref, glob_ref, pst_ref, pln_ref, t_ref, rt_ref,
                     xs_ref, cbuf_ref, zbuf_ref, sems):
    s = pl.program_id(0)
    n_steps = pl.num_programs(0)
    buf = s % 2
    d = t_ref.shape[1]

    def copies(step, b):
        return _segment_copies(
            step, seg_ref, loc_ref, glob_ref,
            lambda loc, glob, n: _seg_copy(cbuf_ref.at[b], xs_ref, sems.at[b], loc, glob, n))

    @pl.when(s >= 2)
    def _():
        _wait_all(copies(s - 2, buf))

    t = t_ref[...]
    rt = rt_ref[...]
    lp = rt[4:6].astype(jnp.int32)
    slot = lax.broadcasted_iota(jnp.int32, (MOE_CBUF, MOE_SUB), 0)
    hit0 = slot == lp[0:1]
    hit1 = slot == lp[1:2]
    onehot = jnp.where(hit0 | hit1, 1.0, 0.0).astype(BF16)
    cbuf_ref[buf, :, 0:d] = jnp.dot(onehot, t, preferred_element_type=F32).astype(BF16)
    wslot = jnp.sum(jnp.where(hit0, rt[2:3], 0.0) + jnp.where(hit1, rt[3:4], 0.0),
                    axis=1, keepdims=True)
    w_hi = wslot.astype(BF16)
    w_lo = (wslot - w_hi.astype(F32)).astype(BF16)
    half = MOE_WCOLS // 2
    cbuf_ref[buf, :, d:d + half] = jnp.broadcast_to(w_hi, (MOE_CBUF, half))
    cbuf_ref[buf, :, d + half:] = jnp.broadcast_to(w_lo, (MOE_CBUF, half))
    _start_all(copies(s, buf))

    @pl.when(s == 0)
    def _():
        zsem = sems.at[2]
        zbuf_ref[...] = jnp.zeros(zbuf_ref.shape, BF16)
        pads = [(pln_ref[e], _seg_copy(zbuf_ref, xs_ref, zsem, 0, pst_ref[e], pln_ref[e]))
                for e in range(N_EXPERTS)]
        _start_all(pads)
        _wait_all(pads)
        tail_tiles = pln_ref[N_EXPERTS] // MOE_TILE

        def tail_copy(k):
            return _seg_copy(zbuf_ref, xs_ref, zsem, 0, pst_ref[N_EXPERTS] + k * MOE_TILE, MOE_TILE)

        def start(k, carry):
            tail_copy(k).start()
            return carry

        def wait(k, carry):
            tail_copy(k).wait()
            return carry
        lax.fori_loop(0, tail_tiles, start, 0)
        lax.fori_loop(0, tail_tiles, wait, 0)

    @pl.when(s == n_steps - 1)
    def _():
        @pl.when(s >= 1)
        def _():
            _wait_all(copies(s - 1, 1 - buf))
        _wait_all(copies(s, buf))


def moe_dispatch(t, route, plan):
    m, d = t.shape
    dw = d + MOE_WCOLS
    return pl.pallas_call(
        _dispatch_kernel,
        grid_spec=pltpu.PrefetchScalarGridSpec(
            num_scalar_prefetch=5,
            grid=(m // MOE_SUB,),
            in_specs=[
                pl.BlockSpec((MOE_SUB, d), lambda i, *_: (i, 0)),
                pl.BlockSpec((8, MOE_SUB), lambda i, *_: (0, i)),
            ],
            out_specs=pl.BlockSpec(memory_space=pl.ANY),
            scratch_shapes=[pltpu.VMEM((2, MOE_CBUF, dw), BF16), pltpu.VMEM((MOE_TILE, dw), BF16),
                            pltpu.SemaphoreType.DMA((3,))],
        ),
        out_shape=jax.ShapeDtypeStruct((plan["n_slots"], dw), BF16),
        compiler_params=pltpu.CompilerParams(dimension_semantics=("arbitrary",),
                                             vmem_limit_bytes=40 * MIB, has_side_effects=True),
        name="moe_dispatch",
    )(plan["seg"], plan["loc_off"], plan["glob_off"], plan["pad_start"], plan["pad_len"],
      t, route)


def _ffn_kernel(te_ref, nx_ref, nu_ref, x_ref, wg_hbm, wu_hbm, wd_hbm, o_ref,
                wgs_ref, wus_ref, wds_ref, wgb_ref, wub_ref, wdb_ref, sem, *, li):
    i = pl.program_id(0)
    d = o_ref.shape[1]

    def weight_copies(e):
        return [pltpu.make_async_copy(wg_hbm.at[li, e], wgs_ref, sem),
                pltpu.make_async_copy(wu_hbm.at[li, e], wus_ref, sem),
                pltpu.make_async_copy(wd_hbm.at[li, e], wds_ref, sem)]

    @pl.when(i < nu_ref[0])
    def _():
        e = te_ref[i]

        @pl.when(i == 0)
        def _():
            for cp in weight_copies(e):
                cp.start()

        @pl.when((i == 0) | (e != te_ref[jnp.maximum(i - 1, 0)]))
        def _():
            for cp in weight_copies(e):
                cp.wait()
            wgb_ref[...] = wgs_ref[...].astype(BF16)
            wub_ref[...] = wus_ref[...].astype(BF16)
            wdb_ref[...] = wds_ref[...].astype(BF16)

            @pl.when(nx_ref[i] >= 0)
            def _():
                for cp in weight_copies(nx_ref[i]):
                    cp.start()

        x = x_ref[:, 0:d]
        half = MOE_WCOLS // 2
        w = x_ref[:, d:d + 1].astype(F32) + x_ref[:, d + half:d + half + 1].astype(F32)
        gate = jnp.dot(x, wgb_ref[...], preferred_element_type=F32)
        up = jnp.dot(x, wub_ref[...], preferred_element_type=F32)
        hid = (gate * _sigmoid(gate) * up * w).astype(BF16)
        o_ref[...] = jnp.dot(hid, wdb_ref[...], preferred_element_type=F32).astype(o_ref.dtype)

    @pl.when(i >= nu_ref[0])
    def _():
        o_ref[...] = jnp.zeros(o_ref.shape, o_ref.dtype)


def grouped_ffn(xs, plan, w_gate, w_up, w_down, li):
    p, dw = xs.shape
    d = dw - MOE_WCOLS
    f = D_EXPERT
    tm = MOE_TILE
    hbm = pl.BlockSpec(memory_space=pl.ANY)
    return pl.pallas_call(
        functools.partial(_ffn_kernel, li=li),
        grid_spec=pltpu.PrefetchScalarGridSpec(
            num_scalar_prefetch=3,
            grid=(p // tm,),
            in_specs=[
                pl.BlockSpec((tm, dw), lambda i, te, nx, nu: (jnp.minimum(i, nu[0] - 1), 0)),
                hbm, hbm, hbm,
            ],
            out_specs=pl.BlockSpec((tm, d), lambda i, te, nx, nu: (i, 0)),
            scratch_shapes=[pltpu.VMEM((d, f), F32), pltpu.VMEM((d, f), F32), pltpu.VMEM((f, d), F32),
                            pltpu.VMEM((d, f), BF16), pltpu.VMEM((d, f), BF16),
                            pltpu.VMEM((f, d), BF16), pltpu.SemaphoreType.DMA(())],
        ),
        out_shape=jax.ShapeDtypeStruct((p, d), BF16),
        compiler_params=_cp(("arbitrary",), 48),
        name="grouped_ffn",
    )(plan["tile_expert"], plan["tile_next"], plan["n_used"], xs, w_gate, w_up, w_down)


def _combine_kernel(seg_ref, loc_ref, glob_ref, x_ref, lp_ref, ys_ref, g_ref, *rest, final_norm):
    if final_norm:
        o_ref, ybuf_ref, sems = rest
    else:
        o_ref, h_ref, ybuf_ref, sems = rest
    s = pl.program_id(0)
    n_steps = pl.num_programs(0)
    buf = s % 2

    def copies(step, b):
        return _segment_copies(
            step, seg_ref, loc_ref, glob_ref,
            lambda loc, glob, n: _seg_copy(ys_ref, ybuf_ref.at[b], sems.at[b], glob, loc, n))

    def fetch(step, b):
        ybuf_ref[b] = jnp.zeros(ybuf_ref.shape[1:], BF16)
        _start_all(copies(step, b))

    @pl.when(s == 0)
    def _():
        fetch(s, buf)

    @pl.when(s + 1 < n_steps)
    def _():
        fetch(s + 1, 1 - buf)

    _wait_all(copies(s, buf))
    lp = lp_ref[...]
    slot = lax.broadcasted_iota(jnp.int32, (MOE_SUB, MOE_CBUF), 1)
    onehot = jnp.where((slot == lp[:, 0:1]) | (slot == lp[:, 1:2]), 1.0, 0.0).astype(BF16)
    out = x_ref[...] + jnp.dot(onehot, ybuf_ref[buf], preferred_element_type=F32)
    ms = jnp.mean(out * out, axis=-1, keepdims=True)
    normed = out * lax.rsqrt(ms + RMS_EPS) * g_ref[...]
    if final_norm:
        o_ref[...] = normed
    else:
        o_ref[...] = out
        h_ref[...] = normed.astype(BF16)


def moe_combine(x, ys, route, plan, gain, final_norm):
    m, d = x.shape
    row = pl.BlockSpec((MOE_SUB, d), lambda i, *_: (i, 0))
    out_specs, out_shape = row, jax.ShapeDtypeStruct((m, d), F32)
    if not final_norm:
        out_specs, out_shape = [row, row], [out_shape, jax.ShapeDtypeStruct((m, d), BF16)]
    return pl.pallas_call(
        functools.partial(_combine_kernel, final_norm=final_norm),
        grid_spec=pltpu.PrefetchScalarGridSpec(
            num_scalar_prefetch=3,
            grid=(m // MOE_SUB,),
            in_specs=[
                row,
                pl.BlockSpec((MOE_SUB, 2), lambda i, *_: (i, 0)),
                pl.BlockSpec(memory_space=pl.ANY),
                pl.BlockSpec((1, d), lambda i, *_: (0, 0)),
            ],
            out_specs=out_specs,
            scratch_shapes=[pltpu.VMEM((2, MOE_CBUF, d), BF16), pltpu.SemaphoreType.DMA((2,))],
        ),
        out_shape=out_shape,
        compiler_params=pltpu.CompilerParams(dimension_semantics=("arbitrary",),
                                             vmem_limit_bytes=40 * MIB),
        name="moe_combine",
    )(plan["seg"], plan["loc_off"], plan["glob_off"], x, route[4:6].T.astype(jnp.int32), ys,
      gain.reshape(1, d))


def hier_moe_layer(x, norm_g, w_group, b_group, w_expert, b_expert, w_gate, w_up, w_down, li,
                   out_gain, final_norm):
    m, _ = x.shape
    route, seg, t = moe_router(x, norm_g, w_group, b_group, w_expert, b_expert)
    plan = moe_plan(seg, m)
    xs = moe_dispatch(t, route, plan)
    ys = grouped_ffn(xs, plan, w_gate, w_up, w_down, li)
    return moe_combine(x, ys, route, plan, out_gain, final_norm)


def _rope_tables(seq, dim):
    pos = jnp.arange(seq, dtype=F32)
    inv = ROPE_THETA ** (-jnp.arange(0, dim, 2, dtype=F32) / dim)
    ang = pos[:, None] * inv[None, :]
    ang = jnp.concatenate([ang, ang], axis=-1)
    sign = jnp.concatenate([-jnp.ones((dim // 2,), F32), jnp.ones((dim // 2,), F32)])
    return jnp.cos(ang), jnp.sin(ang) * sign[None, :]


def _lambda_init(depth_idx):
    return 0.8 - 0.6 * math.exp(-0.3 * depth_idx)


def kernel(x, mem, norm_mix, norm_mem, norm_ffn, norm_final, w_out, w_mem_kv, pool_w_in, pool_w_grp, pool_scale, diff_w_in, diff_lambda, diff_subln, ssd_w_in, ssd_conv_w, ssd_conv_b, ssd_dt_bias, ssd_a_log, ssd_d, ssd_norm, moe_w_group, moe_b_group, moe_w_expert, moe_b_expert, moe_w_gate, moe_w_up, moe_w_down):
    b, s, d = x.shape
    m = b * s
    mem_len = mem.shape[1]
    cos, sin_signed = _rope_tables(s, DIFF_HEAD_DIM)
    xt = x.reshape(m, d)
    mem_kv_all = memory_kv(mem.reshape(b * mem_len, d), norm_mem, w_mem_kv)
    mem_kv_all = mem_kv_all.reshape(DEPTH * b, mem_len, 2 * MEM_WIDTH)
    for i in range(DEPTH):
        kind, slot = i % N_MIXERS, i // N_MIXERS
        if i == 0:
            lhs, gain, tn = xt, norm_mix[i], IN_TN
        else:
            lhs, gain, tn = h_next, None, IN_TN_PRENORMED
        if kind == 0:
            n_in = MIX_WIDTH + MEM_WIDTH
            proj = norm_matmul(lhs, gain, pool_w_in, slot, n_in, IN_TM, tn, BF16)
            proj = proj.reshape(b, s, n_in)
            mix = pool_mixer(proj, pool_w_grp, slot, pool_scale[slot])
        elif kind == 1:
            n_in = 3 * MIX_WIDTH + MEM_WIDTH
            proj = norm_matmul(lhs, gain, diff_w_in, slot, n_in, IN_TM, tn, BF16,
                               rope=(cos, sin_signed, MIX_WIDTH, MIX_WIDTH, s))
            proj = proj.reshape(b, s, n_in)
            mix = diff_attention(proj, diff_lambda[slot], diff_subln[slot], _lambda_init(i))
        else:
            n_main = MIX_WIDTH + SSD_CONV_DIM
            lanes = 2 * SSD_HEAD_DIM
            w_dt = ssd_w_in[slot, :, n_main:SSD_MIX_IN].reshape(d, SSD_GROUPS, SSD_HEADS_PER_GROUP)
            w_dt = jnp.pad(w_dt, ((0, 0), (0, 0), (0, lanes - SSD_HEADS_PER_GROUP)))
            w_tail = jnp.concatenate(
                [ssd_w_in[slot, :, SSD_MIX_IN:], w_dt.reshape(d, SSD_GROUPS * lanes)], axis=1)
            n_in = n_main + SSD_TAIL
            proj = norm_matmul(lhs, gain, ssd_w_in, slot, n_main, IN_TM, tn, F32, tail=w_tail)
            proj = proj.reshape(b, s, n_in)
            mix = ssd_scan(proj, (n_main + MEM_WIDTH) // lanes, ssd_conv_w[slot], ssd_conv_b[slot],
                           ssd_dt_bias[slot], ssd_a_log[slot], ssd_d[slot], ssd_norm[slot])
        q_block = (MIX_WIDTH if kind == 0 else 3 * MIX_WIDTH if kind == 1
                   else MIX_WIDTH + SSD_CONV_DIM) // MEM_WIDTH
        xt = out_projection(xt, mix.reshape(m, MIX_WIDTH), proj, q_block, mem_kv_all, w_out, i)
        last = i == DEPTH - 1
        res = hier_moe_layer(xt, norm_ffn[i], moe_w_group[i], moe_b_group[i], moe_w_expert[i],
                             moe_b_expert[i], moe_w_gate, moe_w_up, moe_w_down, i,
                             norm_final if last else norm_mix[i + 1], last)
        if last:
            xt = res
        else:
            xt, h_next = res
    return xt.reshape(b, s, d)
```

```python
import functools
import math

import jax
import jax.numpy as jnp
from jax import lax
from jax.experimental import pallas as pl
from jax.experimental.pallas import tpu as pltpu

F32 = jnp.float32
BF16 = jnp.bfloat16

D_MODEL = 2048
DEPTH = 4
N_MIXERS = 3
MIX_WIDTH = 1536
MEM_WIDTH = 512
MEM_HEADS = 4
MEM_HEAD_DIM = 128
POOL_WINDOWS = (2, 4, 8, 16)
POOL_GROUPS = 4
POOL_GROUP_DIM = 384
DIFF_HEAD_DIM = 128
DIFF_HEADS = 6
DIFF_V_DIM = 256
ROPE_THETA = 10000.0
SSD_HEAD_DIM = 64
SSD_HEADS = 24
SSD_GROUPS = 4
SSD_HEADS_PER_GROUP = 6
SSD_STATE = 128
SSD_CONV = 4
SSD_CHUNK = 128
SSD_CONV_DIM = 2560
SSD_MIX_IN = 4120
SSD_TAIL = 1024
N_EXPERT_GROUPS = 4
EXPERTS_PER_GROUP = 4
N_EXPERTS = 16
D_EXPERT = 512
RMS_EPS = 1e-6

IN_TM, IN_TN = 2048, 256
IN_TN_PRENORMED = 512
MOE_TILE = 512
MOE_SUB = 512
SEG_ALIGN = 16
MOE_CBUF = 2 * MOE_SUB + 256
MOE_WCOLS = 256
MIB = 1024 * 1024

NT_DIMS = (((1,), (1,)), ((), ()))
TN_DIMS = (((0,), (0,)), ((), ()))


def _cp(sem, vmem_mib):
    return pltpu.CompilerParams(dimension_semantics=sem, vmem_limit_bytes=vmem_mib * MIB)


def _sigmoid(x):
    return 1.0 / (1.0 + jnp.exp(-x))


def _softplus(x):
    return jnp.maximum(x, 0.0) + jnp.log1p(jnp.exp(-jnp.abs(x)))


def _split3(v):
    hi = v.astype(BF16)
    r = v - hi.astype(F32)
    mid = r.astype(BF16)
    lo = (r - mid.astype(F32)).astype(BF16)
    return hi, mid, lo


def _norm_matmul_kernel(*refs, rope_tiles, main_tiles, prenormed):
    refs = list(refs)
    x_ref = refs.pop(0)
    g_ref = None if prenormed else refs.pop(0)
    w_ref = refs.pop(0)
    if rope_tiles is not None:
        cos_ref, sin_ref = refs.pop(0), refs.pop(0)
    if main_tiles is not None:
        wt_ref = refs.pop(0)
    o_ref = refs.pop(0)
    j = pl.program_id(1)

    if prenormed:
        h_ref = x_ref
    else:
        h_ref = refs.pop(0)

        @pl.when(j == 0)
        def _():
            x = x_ref[...]
            ms = jnp.mean(x * x, axis=-1, keepdims=True)
            h_ref[...] = (x * lax.rsqrt(ms + RMS_EPS) * g_ref[...]).astype(BF16)

    def product(weights_ref):
        return jnp.dot(h_ref[...], weights_ref[...].astype(BF16), preferred_element_type=F32)

    if main_tiles is not None:
        @pl.when(j < main_tiles)
        def _():
            o_ref[...] = product(w_ref).astype(o_ref.dtype)

        @pl.when(j >= main_tiles)
        def _():
            o_ref[...] = product(wt_ref).astype(o_ref.dtype)
        return
    acc = product(w_ref)
    if rope_tiles is None:
        o_ref[...] = acc.astype(o_ref.dtype)
        return
    n_q, n_k = rope_tiles
    hd = DIFF_HEAD_DIM

    @pl.when(j < n_q + n_k)
    def _():
        scale = jnp.where(j < n_q, hd ** -0.5, 1.0)
        cos = cos_ref[...]
        sin = sin_ref[...]
        for c in range(o_ref.shape[1] // hd):
            x = acc[:, c * hd:(c + 1) * hd]
            r = x * cos + pltpu.roll(x, hd // 2, axis=1) * sin
            o_ref[:, c * hd:(c + 1) * hd] = (r * scale).astype(o_ref.dtype)

    @pl.when(j >= n_q + n_k)
    def _():
        o_ref[...] = acc.astype(o_ref.dtype)


def norm_matmul(x, g, w, li, n_cols, tm, tn, out_dtype, rope=None, tail=None):
    m, k = x.shape
    prenormed = g is None
    main_tiles = None if tail is None else n_cols // tn
    w_map = ((lambda i, j: (li, 0, j)) if tail is None
             else (lambda i, j: (li, 0, jnp.minimum(j, main_tiles - 1))))
    in_specs = [pl.BlockSpec((tm, k), lambda i, j: (i, 0))]
    args = [x]
    if not prenormed:
        in_specs.append(pl.BlockSpec((1, k), lambda i, j: (0, 0)))
        args.append(g.reshape(1, k))
    in_specs.append(pl.BlockSpec((None, k, tn), w_map))
    args.append(w)
    rope_tiles = None
    if rope is not None:
        cos, sin_signed, n_q, n_k, seq = rope
        rope_tiles = (n_q // tn, n_k // tn)
        pos_blocks = seq // tm
        tab = pl.BlockSpec((tm, DIFF_HEAD_DIM), lambda i, j: (i % pos_blocks, 0))
        in_specs += [tab, tab]
        args += [cos, sin_signed]
    if tail is not None:
        in_specs.append(pl.BlockSpec((k, tn), lambda i, j: (0, jnp.maximum(j - main_tiles, 0))))
        args.append(tail)
        n_cols = n_cols + tail.shape[1]
    return pl.pallas_call(
        functools.partial(_norm_matmul_kernel, rope_tiles=rope_tiles, main_tiles=main_tiles,
                          prenormed=prenormed),
        grid=(m // tm, n_cols // tn),
        in_specs=in_specs,
        out_specs=pl.BlockSpec((tm, tn), lambda i, j: (i, j)),
        out_shape=jax.ShapeDtypeStruct((m, n_cols), out_dtype),
        scratch_shapes=[] if prenormed else [pltpu.VMEM((tm, k), BF16)],
        compiler_params=_cp(("parallel", "parallel" if prenormed else "arbitrary"), 58),
        name="norm_matmul",
    )(*args)


def _mem_kv_kernel(x_ref, g_ref, w_ref, o_ref):
    x = x_ref[...]
    ms = jnp.mean(x * x, axis=-1, keepdims=True)
    h = (x * lax.rsqrt(ms + RMS_EPS) * g_ref[0]).astype(BF16)
    o_ref[0] = jnp.dot(h, w_ref[0].astype(BF16), preferred_element_type=F32)


def memory_kv(mem2, norm_mem, w_mem_kv):
    rows, k = mem2.shape
    layers, _, n = w_mem_kv.shape
    tn = 512
    return pl.pallas_call(
        _mem_kv_kernel,
        grid=(layers, n // tn),
        in_specs=[
            pl.BlockSpec((rows, k), lambda l, j: (0, 0)),
            pl.BlockSpec((1, 1, k), lambda l, j: (l, 0, 0)),
            pl.BlockSpec((1, k, tn), lambda l, j: (l, 0, j)),
        ],
        out_specs=pl.BlockSpec((1, rows, tn), lambda l, j: (l, 0, j)),
        out_shape=jax.ShapeDtypeStruct((layers, rows, n), F32),
        compiler_params=_cp(("parallel", "parallel"), 48),
        name="memory_kv",
    )(mem2, norm_mem.reshape(layers, 1, k), w_mem_kv)


def _pool_kernel(u_ref, w_ref, sc_ref, o_ref, pad_ref):
    grp = pl.program_id(1)
    s, c = u_ref.shape[1], u_ref.shape[2]
    rows = 256
    pad_ref[0:16, :] = jnp.zeros((16, c), F32)
    pad_ref[16:, :] = u_ref[0].astype(F32)
    wb = w_ref[...].astype(BF16)
    sc = sc_ref[0]

    for gi, win in enumerate(POOL_WINDOWS):
        @pl.when(grp == gi)
        def _(win=win):
            for r in range(s // rows):
                xh = pad_ref[r * rows:r * rows + rows + 16, :]
                acc = xh
                k = 1
                while k < win:
                    acc = acc + pltpu.roll(acc, k, axis=0)
                    k *= 2
                t = r * rows + lax.broadcasted_iota(jnp.int32, (rows, 1), 0)
                cnt = jnp.minimum(t + 1, win).astype(F32)
                mixed = (acc[16:, :] / cnt - xh[16:, :]).astype(BF16)
                o_ref[0, r * rows:(r + 1) * rows, :] = (
                    jnp.dot(mixed, wb, preferred_element_type=F32) * sc).astype(o_ref.dtype)


def pool_mixer(proj3, w_grp, li, scale):
    b, s, _ = proj3.shape
    c = POOL_GROUP_DIM
    return pl.pallas_call(
        _pool_kernel,
        grid=(b, POOL_GROUPS),
        in_specs=[
            pl.BlockSpec((1, s, c), lambda i, g: (i, 0, g)),
            pl.BlockSpec((None, None, c, c), lambda i, g: (li, g, 0, 0)),
            pl.BlockSpec((1, 1, c), lambda i, g: (g, 0, 0)),
        ],
        out_specs=pl.BlockSpec((1, s, c), lambda i, g: (i, 0, g)),
        out_shape=jax.ShapeDtypeStruct((b, s, MIX_WIDTH), BF16),
        scratch_shapes=[pltpu.VMEM((s + 16, c), F32)],
        compiler_params=_cp(("parallel", "parallel"), 40),
        name="pool_mixer",
    )(proj3, w_grp, scale.reshape(POOL_GROUPS, 1, c))


def _diff_attn_kernel(lam_ref, sub_ref, q_ref, k_ref, v_ref, o_ref, *, lambda_init):
    lam = lam_ref[...]
    s1 = jnp.sum(lam[0:1] * lam[1:2], axis=-1, keepdims=True)
    s2 = jnp.sum(lam[2:3] * lam[3:4], axis=-1, keepdims=True)
    lmbda = jnp.exp(s1) - jnp.exp(s2) + lambda_init
    s = q_ref.shape[1]
    tq = 512
    d = DIFF_HEAD_DIM
    diag_mask = (lax.broadcasted_iota(jnp.int32, (tq, tq), 1)
                 <= lax.broadcasted_iota(jnp.int32, (tq, tq), 0))
    for i in range(s // tq):
        lo = i * tq
        q = q_ref[0, lo:lo + tq, :]
        outs = []
        for c in range(2):
            cols = slice(c * d, (c + 1) * d)
            sd = lax.dot_general(q[:, cols], k_ref[0, lo:lo + tq, cols], NT_DIMS,
                                 preferred_element_type=F32)
            sd = jnp.where(diag_mask, sd, -jnp.inf)
            top = jnp.max(sd, axis=-1, keepdims=True)
            if lo:
                sp = lax.dot_general(q[:, cols], k_ref[0, 0:lo, cols], NT_DIMS,
                                     preferred_element_type=F32)
                top = jnp.maximum(top, jnp.max(sp, axis=-1, keepdims=True))
            ed = jnp.exp(sd - top)
            pv = jnp.dot(ed.astype(BF16), v_ref[0, lo:lo + tq, :], preferred_element_type=F32)
            den = jnp.sum(ed, axis=-1, keepdims=True)
            if lo:
                ep = jnp.exp(sp - top)
                pv = pv + jnp.dot(ep.astype(BF16), v_ref[0, 0:lo, :], preferred_element_type=F32)
                den = den + jnp.sum(ep, axis=-1, keepdims=True)
            outs.append(pv / den)
        o = outs[0] - lmbda * outs[1]
        ms = jnp.mean(o * o, axis=-1, keepdims=True)
        o_ref[0, i * tq:(i + 1) * tq, :] = (
            o * lax.rsqrt(ms + RMS_EPS) * sub_ref[...] * (1.0 - lambda_init)).astype(o_ref.dtype)


def diff_attention(proj3, lam, subln, lambda_init):
    b, s, _ = proj3.shape
    vd = DIFF_V_DIM
    nh = DIFF_HEADS
    return pl.pallas_call(
        functools.partial(_diff_attn_kernel, lambda_init=lambda_init),
        grid=(b, nh),
        in_specs=[
            pl.BlockSpec((4, DIFF_HEAD_DIM), lambda i, h: (0, 0)),
            pl.BlockSpec((1, vd), lambda i, h: (0, 0)),
            pl.BlockSpec((1, s, vd), lambda i, h: (i, 0, h)),
            pl.BlockSpec((1, s, vd), lambda i, h: (i, 0, nh + h)),
            pl.BlockSpec((1, s, vd), lambda i, h: (i, 0, 2 * nh + h)),
        ],
        out_specs=pl.BlockSpec((1, s, vd), lambda i, h: (i, 0, h)),
        out_shape=jax.ShapeDtypeStruct((b, s, MIX_WIDTH), BF16),
        compiler_params=_cp(("parallel", "parallel"), 48),
        name="diff_attention",
    )(lam, subln.reshape(1, vd), proj3, proj3, proj3)


CONV_HALO = 8
SSD_STEP_CHUNKS = 4


def _causal_conv_silu(u_ref, halo_ref, w_ref, b_ref):
    cur = u_ref[0]
    ext = jnp.concatenate([halo_ref[...], cur], axis=0)
    w = w_ref[...]
    y = ext * w[SSD_CONV - 1:SSD_CONV]
    for j in range(1, SSD_CONV):
        y = y + pltpu.roll(ext, j, axis=0) * w[SSD_CONV - 1 - j:SSD_CONV - j]
    halo_ref[...] = cur[cur.shape[0] - CONV_HALO:, :]
    y = y[CONV_HALO:, :] + b_ref[...]
    return y * _sigmoid(y)


def _dot3(a_f32, b_bf16):
    return sum(jnp.dot(t, b_bf16, preferred_element_type=F32) for t in _split3(a_f32))


def _ssd_kernel(x_ref, b_ref, c_ref, z_ref, dt_ref, wx_ref, wb_ref, wc_ref, cbx_ref, cbb_ref, cbc_ref,
                bias_ref, alog_ref, dx_ref, ng_ref, o_ref,
                state_ref, y_ref, xs_ref, hx_ref, hb_ref, hc_ref):
    @pl.when(pl.program_id(2) == 0)
    def _():
        state_ref[...] = jnp.zeros(state_ref.shape, F32)
        hx_ref[...] = jnp.zeros(hx_ref.shape, F32)
        hb_ref[...] = jnp.zeros(hb_ref.shape, F32)
        hc_ref[...] = jnp.zeros(hc_ref.shape, F32)

    ln = SSD_CHUNK
    hg = SSD_HEADS_PER_GROUP
    lanes = 2 * SSD_HEAD_DIM
    xs_ref[...] = _causal_conv_silu(x_ref, hx_ref, wx_ref, cbx_ref)
    bm_blk = _causal_conv_silu(b_ref, hb_ref, wb_ref, cbb_ref)
    cm_blk = _causal_conv_silu(c_ref, hc_ref, wc_ref, cbc_ref).astype(BF16)
    row = lax.broadcasted_iota(jnp.int32, (ln, ln), 0)
    col = lax.broadcasted_iota(jnp.int32, (ln, ln), 1)
    causal = col <= row
    ones_lower = jnp.where(causal, 1.0, 0.0).astype(BF16)
    sel_r = lax.broadcasted_iota(jnp.int32, (lanes, hg * lanes), 0)
    sel_c = lax.broadcasted_iota(jnp.int32, (lanes, hg * lanes), 1)
    spread = jnp.where(jnp.right_shift(sel_c, lanes.bit_length() - 1) == sel_r,
                       1.0, 0.0).astype(BF16)
    low_half = lax.broadcasted_iota(jnp.int32, (ln, lanes), 1) < SSD_HEAD_DIM
    dt_blk = _softplus(dt_ref[0] + bias_ref[0])
    neg_a = -jnp.exp(alog_ref[0])

    for cc in range(x_ref.shape[1] // ln):
        rows = slice(cc * ln, (cc + 1) * ln)
        bm = bm_blk[rows]
        bm_t = bm.T.astype(BF16)
        cm = cm_blk[rows]
        dt = dt_blk[rows]
        acs = sum(jnp.dot(ones_lower, t, preferred_element_type=F32) for t in _split3(dt * neg_a))
        acs_rows = acs.T
        dt_all = _dot3(dt, spread)
        acs_all = _dot3(acs, spread)
        scores = lax.dot_general(cm, bm.astype(BF16), NT_DIMS, preferred_element_type=F32)
        for k in range(hg // 2):
            heads = (2 * k, 2 * k + 1)
            tile = lambda a, h: a[:, h * lanes:(h + 1) * lanes]
            x = xs_ref[rows, k * lanes:(k + 1) * lanes]
            dt_p = jnp.where(low_half, tile(dt_all, heads[0]), tile(dt_all, heads[1]))
            acs_p = jnp.where(low_half, tile(acs_all, heads[0]), tile(acs_all, heads[1]))
            xc = x * dt_p
            prev = state_ref[k]
            y = jnp.dot(cm, prev.astype(BF16), preferred_element_type=F32) * jnp.exp(acs_p)
            for h, own in zip(heads, (low_half, ~low_half)):
                decay = jnp.exp(
                    jnp.where(causal, tile(acs_all, h) - acs_rows[h:h + 1, :], -jnp.inf))
                y = y + jnp.dot((scores * decay).astype(BF16),
                                jnp.where(own, xc, 0.0).astype(BF16), preferred_element_type=F32)
            a_last = acs_p[ln - 1:ln, :]
            st = jnp.dot(bm_t, (xc * jnp.exp(a_last - acs_p)).astype(BF16),
                         preferred_element_type=F32)
            state_ref[k] = prev * jnp.exp(a_last) + st
            y_ref[rows, k * lanes:(k + 1) * lanes] = y

    z = z_ref[0]
    yz = (y_ref[...] + dx_ref[0] * xs_ref[...]) * (z * _sigmoid(z))
    ms = jnp.mean(yz * yz, axis=-1, keepdims=True)
    o_ref[0] = (yz * lax.rsqrt(ms + RMS_EPS) * ng_ref[0]).astype(o_ref.dtype)


def ssd_scan(proj3, dt_blk, conv_w, conv_b, dt_bias, a_log, d_skip, norm_g):
    b, s, _ = proj3.shape
    ln, g, hg, n = SSD_CHUNK, SSD_GROUPS, SSD_HEADS_PER_GROUP, SSD_STATE
    gw = hg * SSD_HEAD_DIM
    lanes = 2 * SSD_HEAD_DIM
    assert ln == lanes and n == lanes

    def head_lanes(v):
        v = v.reshape(v.shape[:-1] + (g, hg))
        v = jnp.pad(v, [(0, 0)] * (v.ndim - 1) + [(0, lanes - hg)])
        return v.reshape(v.shape[:-2] + (g * lanes,))

    per_group = pl.BlockSpec((1, 1, lanes), lambda i, j, c: (j, 0, 0))
    d_chan = jnp.repeat(d_skip, SSD_HEAD_DIM)
    x_blk, b_blk = MIX_WIDTH // gw, (2 * MIX_WIDTH) // n
    c_blk = b_blk + g
    cw_b, cw_c = MIX_WIDTH // n, MIX_WIDTH // n + g
    conv_b2 = conv_b.reshape(1, SSD_CONV_DIM)
    rows = SSD_STEP_CHUNKS * ln
    return pl.pallas_call(
        _ssd_kernel,
        grid=(b, g, s // rows),
        in_specs=[
            pl.BlockSpec((1, rows, gw), lambda i, j, c: (i, c, x_blk + j)),
            pl.BlockSpec((1, rows, n), lambda i, j, c: (i, c, b_blk + j)),
            pl.BlockSpec((1, rows, n), lambda i, j, c: (i, c, c_blk + j)),
            pl.BlockSpec((1, rows, gw), lambda i, j, c: (i, c, j)),
            pl.BlockSpec((1, rows, lanes), lambda i, j, c: (i, c, dt_blk + j)),
            pl.BlockSpec((SSD_CONV, gw), lambda i, j, c: (0, j)),
            pl.BlockSpec((SSD_CONV, n), lambda i, j, c: (0, cw_b + j)),
            pl.BlockSpec((SSD_CONV, n), lambda i, j, c: (0, cw_c + j)),
            pl.BlockSpec((1, gw), lambda i, j, c: (0, j)),
            pl.BlockSpec((1, n), lambda i, j, c: (0, cw_b + j)),
            pl.BlockSpec((1, n), lambda i, j, c: (0, cw_c + j)),
            per_group, per_group,
            pl.BlockSpec((1, 1, gw), lambda i, j, c: (j, 0, 0)),
            pl.BlockSpec((1, 1, gw), lambda i, j, c: (j, 0, 0)),
        ],
        out_specs=pl.BlockSpec((1, rows, gw), lambda i, j, c: (i, c, j)),
        out_shape=jax.ShapeDtypeStruct((b, s, MIX_WIDTH), BF16),
        scratch_shapes=[pltpu.VMEM((hg // 2, n, lanes), F32), pltpu.VMEM((rows, gw), F32),
                        pltpu.VMEM((rows, gw), F32), pltpu.VMEM((CONV_HALO, gw), F32),
                        pltpu.VMEM((CONV_HALO, n), F32), pltpu.VMEM((CONV_HALO, n), F32)],
        compiler_params=_cp(("parallel", "parallel", "arbitrary"), 32),
        name="ssd_scan",
    )(proj3, proj3, proj3, proj3, proj3, conv_w, conv_w, conv_w,
      conv_b2, conv_b2, conv_b2, head_lanes(dt_bias).reshape(g, 1, lanes),
      head_lanes(a_log).reshape(g, 1, lanes), d_chan.reshape(g, 1, gw), norm_g.reshape(g, 1, gw))


def _memory_attention(q_ref, kv_ref, o_ref):
    d = MEM_HEAD_DIM
    for h in range(MEM_HEADS):
        q = (q_ref[0, :, h * d:(h + 1) * d].astype(F32) * (d ** -0.5)).astype(BF16)
        k = kv_ref[0, :, h * d:(h + 1) * d].astype(BF16)
        v = kv_ref[0, :, MEM_WIDTH + h * d:MEM_WIDTH + (h + 1) * d].astype(BF16)
        sc = lax.dot_general(q, k, NT_DIMS, preferred_element_type=F32)
        e = jnp.exp(sc - jnp.max(sc, axis=-1, keepdims=True))
        pr = (e / jnp.sum(e, axis=-1, keepdims=True)).astype(BF16)
        o_ref[:, h * d:(h + 1) * d] = jnp.dot(
            pr, v, preferred_element_type=F32).astype(o_ref.dtype)


def _outproj_kernel(x_ref, a_ref, q_ref, kv_ref, wa_ref, wm_ref, o_ref, mo_ref):
    @pl.when(pl.program_id(1) == 0)
    def _():
        _memory_attention(q_ref, kv_ref, mo_ref)

    acc = jnp.dot(a_ref[...], wa_ref[...].astype(BF16), preferred_element_type=F32)
    acc = acc + jnp.dot(mo_ref[...], wm_ref[...].astype(BF16), preferred_element_type=F32)
    o_ref[...] = x_ref[...] + acc


def out_projection(x, mix, proj3, q_block, mem_kv, w_out, li):
    m, d = x.shape
    b, s, _ = proj3.shape
    mem_len = mem_kv.shape[1]
    tm, tn = s, 512
    return pl.pallas_call(
        _outproj_kernel,
        grid=(m // tm, d // tn),
        in_specs=[
            pl.BlockSpec((tm, tn), lambda i, j: (i, j)),
            pl.BlockSpec((tm, MIX_WIDTH), lambda i, j: (i, 0)),
            pl.BlockSpec((1, s, MEM_WIDTH), lambda i, j: (i, 0, q_block)),
            pl.BlockSpec((1, mem_len, 2 * MEM_WIDTH), lambda i, j: (li * b + i, 0, 0)),
            pl.BlockSpec((None, MIX_WIDTH, tn), lambda i, j: (li, 0, j)),
            pl.BlockSpec((None, MEM_WIDTH, tn), lambda i, j: (li, MIX_WIDTH // MEM_WIDTH, j)),
        ],
        out_specs=pl.BlockSpec((tm, tn), lambda i, j: (i, j)),
        out_shape=jax.ShapeDtypeStruct((m, d), F32),
        scratch_shapes=[pltpu.VMEM((tm, MEM_WIDTH), BF16)],
        compiler_params=_cp(("parallel", "arbitrary"), 58),
        name="out_projection",
    )(x, mix, proj3, mem_kv, w_out, w_out)


def _router_kernel(x_ref, g_ref, wr_ref, br_ref, r_ref, seg_ref, t_ref):
    x = x_ref[...]
    ms = jnp.mean(x * x, axis=-1, keepdims=True)
    t = x * lax.rsqrt(ms + RMS_EPS) * g_ref[...]
    th = t.astype(BF16)
    t_ref[...] = th
    tl = (t - th.astype(F32)).astype(BF16)
    w = wr_ref[...]
    wh = w.astype(BF16)
    wl = (w - wh.astype(F32)).astype(BF16)
    lg = (lax.dot_general(wh, th, NT_DIMS, preferred_element_type=F32)
          + lax.dot_general(wh, tl, NT_DIMS, preferred_element_type=F32)
          + lax.dot_general(wl, th, NT_DIMS, preferred_element_type=F32)) + br_ref[...]
    ng, epg = N_EXPERT_GROUPS, EXPERTS_PER_GROUP
    gl = [lg[j:j + 1] for j in range(ng)]
    el = [lg[ng + j:ng + j + 1] for j in range(N_EXPERTS)]

    def first_argmax(vals):
        top = functools.reduce(jnp.maximum, vals)
        idx = jnp.full(top.shape, len(vals) - 1, jnp.int32)
        for j in range(len(vals) - 2, -1, -1):
            idx = jnp.where(vals[j] >= top, j, idx)
        return top, idx

    gmax, gsel = first_argmax(gl)
    g_w = 1.0 / functools.reduce(lambda a, b: a + b, [jnp.exp(v - gmax) for v in gl])
    e_in = []
    for j in range(epg):
        v = el[(ng - 1) * epg + j]
        for gi in range(ng - 2, -1, -1):
            v = jnp.where(gsel == gi, el[gi * epg + j], v)
        e_in.append(v)
    emax = functools.reduce(jnp.maximum, e_in)
    pe = [jnp.exp(v - emax) for v in e_in]
    se = functools.reduce(lambda a, b: a + b, pe)
    prob = [v / se for v in pe]
    v1, i1 = first_argmax(prob)
    rest = [jnp.where(i1 == j, -1.0, prob[j]) for j in range(epg)]
    v2, i2 = first_argmax(rest)
    tot = v1 + v2
    e1 = gsel * epg + i1
    e2 = gsel * epg + i2

    tm = x.shape[0]
    eidx = lax.broadcasted_iota(jnp.int32, (N_EXPERTS, tm), 0)
    hit1 = eidx == e1
    hit2 = eidx == e2
    onehot = jnp.where(hit1 | hit2, 1.0, 0.0)
    tok_r = lax.broadcasted_iota(jnp.int32, (tm, tm), 0)
    tok_c = lax.broadcasted_iota(jnp.int32, (tm, tm), 1)
    earlier = jnp.where(tok_r < tok_c, 1.0, 0.0).astype(BF16)
    rank = jnp.dot(onehot.astype(BF16), earlier, preferred_element_type=F32)
    cnt = jnp.sum(onehot, axis=1, keepdims=True).astype(jnp.int32)
    seg = jnp.bitwise_and(cnt + (SEG_ALIGN - 1), -SEG_ALIGN)
    ex_r = lax.broadcasted_iota(jnp.int32, (N_EXPERTS, N_EXPERTS), 0)
    ex_c = lax.broadcasted_iota(jnp.int32, (N_EXPERTS, N_EXPERTS), 1)
    lower = jnp.where(ex_c < ex_r, 1.0, 0.0).astype(BF16)
    seg_lanes = jnp.broadcast_to(seg.astype(F32), (N_EXPERTS, 128))
    start = jnp.dot(lower, seg_lanes.astype(BF16), preferred_element_type=F32)[:, 0:1]
    slot = start + rank
    lpos1 = jnp.sum(jnp.where(hit1, slot, 0.0), axis=0, keepdims=True)
    lpos2 = jnp.sum(jnp.where(hit2, slot, 0.0), axis=0, keepdims=True)
    r_ref[...] = jnp.concatenate(
        [e1.astype(F32), e2.astype(F32), v1 / tot * g_w, v2 / tot * g_w, lpos1, lpos2,
         jnp.zeros((2, tm), F32)], axis=0)
    seg_ref[0] = jnp.broadcast_to(seg, (N_EXPERTS, 128))


def moe_router(x, g, w_group, b_group, w_expert, b_expert):
    m, d = x.shape
    tm = MOE_SUB
    nr = 32
    wr = jnp.zeros((nr, d), F32).at[:N_EXPERT_GROUPS].set(w_group.T)
    wr = wr.at[N_EXPERT_GROUPS:N_EXPERT_GROUPS + N_EXPERTS].set(w_expert.T)
    br = jnp.zeros((nr, 1), F32).at[:N_EXPERT_GROUPS, 0].set(b_group)
    br = br.at[N_EXPERT_GROUPS:N_EXPERT_GROUPS + N_EXPERTS, 0].set(b_expert)
    route, seg3, t = pl.pallas_call(
        _router_kernel,
        grid=(m // tm,),
        in_specs=[
            pl.BlockSpec((tm, d), lambda i: (i, 0)),
            pl.BlockSpec((1, d), lambda i: (0, 0)),
            pl.BlockSpec((nr, d), lambda i: (0, 0)),
            pl.BlockSpec((nr, 1), lambda i: (0, 0)),
        ],
        out_specs=[pl.BlockSpec((8, tm), lambda i: (0, i)),
                   pl.BlockSpec((1, N_EXPERTS, 128), lambda i: (i, 0, 0)),
                   pl.BlockSpec((tm, d), lambda i: (i, 0))],
        out_shape=[jax.ShapeDtypeStruct((8, m), F32),
                   jax.ShapeDtypeStruct((m // tm, N_EXPERTS, 128), jnp.int32),
                   jax.ShapeDtypeStruct((m, d), BF16)],
        compiler_params=_cp(("parallel",), 40),
        name="moe_router",
    )(x, g.reshape(1, d), wr, br)
    return route, seg3[:, :, 0], t


def moe_plan(seg, m):
    nsub = m // MOE_SUB
    tm = MOE_TILE
    loc_off = jnp.cumsum(seg, axis=1) - seg
    reg_off = jnp.cumsum(seg, axis=0) - seg
    length = jnp.sum(seg, axis=0)
    padded = (length + tm - 1) // tm * tm
    e_end = jnp.cumsum(padded)
    e_start = e_end - padded
    glob_off = e_start[None, :] + reg_off
    n_slots = (2 * m + nsub * N_EXPERTS * (SEG_ALIGN - 1) + N_EXPERTS * (tm - 1) + tm - 1) // tm * tm
    tile_start = jnp.arange(n_slots // tm, dtype=jnp.int32) * tm
    tile_expert = jnp.minimum(
        jnp.sum((tile_start[:, None] >= e_end[None, :]).astype(jnp.int32), axis=1), N_EXPERTS - 1)
    ar = jnp.arange(N_EXPERTS, dtype=jnp.int32)
    later = (ar[None, :] > ar[:, None]) & (padded[None, :] > 0)
    nxt = jnp.min(jnp.where(later, ar[None, :], N_EXPERTS), axis=1)
    nxt = jnp.where(nxt == N_EXPERTS, -1, nxt).astype(jnp.int32)
    tile_next = jnp.sum(jnp.where(tile_expert[:, None] == ar[None, :], nxt[None, :], 0), axis=1)
    return dict(seg=seg.reshape(-1), loc_off=loc_off.reshape(-1), glob_off=glob_off.reshape(-1),
                n_slots=n_slots, tile_expert=tile_expert, tile_next=tile_next,
                n_used=(e_end[-1:] // tm).astype(jnp.int32),
                pad_start=jnp.concatenate([e_start + length, e_end[-1:]]).astype(jnp.int32),
                pad_len=jnp.concatenate([padded - length, n_slots - e_end[-1:]]).astype(jnp.int32))


def _aligned(v):
    return v if isinstance(v, int) else pl.multiple_of(v, SEG_ALIGN)


def _seg_copy(src_ref, dst_ref, sem, src_row, dst_row, n_rows):
    n_rows = _aligned(n_rows)
    return pltpu.make_async_copy(src_ref.at[pl.ds(_aligned(src_row), n_rows)],
                                 dst_ref.at[pl.ds(_aligned(dst_row), n_rows)], sem)


def _segment_copies(step, seg_ref, loc_ref, glob_ref, make):
    out = []
    for e in range(N_EXPERTS):
        k = step * N_EXPERTS + e
        out.append((seg_ref[k], make(loc_ref[k], glob_ref[k], seg_ref[k])))
    return out


def _start_all(copies):
    for n, cp in copies:
        @pl.when(n > 0)
        def _(cp=cp):
            cp.start()


def _wait_all(copies):
    for n, cp in copies:
        @pl.when(n > 0)
        def _(cp=cp):
            cp.wait()


def _dispatch_kernel(seg_ref, loc_ref, glob_ref, pst_ref, pln_ref, t_ref, rt_ref,
                     xs_ref, cbuf_ref, zbuf_ref, sems):
    s = pl.program_id(0)
    n_steps = pl.num_programs(0)
    buf = s % 2
    d = t_ref.shape[1]

    def copies(step, b):
        return _segment_copies(
            step, seg_ref, loc_ref, glob_ref,
            lambda loc, glob, n: _seg_copy(cbuf_ref.at[b], xs_ref, sems.at[b], loc, glob, n))

    @pl.when(s >= 2)
    def _():
        _wait_all(copies(s - 2, buf))

    t = t_ref[...]
    rt = rt_ref[...]
    lp = rt[4:6].astype(jnp.int32)
    slot = lax.broadcasted_iota(jnp.int32, (MOE_CBUF, MOE_SUB), 0)
    hit0 = slot == lp[0:1]
    hit1 = slot == lp[1:2]
    onehot = jnp.where(hit0 | hit1, 1.0, 0.0).astype(BF16)
    cbuf_ref[buf, :, 0:d] = jnp.dot(onehot, t, preferred_element_type=F32).astype(BF16)
    wslot = jnp.sum(jnp.where(hit0, rt[2:3], 0.0) + jnp.where(hit1, rt[3:4], 0.0),
                    axis=1, keepdims=True)
    w_hi = wslot.astype(BF16)
    w_lo = (wslot - w_hi.astype(F32)).astype(BF16)
    half = MOE_WCOLS // 2
    cbuf_ref[buf, :, d:d + half] = jnp.broadcast_to(w_hi, (MOE_CBUF, half))
    cbuf_ref[buf, :, d + half:] = jnp.broadcast_to(w_lo, (MOE_CBUF, half))
    _start_all(copies(s, buf))

    @pl.when(s == 0)
    def _():
        zsem = sems.at[2]
        zbuf_ref[...] = jnp.zeros(zbuf_ref.shape, BF16)
        pads = [(pln_ref[e], _seg_copy(zbuf_ref, xs_ref, zsem, 0, pst_ref[e], pln_ref[e]))
                for e in range(N_EXPERTS)]
        _start_all(pads)
        _wait_all(pads)
        tail_tiles = pln_ref[N_EXPERTS] // MOE_TILE

        def tail_copy(k):
            return _seg_copy(zbuf_ref, xs_ref, zsem, 0, pst_ref[N_EXPERTS] + k * MOE_TILE, MOE_TILE)

        def start(k, carry):
            tail_copy(k).start()
            return carry

        def wait(k, carry):
            tail_copy(k).wait()
            return carry
        lax.fori_loop(0, tail_tiles, start, 0)
        lax.fori_loop(0, tail_tiles, wait, 0)

    @pl.when(s == n_steps - 1)
    def _():
        @pl.when(s >= 1)
        def _():
            _wait_all(copies(s - 1, 1 - buf))
        _wait_all(copies(s, buf))


def moe_dispatch(t, route, plan):
    m, d = t.shape
    dw = d + MOE_WCOLS
    return pl.pallas_call(
        _dispatch_kernel,
        grid_spec=pltpu.PrefetchScalarGridSpec(
            num_scalar_prefetch=5,
            grid=(m // MOE_SUB,),
            in_specs=[
                pl.BlockSpec((MOE_SUB, d), lambda i, *_: (i, 0)),
                pl.BlockSpec((8, MOE_SUB), lambda i, *_: (0, i)),
            ],
            out_specs=pl.BlockSpec(memory_space=pl.ANY),
            scratch_shapes=[pltpu.VMEM((2, MOE_CBUF, dw), BF16), pltpu.VMEM((MOE_TILE, dw), BF16),
                            pltpu.SemaphoreType.DMA((3,))],
        ),
        out_shape=jax.ShapeDtypeStruct((plan["n_slots"], dw), BF16),
        compiler_params=pltpu.CompilerParams(dimension_semantics=("arbitrary",),
                                             vmem_limit_bytes=40 * MIB, has_side_effects=True),
        name="moe_dispatch",
    )(plan["seg"], plan["loc_off"], plan["glob_off"], plan["pad_start"], plan["pad_len"],
      t, route)


def _ffn_kernel(te_ref, nx_ref, nu_ref, x_ref, wg_hbm, wu_hbm, wd_hbm, o_ref,
                wgs_ref, wus_ref, wds_ref, wgb_ref, wub_ref, wdb_ref, sem, *, li):
    i = pl.program_id(0)
    d = o_ref.shape[1]

    def weight_copies(e):
        return [pltpu.make_async_copy(wg_hbm.at[li, e], wgs_ref, sem),
                pltpu.make_async_copy(wu_hbm.at[li, e], wus_ref, sem),
                pltpu.make_async_copy(wd_hbm.at[li, e], wds_ref, sem)]

    @pl.when(i < nu_ref[0])
    def _():
        e = te_ref[i]

        @pl.when(i == 0)
        def _():
            for cp in weight_copies(e):
                cp.start()

        @pl.when((i == 0) | (e != te_ref[jnp.maximum(i - 1, 0)]))
        def _():
            for cp in weight_copies(e):
                cp.wait()
            wgb_ref[...] = wgs_ref[...].astype(BF16)
            wub_ref[...] = wus_ref[...].astype(BF16)
            wdb_ref[...] = wds_ref[...].astype(BF16)

            @pl.when(nx_ref[i] >= 0)
            def _():
                for cp in weight_copies(nx_ref[i]):
                    cp.start()

        x = x_ref[:, 0:d]
        half = MOE_WCOLS // 2
        w = x_ref[:, d:d + 1].astype(F32) + x_ref[:, d + half:d + half + 1].astype(F32)
        gate = jnp.dot(x, wgb_ref[...], preferred_element_type=F32)
        up = jnp.dot(x, wub_ref[...], preferred_element_type=F32)
        hid = (gate * _sigmoid(gate) * up * w).astype(BF16)
        o_ref[...] = jnp.dot(hid, wdb_ref[...], preferred_element_type=F32).astype(o_ref.dtype)

    @pl.when(i >= nu_ref[0])
    def _():
        o_ref[...] = jnp.zeros(o_ref.shape, o_ref.dtype)


def grouped_ffn(xs, plan, w_gate, w_up, w_down, li):
    p, dw = xs.shape
    d = dw - MOE_WCOLS
    f = D_EXPERT
    tm = MOE_TILE
    hbm = pl.BlockSpec(memory_space=pl.ANY)
    return pl.pallas_call(
        functools.partial(_ffn_kernel, li=li),
        grid_spec=pltpu.PrefetchScalarGridSpec(
            num_scalar_prefetch=3,
            grid=(p // tm,),
            in_specs=[
                pl.BlockSpec((tm, dw), lambda i, te, nx, nu: (jnp.minimum(i, nu[0] - 1), 0)),
                hbm, hbm, hbm,
            ],
            out_specs=pl.BlockSpec((tm, d), lambda i, te, nx, nu: (i, 0)),
            scratch_shapes=[pltpu.VMEM((d, f), F32), pltpu.VMEM((d, f), F32), pltpu.VMEM((f, d), F32),
                            pltpu.VMEM((d, f), BF16), pltpu.VMEM((d, f), BF16),
                            pltpu.VMEM((f, d), BF16), pltpu.SemaphoreType.DMA(())],
        ),
        out_shape=jax.ShapeDtypeStruct((p, d), BF16),
        compiler_params=_cp(("arbitrary",), 48),
        name="grouped_ffn",
    )(plan["tile_expert"], plan["tile_next"], plan["n_used"], xs, w_gate, w_up, w_down)


def _combine_kernel(seg_ref, loc_ref, glob_ref, x_ref, lp_ref, ys_ref, g_ref, *rest, final_norm):
    if final_norm:
        o_ref, ybuf_ref, sems = rest
    else:
        o_ref, h_ref, ybuf_ref, sems = rest
    s = pl.program_id(0)
    n_steps = pl.num_programs(0)
    buf = s % 2

    def copies(step, b):
        return _segment_copies(
            step, seg_ref, loc_ref, glob_ref,
            lambda loc, glob, n: _seg_copy(ys_ref, ybuf_ref.at[b], sems.at[b], glob, loc, n))

    def fetch(step, b):
        ybuf_ref[b] = jnp.zeros(ybuf_ref.shape[1:], BF16)
        _start_all(copies(step, b))

    @pl.when(s == 0)
    def _():
        fetch(s, buf)

    @pl.when(s + 1 < n_steps)
    def _():
        fetch(s + 1, 1 - buf)

    _wait_all(copies(s, buf))
    lp = lp_ref[...]
    slot = lax.broadcasted_iota(jnp.int32, (MOE_SUB, MOE_CBUF), 1)
    onehot = jnp.where((slot == lp[:, 0:1]) | (slot == lp[:, 1:2]), 1.0, 0.0).astype(BF16)
    out = x_ref[...] + jnp.dot(onehot, ybuf_ref[buf], preferred_element_type=F32)
    ms = jnp.mean(out * out, axis=-1, keepdims=True)
    normed = out * lax.rsqrt(ms + RMS_EPS) * g_ref[...]
    if final_norm:
        o_ref[...] = normed
    else:
        o_ref[...] = out
        h_ref[...] = normed.astype(BF16)


def moe_combine(x, ys, route, plan, gain, final_norm):
    m, d = x.shape
    row = pl.BlockSpec((MOE_SUB, d), lambda i, *_: (i, 0))
    out_specs, out_shape = row, jax.ShapeDtypeStruct((m, d), F32)
    if not final_norm:
        out_specs, out_shape = [row, row], [out_shape, jax.ShapeDtypeStruct((m, d), BF16)]
    return pl.pallas_call(
        functools.partial(_combine_kernel, final_norm=final_norm),
        grid_spec=pltpu.PrefetchScalarGridSpec(
            num_scalar_prefetch=3,
            grid=(m // MOE_SUB,),
            in_specs=[
                row,
                pl.BlockSpec((MOE_SUB, 2), lambda i, *_: (i, 0)),
                pl.BlockSpec(memory_space=pl.ANY),
                pl.BlockSpec((1, d), lambda i, *_: (0, 0)),
            ],
            out_specs=out_specs,
            scratch_shapes=[pltpu.VMEM((2, MOE_CBUF, d), BF16), pltpu.SemaphoreType.DMA((2,))],
        ),
        out_shape=out_shape,
        compiler_params=pltpu.CompilerParams(dimension_semantics=("arbitrary",),
                                             vmem_limit_bytes=40 * MIB),
        name="moe_combine",
    )(plan["seg"], plan["loc_off"], plan["glob_off"], x, route[4:6].T.astype(jnp.int32), ys,
      gain.reshape(1, d))


def hier_moe_layer(x, norm_g, w_group, b_group, w_expert, b_expert, w_gate, w_up, w_down, li,
                   out_gain, final_norm):
    m, _ = x.shape
    route, seg, t = moe_router(x, norm_g, w_group, b_group, w_expert, b_expert)
    plan = moe_plan(seg, m)
    xs = moe_dispatch(t, route, plan)
    ys = grouped_ffn(xs, plan, w_gate, w_up, w_down, li)
    return moe_combine(x, ys, route, plan, out_gain, final_norm)


def _rope_tables(seq, dim):
    pos = jnp.arange(seq, dtype=F32)
    inv = ROPE_THETA ** (-jnp.arange(0, dim, 2, dtype=F32) / dim)
    ang = pos[:, None] * inv[None, :]
    ang = jnp.concatenate([ang, ang], axis=-1)
    sign = jnp.concatenate([-jnp.ones((dim // 2,), F32), jnp.ones((dim // 2,), F32)])
    return jnp.cos(ang), jnp.sin(ang) * sign[None, :]


def _lambda_init(depth_idx):
    return 0.8 - 0.6 * math.exp(-0.3 * depth_idx)


def kernel(x, mem, norm_mix, norm_mem, norm_ffn, norm_final, w_out, w_mem_kv, pool_w_in, pool_w_grp, pool_scale, diff_w_in, diff_lambda, diff_subln, ssd_w_in, ssd_conv_w, ssd_conv_b, ssd_dt_bias, ssd_a_log, ssd_d, ssd_norm, moe_w_group, moe_b_group, moe_w_expert, moe_b_expert, moe_w_gate, moe_w_up, moe_w_down):
    b, s, d = x.shape
    m = b * s
    mem_len = mem.shape[1]
    cos, sin_signed = _rope_tables(s, DIFF_HEAD_DIM)
    xt = x.reshape(m, d)
    mem_kv_all = memory_kv(mem.reshape(b * mem_len, d), norm_mem, w_mem_kv)
    mem_kv_all = mem_kv_all.reshape(DEPTH * b, mem_len, 2 * MEM_WIDTH)
    for i in range(DEPTH):
        kind, slot = i % N_MIXERS, i // N_MIXERS
        if i == 0:
            lhs, gain, tn = xt, norm_mix[i], IN_TN
        else:
            lhs, gain, tn = h_next, None, IN_TN_PRENORMED
        if kind == 0:
            n_in = MIX_WIDTH + MEM_WIDTH
            proj = norm_matmul(lhs, gain, pool_w_in, slot, n_in, IN_TM, tn, BF16)
            proj = proj.reshape(b, s, n_in)
            mix = pool_mixer(proj, pool_w_grp, slot, pool_scale[slot])
        elif kind == 1:
            n_in = 3 * MIX_WIDTH + MEM_WIDTH
            proj = norm_matmul(lhs, gain, diff_w_in, slot, n_in, IN_TM, tn, BF16,
                               rope=(cos, sin_signed, MIX_WIDTH, MIX_WIDTH, s))
            proj = proj.reshape(b, s, n_in)
            mix = diff_attention(proj, diff_lambda[slot], diff_subln[slot], _lambda_init(i))
        else:
            n_main = MIX_WIDTH + SSD_CONV_DIM
            lanes = 2 * SSD_HEAD_DIM
            w_dt = ssd_w_in[slot, :, n_main:SSD_MIX_IN].reshape(d, SSD_GROUPS, SSD_HEADS_PER_GROUP)
            w_dt = jnp.pad(w_dt, ((0, 0), (0, 0), (0, lanes - SSD_HEADS_PER_GROUP)))
            w_tail = jnp.concatenate(
                [ssd_w_in[slot, :, SSD_MIX_IN:], w_dt.reshape(d, SSD_GROUPS * lanes)], axis=1)
            n_in = n_main + SSD_TAIL
            proj = norm_matmul(lhs, gain, ssd_w_in, slot, n_main, IN_TM, tn, F32, tail=w_tail)
            proj = proj.reshape(b, s, n_in)
            mix = ssd_scan(proj, (n_main + MEM_WIDTH) // lanes, ssd_conv_w[slot], ssd_conv_b[slot],
                           ssd_dt_bias[slot], ssd_a_log[slot], ssd_d[slot], ssd_norm[slot])
        q_block = (MIX_WIDTH if kind == 0 else 3 * MIX_WIDTH if kind == 1
                   else MIX_WIDTH + SSD_CONV_DIM) // MEM_WIDTH
        xt = out_projection(xt, mix.reshape(m, MIX_WIDTH), proj, q_block, mem_kv_all, w_out, i)
        last = i == DEPTH - 1
        res = hier_moe_layer(xt, norm_ffn[i], moe_w_group[i], moe_b_group[i], moe_w_expert[i],
                             moe_b_expert[i], moe_w_gate, moe_w_up, moe_w_down, i,
                             norm_final if last else norm_mix[i + 1], last)
        if last:
            xt = res
        else:
            xt, h_next = res
    return xt.reshape(b, s, d)
```

```python
import functools
import math

import jax
import jax.numpy as jnp
from jax import lax
from jax.experimental import pallas as pl
from jax.experimental.pallas import tpu as pltpu

F32 = jnp.float32
BF16 = jnp.bfloat16

DEPTH = 4
N_MIXERS = 3
MIX_WIDTH = 1536
MEM_WIDTH = 512
MEM_HEADS = 4
MEM_HEAD_DIM = 128
POOL_WINDOWS = (2, 4, 8, 16)
POOL_GROUPS = 4
POOL_GROUP_DIM = 384
DIFF_HEAD_DIM = 128
DIFF_HEADS = 6
DIFF_V_DIM = 256
ROPE_THETA = 10000.0
SSD_HEAD_DIM = 64
SSD_GROUPS = 4
SSD_HEADS_PER_GROUP = 6
SSD_STATE = 128
SSD_CONV = 4
SSD_CHUNK = 128
SSD_CONV_DIM = 2560
SSD_MIX_IN = 4120
SSD_TAIL = 1024
N_EXPERT_GROUPS = 4
EXPERTS_PER_GROUP = 4
N_EXPERTS = 16
D_EXPERT = 512
RMS_EPS = 1e-6

IN_TM, IN_TN = 2048, 256
IN_TN_PRENORMED = 512
MOE_TILE = 512
MOE_SUB = 512
SEG_ALIGN = 16
MOE_CBUF = 2 * MOE_SUB + 256
MOE_WCOLS = 256
MIB = 1024 * 1024

NT_DIMS = (((1,), (1,)), ((), ()))


def _cp(sem, vmem_mib):
    return pltpu.CompilerParams(dimension_semantics=sem, vmem_limit_bytes=vmem_mib * MIB)


def _sigmoid(x):
    return 1.0 / (1.0 + jnp.exp(-x))


def _softplus(x):
    return jnp.maximum(x, 0.0) + jnp.log1p(jnp.exp(-jnp.abs(x)))


def _split3(v):
    hi = v.astype(BF16)
    r = v - hi.astype(F32)
    mid = r.astype(BF16)
    lo = (r - mid.astype(F32)).astype(BF16)
    return hi, mid, lo


def _norm_matmul_kernel(*refs, rope_tiles, main_tiles, prenormed):
    refs = list(refs)
    x_ref = refs.pop(0)
    g_ref = None if prenormed else refs.pop(0)
    w_ref = refs.pop(0)
    if rope_tiles is not None:
        cos_ref, sin_ref = refs.pop(0), refs.pop(0)
    if main_tiles is not None:
        wt_ref = refs.pop(0)
    o_ref = refs.pop(0)
    j = pl.program_id(1)

    if prenormed:
        h_ref = x_ref
    else:
        h_ref = refs.pop(0)

        @pl.when(j == 0)
        def _():
            x = x_ref[...]
            ms = jnp.mean(x * x, axis=-1, keepdims=True)
            h_ref[...] = (x * lax.rsqrt(ms + RMS_EPS) * g_ref[...]).astype(BF16)

    def product(weights_ref):
        return jnp.dot(h_ref[...], weights_ref[...].astype(BF16), preferred_element_type=F32)

    if main_tiles is not None:
        @pl.when(j < main_tiles)
        def _():
            o_ref[...] = product(w_ref).astype(o_ref.dtype)

        @pl.when(j >= main_tiles)
        def _():
            o_ref[...] = product(wt_ref).astype(o_ref.dtype)
        return
    acc = product(w_ref)
    if rope_tiles is None:
        o_ref[...] = acc.astype(o_ref.dtype)
        return
    n_q, n_k = rope_tiles
    hd = DIFF_HEAD_DIM

    @pl.when(j < n_q + n_k)
    def _():
        scale = jnp.where(j < n_q, hd ** -0.5, 1.0)
        cos = cos_ref[...]
        sin = sin_ref[...]
        for c in range(o_ref.shape[1] // hd):
            x = acc[:, c * hd:(c + 1) * hd]
            r = x * cos + pltpu.roll(x, hd // 2, axis=1) * sin
            o_ref[:, c * hd:(c + 1) * hd] = (r * scale).astype(o_ref.dtype)

    @pl.when(j >= n_q + n_k)
    def _():
        o_ref[...] = acc.astype(o_ref.dtype)


def norm_matmul(x, g, w, li, n_cols, tm, tn, out_dtype, rope=None, tail=None):
    m, k = x.shape
    prenormed = g is None
    main_tiles = None if tail is None else n_cols // tn
    w_map = ((lambda i, j: (li, 0, j)) if tail is None
             else (lambda i, j: (li, 0, jnp.minimum(j, main_tiles - 1))))
    in_specs = [pl.BlockSpec((tm, k), lambda i, j: (i, 0))]
    args = [x]
    if not prenormed:
        in_specs.append(pl.BlockSpec((1, k), lambda i, j: (0, 0)))
        args.append(g.reshape(1, k))
    in_specs.append(pl.BlockSpec((None, k, tn), w_map))
    args.append(w)
    rope_tiles = None
    if rope is not None:
        cos, sin_signed, n_q, n_k, seq = rope
        rope_tiles = (n_q // tn, n_k // tn)
        pos_blocks = seq // tm
        tab = pl.BlockSpec((tm, DIFF_HEAD_DIM), lambda i, j: (i % pos_blocks, 0))
        in_specs += [tab, tab]
        args += [cos, sin_signed]
    if tail is not None:
        in_specs.append(pl.BlockSpec((k, tn), lambda i, j: (0, jnp.maximum(j - main_tiles, 0))))
        args.append(tail)
        n_cols = n_cols + tail.shape[1]
    return pl.pallas_call(
        functools.partial(_norm_matmul_kernel, rope_tiles=rope_tiles, main_tiles=main_tiles,
                          prenormed=prenormed),
        grid=(m // tm, n_cols // tn),
        in_specs=in_specs,
        out_specs=pl.BlockSpec((tm, tn), lambda i, j: (i, j)),
        out_shape=jax.ShapeDtypeStruct((m, n_cols), out_dtype),
        scratch_shapes=[] if prenormed else [pltpu.VMEM((tm, k), BF16)],
        compiler_params=_cp(("parallel", "parallel" if prenormed else "arbitrary"), 58),
        name="norm_matmul",
    )(*args)


def _mem_kv_kernel(x_ref, g_ref, w_ref, o_ref):
    x = x_ref[...]
    ms = jnp.mean(x * x, axis=-1, keepdims=True)
    h = (x * lax.rsqrt(ms + RMS_EPS) * g_ref[0]).astype(BF16)
    o_ref[0] = jnp.dot(h, w_ref[0].astype(BF16), preferred_element_type=F32)


def memory_kv(mem2, norm_mem, w_mem_kv):
    rows, k = mem2.shape
    layers, _, n = w_mem_kv.shape
    tn = 512
    return pl.pallas_call(
        _mem_kv_kernel,
        grid=(layers, n // tn),
        in_specs=[
            pl.BlockSpec((rows, k), lambda l, j: (0, 0)),
            pl.BlockSpec((1, 1, k), lambda l, j: (l, 0, 0)),
            pl.BlockSpec((1, k, tn), lambda l, j: (l, 0, j)),
        ],
        out_specs=pl.BlockSpec((1, rows, tn), lambda l, j: (l, 0, j)),
        out_shape=jax.ShapeDtypeStruct((layers, rows, n), F32),
        compiler_params=_cp(("parallel", "parallel"), 48),
        name="memory_kv",
    )(mem2, norm_mem.reshape(layers, 1, k), w_mem_kv)


def _pool_kernel(u_ref, w_ref, sc_ref, o_ref, pad_ref):
    grp = pl.program_id(1)
    s, c = u_ref.shape[1], u_ref.shape[2]
    rows = 256
    pad_ref[0:16, :] = jnp.zeros((16, c), F32)
    pad_ref[16:, :] = u_ref[0].astype(F32)
    wb = w_ref[...].astype(BF16)
    sc = sc_ref[0]

    for gi, win in enumerate(POOL_WINDOWS):
        @pl.when(grp == gi)
        def _(win=win):
            for r in range(s // rows):
                xh = pad_ref[r * rows:r * rows + rows + 16, :]
                acc = xh
                k = 1
                while k < win:
                    acc = acc + pltpu.roll(acc, k, axis=0)
                    k *= 2
                t = r * rows + lax.broadcasted_iota(jnp.int32, (rows, 1), 0)
                cnt = jnp.minimum(t + 1, win).astype(F32)
                mixed = (acc[16:, :] / cnt - xh[16:, :]).astype(BF16)
                o_ref[0, r * rows:(r + 1) * rows, :] = (
                    jnp.dot(mixed, wb, preferred_element_type=F32) * sc).astype(o_ref.dtype)


def pool_mixer(proj3, w_grp, li, scale):
    b, s, _ = proj3.shape
    c = POOL_GROUP_DIM
    return pl.pallas_call(
        _pool_kernel,
        grid=(b, POOL_GROUPS),
        in_specs=[
            pl.BlockSpec((1, s, c), lambda i, g: (i, 0, g)),
            pl.BlockSpec((None, None, c, c), lambda i, g: (li, g, 0, 0)),
            pl.BlockSpec((1, 1, c), lambda i, g: (g, 0, 0)),
        ],
        out_specs=pl.BlockSpec((1, s, c), lambda i, g: (i, 0, g)),
        out_shape=jax.ShapeDtypeStruct((b, s, MIX_WIDTH), BF16),
        scratch_shapes=[pltpu.VMEM((s + 16, c), F32)],
        compiler_params=_cp(("parallel", "parallel"), 40),
        name="pool_mixer",
    )(proj3, w_grp, scale.reshape(POOL_GROUPS, 1, c))


def _diff_attn_kernel(lam_ref, sub_ref, q_ref, k_ref, v_ref, o_ref, *, lambda_init):
    lam = lam_ref[...]
    s1 = jnp.sum(lam[0:1] * lam[1:2], axis=-1, keepdims=True)
    s2 = jnp.sum(lam[2:3] * lam[3:4], axis=-1, keepdims=True)
    lmbda = jnp.exp(s1) - jnp.exp(s2) + lambda_init
    s = q_ref.shape[1]
    tq = 512
    d = DIFF_HEAD_DIM
    diag_mask = (lax.broadcasted_iota(jnp.int32, (tq, tq), 1)
                 <= lax.broadcasted_iota(jnp.int32, (tq, tq), 0))
    for i in range(s // tq):
        lo = i * tq
        q = q_ref[0, lo:lo + tq, :]
        outs = []
        for c in range(2):
            cols = slice(c * d, (c + 1) * d)
            sd = lax.dot_general(q[:, cols], k_ref[0, lo:lo + tq, cols], NT_DIMS,
                                 preferred_element_type=F32)
            sd = jnp.where(diag_mask, sd, -jnp.inf)
            top = jnp.max(sd, axis=-1, keepdims=True)
            if lo:
                sp = lax.dot_general(q[:, cols], k_ref[0, 0:lo, cols], NT_DIMS,
                                     preferred_element_type=F32)
                top = jnp.maximum(top, jnp.max(sp, axis=-1, keepdims=True))
            ed = jnp.exp(sd - top)
            pv = jnp.dot(ed.astype(BF16), v_ref[0, lo:lo + tq, :], preferred_element_type=F32)
            den = jnp.sum(ed, axis=-1, keepdims=True)
            if lo:
                ep = jnp.exp(sp - top)
                pv = pv + jnp.dot(ep.astype(BF16), v_ref[0, 0:lo, :], preferred_element_type=F32)
                den = den + jnp.sum(ep, axis=-1, keepdims=True)
            outs.append(pv / den)
        o = outs[0] - lmbda * outs[1]
        ms = jnp.mean(o * o, axis=-1, keepdims=True)
        o_ref[0, i * tq:(i + 1) * tq, :] = (
            o * lax.rsqrt(ms + RMS_EPS) * sub_ref[...] * (1.0 - lambda_init)).astype(o_ref.dtype)


def diff_attention(proj3, lam, subln, lambda_init):
    b, s, _ = proj3.shape
    vd = DIFF_V_DIM
    nh = DIFF_HEADS
    return pl.pallas_call(
        functools.partial(_diff_attn_kernel, lambda_init=lambda_init),
        grid=(b, nh),
        in_specs=[
            pl.BlockSpec((4, DIFF_HEAD_DIM), lambda i, h: (0, 0)),
            pl.BlockSpec((1, vd), lambda i, h: (0, 0)),
            pl.BlockSpec((1, s, vd), lambda i, h: (i, 0, h)),
            pl.BlockSpec((1, s, vd), lambda i, h: (i, 0, nh + h)),
            pl.BlockSpec((1, s, vd), lambda i, h: (i, 0, 2 * nh + h)),
        ],
        out_specs=pl.BlockSpec((1, s, vd), lambda i, h: (i, 0, h)),
        out_shape=jax.ShapeDtypeStruct((b, s, MIX_WIDTH), BF16),
        compiler_params=_cp(("parallel", "parallel"), 48),
        name="diff_attention",
    )(lam, subln.reshape(1, vd), proj3, proj3, proj3)


CONV_HALO = 8
SSD_STEP_CHUNKS = 8


def _causal_conv_silu(u_ref, halo_ref, w_ref, b_ref):
    cur = u_ref[0]
    ext = jnp.concatenate([halo_ref[...], cur], axis=0)
    w = w_ref[...]
    y = ext * w[SSD_CONV - 1:SSD_CONV]
    for j in range(1, SSD_CONV):
        y = y + pltpu.roll(ext, j, axis=0) * w[SSD_CONV - 1 - j:SSD_CONV - j]
    halo_ref[...] = cur[cur.shape[0] - CONV_HALO:, :]
    y = y[CONV_HALO:, :] + b_ref[...]
    return y * _sigmoid(y)


def _dot3(a_f32, b_bf16):
    return sum(jnp.dot(t, b_bf16, preferred_element_type=F32) for t in _split3(a_f32))


def _ssd_kernel(x_ref, b_ref, c_ref, z_ref, dt_ref, wx_ref, wb_ref, wc_ref, cbx_ref, cbb_ref, cbc_ref,
                bias_ref, alog_ref, dx_ref, ng_ref, o_ref,
                state_ref, y_ref, xs_ref, hx_ref, hb_ref, hc_ref):
    @pl.when(pl.program_id(2) == 0)
    def _():
        state_ref[...] = jnp.zeros(state_ref.shape, F32)
        hx_ref[...] = jnp.zeros(hx_ref.shape, F32)
        hb_ref[...] = jnp.zeros(hb_ref.shape, F32)
        hc_ref[...] = jnp.zeros(hc_ref.shape, F32)

    ln = SSD_CHUNK
    hg = SSD_HEADS_PER_GROUP
    lanes = 2 * SSD_HEAD_DIM
    xs_ref[...] = _causal_conv_silu(x_ref, hx_ref, wx_ref, cbx_ref)
    bm_blk = _causal_conv_silu(b_ref, hb_ref, wb_ref, cbb_ref)
    cm_blk = _causal_conv_silu(c_ref, hc_ref, wc_ref, cbc_ref).astype(BF16)
    row = lax.broadcasted_iota(jnp.int32, (ln, ln), 0)
    col = lax.broadcasted_iota(jnp.int32, (ln, ln), 1)
    causal = col <= row
    ones_lower = jnp.where(causal, 1.0, 0.0).astype(BF16)
    sel_r = lax.broadcasted_iota(jnp.int32, (lanes, hg * lanes), 0)
    sel_c = lax.broadcasted_iota(jnp.int32, (lanes, hg * lanes), 1)
    spread = jnp.where(jnp.right_shift(sel_c, lanes.bit_length() - 1) == sel_r,
                       1.0, 0.0).astype(BF16)
    low_half = lax.broadcasted_iota(jnp.int32, (ln, lanes), 1) < SSD_HEAD_DIM
    dt_blk = _softplus(dt_ref[0] + bias_ref[0])
    neg_a = -jnp.exp(alog_ref[0])

    for cc in range(x_ref.shape[1] // ln):
        rows = slice(cc * ln, (cc + 1) * ln)
        bm = bm_blk[rows]
        bm_t = bm.T.astype(BF16)
        cm = cm_blk[rows]
        dt = dt_blk[rows]
        acs = sum(jnp.dot(ones_lower, t, preferred_element_type=F32) for t in _split3(dt * neg_a))
        acs_rows = acs.T
        dt_all = _dot3(dt, spread)
        acs_all = _dot3(acs, spread)
        scores = lax.dot_general(cm, bm.astype(BF16), NT_DIMS, preferred_element_type=F32)
        for k in range(hg // 2):
            heads = (2 * k, 2 * k + 1)
            tile = lambda a, h: a[:, h * lanes:(h + 1) * lanes]
            x = xs_ref[rows, k * lanes:(k + 1) * lanes]
            dt_p = jnp.where(low_half, tile(dt_all, heads[0]), tile(dt_all, heads[1]))
            acs_p = jnp.where(low_half, tile(acs_all, heads[0]), tile(acs_all, heads[1]))
            xc = x * dt_p
            prev = state_ref[k]
            y = jnp.dot(cm, prev.astype(BF16), preferred_element_type=F32) * jnp.exp(acs_p)
            for h, own in zip(heads, (low_half, ~low_half)):
                decay = jnp.exp(
                    jnp.where(causal, tile(acs_all, h) - acs_rows[h:h + 1, :], -jnp.inf))
                y = y + jnp.dot((scores * decay).astype(BF16),
                                jnp.where(own, xc, 0.0).astype(BF16), preferred_element_type=F32)
            a_last = acs_p[ln - 1:ln, :]
            st = jnp.dot(bm_t, (xc * jnp.exp(a_last - acs_p)).astype(BF16),
                         preferred_element_type=F32)
            state_ref[k] = prev * jnp.exp(a_last) + st
            y_ref[rows, k * lanes:(k + 1) * lanes] = y

    z = z_ref[0]
    yz = (y_ref[...] + dx_ref[0] * xs_ref[...]) * (z * _sigmoid(z))
    ms = jnp.mean(yz * yz, axis=-1, keepdims=True)
    o_ref[0] = (yz * lax.rsqrt(ms + RMS_EPS) * ng_ref[0]).astype(o_ref.dtype)


def ssd_scan(proj3, dt_blk, conv_w, conv_b, dt_bias, a_log, d_skip, norm_g):
    b, s, _ = proj3.shape
    ln, g, hg, n = SSD_CHUNK, SSD_GROUPS, SSD_HEADS_PER_GROUP, SSD_STATE
    gw = hg * SSD_HEAD_DIM
    lanes = 2 * SSD_HEAD_DIM
    assert ln == lanes and n == lanes

    def head_lanes(v):
        v = v.reshape(v.shape[:-1] + (g, hg))
        v = jnp.pad(v, [(0, 0)] * (v.ndim - 1) + [(0, lanes - hg)])
        return v.reshape(v.shape[:-2] + (g * lanes,))

    per_group = pl.BlockSpec((1, 1, lanes), lambda i, j, c: (j, 0, 0))
    d_chan = jnp.repeat(d_skip, SSD_HEAD_DIM)
    x_blk, b_blk = MIX_WIDTH // gw, (2 * MIX_WIDTH) // n
    c_blk = b_blk + g
    cw_b, cw_c = MIX_WIDTH // n, MIX_WIDTH // n + g
    conv_b2 = conv_b.reshape(1, SSD_CONV_DIM)
    rows = SSD_STEP_CHUNKS * ln
    return pl.pallas_call(
        _ssd_kernel,
        grid=(b, g, s // rows),
        in_specs=[
            pl.BlockSpec((1, rows, gw), lambda i, j, c: (i, c, x_blk + j)),
            pl.BlockSpec((1, rows, n), lambda i, j, c: (i, c, b_blk + j)),
            pl.BlockSpec((1, rows, n), lambda i, j, c: (i, c, c_blk + j)),
            pl.BlockSpec((1, rows, gw), lambda i, j, c: (i, c, j)),
            pl.BlockSpec((1, rows, lanes), lambda i, j, c: (i, c, dt_blk + j)),
            pl.BlockSpec((SSD_CONV, gw), lambda i, j, c: (0, j)),
            pl.BlockSpec((SSD_CONV, n), lambda i, j, c: (0, cw_b + j)),
            pl.BlockSpec((SSD_CONV, n), lambda i, j, c: (0, cw_c + j)),
            pl.BlockSpec((1, gw), lambda i, j, c: (0, j)),
            pl.BlockSpec((1, n), lambda i, j, c: (0, cw_b + j)),
            pl.BlockSpec((1, n), lambda i, j, c: (0, cw_c + j)),
            per_group, per_group,
            pl.BlockSpec((1, 1, gw), lambda i, j, c: (j, 0, 0)),
            pl.BlockSpec((1, 1, gw), lambda i, j, c: (j, 0, 0)),
        ],
        out_specs=pl.BlockSpec((1, rows, gw), lambda i, j, c: (i, c, j)),
        out_shape=jax.ShapeDtypeStruct((b, s, MIX_WIDTH), BF16),
        scratch_shapes=[pltpu.VMEM((hg // 2, n, lanes), F32), pltpu.VMEM((rows, gw), F32),
                        pltpu.VMEM((rows, gw), F32), pltpu.VMEM((CONV_HALO, gw), F32),
                        pltpu.VMEM((CONV_HALO, n), F32), pltpu.VMEM((CONV_HALO, n), F32)],
        compiler_params=_cp(("parallel", "parallel", "arbitrary"), 32),
        name="ssd_scan",
    )(proj3, proj3, proj3, proj3, proj3, conv_w, conv_w, conv_w,
      conv_b2, conv_b2, conv_b2, head_lanes(dt_bias).reshape(g, 1, lanes),
      head_lanes(a_log).reshape(g, 1, lanes), d_chan.reshape(g, 1, gw), norm_g.reshape(g, 1, gw))


def _memory_attention(q_ref, kv_ref, o_ref):
    d = MEM_HEAD_DIM
    for h in range(MEM_HEADS):
        q = (q_ref[0, :, h * d:(h + 1) * d].astype(F32) * (d ** -0.5)).astype(BF16)
        k = kv_ref[0, :, h * d:(h + 1) * d].astype(BF16)
        v = kv_ref[0, :, MEM_WIDTH + h * d:MEM_WIDTH + (h + 1) * d].astype(BF16)
        sc = lax.dot_general(q, k, NT_DIMS, preferred_element_type=F32)
        e = jnp.exp(sc - jnp.max(sc, axis=-1, keepdims=True))
        pr = (e / jnp.sum(e, axis=-1, keepdims=True)).astype(BF16)
        o_ref[:, h * d:(h + 1) * d] = jnp.dot(
            pr, v, preferred_element_type=F32).astype(o_ref.dtype)


def _outproj_kernel(x_ref, a_ref, q_ref, kv_ref, wa_ref, wm_ref, o_ref, mo_ref):
    @pl.when(pl.program_id(1) == 0)
    def _():
        _memory_attention(q_ref, kv_ref, mo_ref)

    acc = jnp.dot(a_ref[...], wa_ref[...].astype(BF16), preferred_element_type=F32)
    acc = acc + jnp.dot(mo_ref[...], wm_ref[...].astype(BF16), preferred_element_type=F32)
    o_ref[...] = x_ref[...] + acc


def out_projection(x, mix, proj3, q_block, mem_kv, w_out, li):
    m, d = x.shape
    b, s, _ = proj3.shape
    mem_len = mem_kv.shape[1]
    tm, tn = s, 512
    return pl.pallas_call(
        _outproj_kernel,
        grid=(m // tm, d // tn),
        in_specs=[
            pl.BlockSpec((tm, tn), lambda i, j: (i, j)),
            pl.BlockSpec((tm, MIX_WIDTH), lambda i, j: (i, 0)),
            pl.BlockSpec((1, s, MEM_WIDTH), lambda i, j: (i, 0, q_block)),
            pl.BlockSpec((1, mem_len, 2 * MEM_WIDTH), lambda i, j: (li * b + i, 0, 0)),
            pl.BlockSpec((None, MIX_WIDTH, tn), lambda i, j: (li, 0, j)),
            pl.BlockSpec((None, MEM_WIDTH, tn), lambda i, j: (li, MIX_WIDTH // MEM_WIDTH, j)),
        ],
        out_specs=pl.BlockSpec((tm, tn), lambda i, j: (i, j)),
        out_shape=jax.ShapeDtypeStruct((m, d), F32),
        scratch_shapes=[pltpu.VMEM((tm, MEM_WIDTH), BF16)],
        compiler_params=_cp(("parallel", "arbitrary"), 58),
        name="out_projection",
    )(x, mix, proj3, mem_kv, w_out, w_out)


def _router_kernel(x_ref, g_ref, wr_ref, br_ref, r_ref, seg_ref, t_ref):
    x = x_ref[...]
    ms = jnp.mean(x * x, axis=-1, keepdims=True)
    t = x * lax.rsqrt(ms + RMS_EPS) * g_ref[...]
    th = t.astype(BF16)
    t_ref[...] = th
    tl = (t - th.astype(F32)).astype(BF16)
    w = wr_ref[...]
    wh = w.astype(BF16)
    wl = (w - wh.astype(F32)).astype(BF16)
    lg = (lax.dot_general(wh, th, NT_DIMS, preferred_element_type=F32)
          + lax.dot_general(wh, tl, NT_DIMS, preferred_element_type=F32)
          + lax.dot_general(wl, th, NT_DIMS, preferred_element_type=F32)) + br_ref[...]
    ng, epg = N_EXPERT_GROUPS, EXPERTS_PER_GROUP
    gl = [lg[j:j + 1] for j in range(ng)]
    el = [lg[ng + j:ng + j + 1] for j in range(N_EXPERTS)]

    def first_argmax(vals):
        top = functools.reduce(jnp.maximum, vals)
        idx = jnp.full(top.shape, len(vals) - 1, jnp.int32)
        for j in range(len(vals) - 2, -1, -1):
            idx = jnp.where(vals[j] >= top, j, idx)
        return top, idx

    gmax, gsel = first_argmax(gl)
    g_w = 1.0 / functools.reduce(lambda a, b: a + b, [jnp.exp(v - gmax) for v in gl])
    e_in = []
    for j in range(epg):
        v = el[(ng - 1) * epg + j]
        for gi in range(ng - 2, -1, -1):
            v = jnp.where(gsel == gi, el[gi * epg + j], v)
        e_in.append(v)
    emax = functools.reduce(jnp.maximum, e_in)
    pe = [jnp.exp(v - emax) for v in e_in]
    se = functools.reduce(lambda a, b: a + b, pe)
    prob = [v / se for v in pe]
    v1, i1 = first_argmax(prob)
    rest = [jnp.where(i1 == j, -1.0, prob[j]) for j in range(epg)]
    v2, i2 = first_argmax(rest)
    tot = v1 + v2
    e1 = gsel * epg + i1
    e2 = gsel * epg + i2

    tm = x.shape[0]
    eidx = lax.broadcasted_iota(jnp.int32, (N_EXPERTS, tm), 0)
    hit1 = eidx == e1
    hit2 = eidx == e2
    onehot = jnp.where(hit1 | hit2, 1.0, 0.0)
    tok_r = lax.broadcasted_iota(jnp.int32, (tm, tm), 0)
    tok_c = lax.broadcasted_iota(jnp.int32, (tm, tm), 1)
    earlier = jnp.where(tok_r < tok_c, 1.0, 0.0).astype(BF16)
    rank = jnp.dot(onehot.astype(BF16), earlier, preferred_element_type=F32)
    cnt = jnp.sum(onehot, axis=1, keepdims=True).astype(jnp.int32)
    seg = jnp.bitwise_and(cnt + (SEG_ALIGN - 1), -SEG_ALIGN)
    ex_r = lax.broadcasted_iota(jnp.int32, (N_EXPERTS, N_EXPERTS), 0)
    ex_c = lax.broadcasted_iota(jnp.int32, (N_EXPERTS, N_EXPERTS), 1)
    lower = jnp.where(ex_c < ex_r, 1.0, 0.0).astype(BF16)
    seg_lanes = jnp.broadcast_to(seg.astype(F32), (N_EXPERTS, 128))
    start = jnp.dot(lower, seg_lanes.astype(BF16), preferred_element_type=F32)[:, 0:1]
    slot = start + rank
    lpos1 = jnp.sum(jnp.where(hit1, slot, 0.0), axis=0, keepdims=True)
    lpos2 = jnp.sum(jnp.where(hit2, slot, 0.0), axis=0, keepdims=True)
    r_ref[...] = jnp.concatenate(
        [e1.astype(F32), e2.astype(F32), v1 / tot * g_w, v2 / tot * g_w, lpos1, lpos2,
         jnp.zeros((2, tm), F32)], axis=0)
    seg_ref[0] = jnp.broadcast_to(seg, (N_EXPERTS, 128))


def moe_router(x, g, w_group, b_group, w_expert, b_expert):
    m, d = x.shape
    tm = MOE_SUB
    nr = 32
    wr = jnp.zeros((nr, d), F32).at[:N_EXPERT_GROUPS].set(w_group.T)
    wr = wr.at[N_EXPERT_GROUPS:N_EXPERT_GROUPS + N_EXPERTS].set(w_expert.T)
    br = jnp.zeros((nr, 1), F32).at[:N_EXPERT_GROUPS, 0].set(b_group)
    br = br.at[N_EXPERT_GROUPS:N_EXPERT_GROUPS + N_EXPERTS, 0].set(b_expert)
    route, seg3, t = pl.pallas_call(
        _router_kernel,
        grid=(m // tm,),
        in_specs=[
            pl.BlockSpec((tm, d), lambda i: (i, 0)),
            pl.BlockSpec((1, d), lambda i: (0, 0)),
            pl.BlockSpec((nr, d), lambda i: (0, 0)),
            pl.BlockSpec((nr, 1), lambda i: (0, 0)),
        ],
        out_specs=[pl.BlockSpec((8, tm), lambda i: (0, i)),
                   pl.BlockSpec((1, N_EXPERTS, 128), lambda i: (i, 0, 0)),
                   pl.BlockSpec((tm, d), lambda i: (i, 0))],
        out_shape=[jax.ShapeDtypeStruct((8, m), F32),
                   jax.ShapeDtypeStruct((m // tm, N_EXPERTS, 128), jnp.int32),
                   jax.ShapeDtypeStruct((m, d), BF16)],
        compiler_params=_cp(("parallel",), 40),
        name="moe_router",
    )(x, g.reshape(1, d), wr, br)
    return route, seg3[:, :, 0], t


def moe_plan(seg, m):
    nsub = m // MOE_SUB
    tm = MOE_TILE
    loc_off = jnp.cumsum(seg, axis=1) - seg
    reg_off = jnp.cumsum(seg, axis=0) - seg
    length = jnp.sum(seg, axis=0)
    padded = (length + tm - 1) // tm * tm
    e_end = jnp.cumsum(padded)
    e_start = e_end - padded
    glob_off = e_start[None, :] + reg_off
    n_slots = (2 * m + nsub * N_EXPERTS * (SEG_ALIGN - 1) + N_EXPERTS * (tm - 1) + tm - 1) // tm * tm
    tile_start = jnp.arange(n_slots // tm, dtype=jnp.int32) * tm
    tile_expert = jnp.minimum(
        jnp.sum((tile_start[:, None] >= e_end[None, :]).astype(jnp.int32), axis=1), N_EXPERTS - 1)
    ar = jnp.arange(N_EXPERTS, dtype=jnp.int32)
    later = (ar[None, :] > ar[:, None]) & (padded[None, :] > 0)
    nxt = jnp.min(jnp.where(later, ar[None, :], N_EXPERTS), axis=1)
    nxt = jnp.where(nxt == N_EXPERTS, -1, nxt).astype(jnp.int32)
    tile_next = jnp.sum(jnp.where(tile_expert[:, None] == ar[None, :], nxt[None, :], 0), axis=1)
    return dict(seg=seg.reshape(-1), loc_off=loc_off.reshape(-1), glob_off=glob_off.reshape(-1),
                n_slots=n_slots, tile_expert=tile_expert, tile_next=tile_next,
                n_used=(e_end[-1:] // tm).astype(jnp.int32),
                pad_start=jnp.concatenate([e_start + length, e_end[-1:]]).astype(jnp.int32),
                pad_len=jnp.concatenate([padded - length, n_slots - e_end[-1:]]).astype(jnp.int32))


def _aligned(v):
    return v if isinstance(v, int) else pl.multiple_of(v, SEG_ALIGN)


def _seg_copy(src_ref, dst_ref, sem, src_row, dst_row, n_rows):
    n_rows = _aligned(n_rows)
    return pltpu.make_async_copy(src_ref.at[pl.ds(_aligned(src_row), n_rows)],
                                 dst_ref.at[pl.ds(_aligned(dst_row), n_rows)], sem)


def _segment_copies(step, seg_ref, loc_ref, glob_ref, make):
    out = []
    for e in range(N_EXPERTS):
        k = step * N_EXPERTS + e
        out.append((seg_ref[k], make(loc_ref[k], glob_ref[k], seg_ref[k])))
    return out


def _start_all(copies):
    for n, cp in copies:
        @pl.when(n > 0)
        def _(cp=cp):
            cp.start()


def _wait_all(copies):
    for n, cp in copies:
        @pl.when(n > 0)
        def _(cp=cp):
            cp.wait()


def _dispatch_kernel(seg_ref, loc_ref, glob_ref, pst_ref, pln_ref, t_ref, rt_ref,
                     xs_ref, cbuf_ref, zbuf_ref, sems):
    s = pl.program_id(0)
    n_steps = pl.num_programs(0)
    buf = s % 2
    d = t_ref.shape[1]

    def copies(step, b):
        return _segment_copies(
            step, seg_ref, loc_ref, glob_ref,
            lambda loc, glob, n: _seg_copy(cbuf_ref.at[b], xs_ref, sems.at[b], loc, glob, n))

    @pl.when(s >= 2)
    def _():
        _wait_all(copies(s - 2, buf))

    t = t_ref[...]
    rt = rt_ref[...]
    lp = rt[4:6].astype(jnp.int32)
    slot = lax.broadcasted_iota(jnp.int32, (MOE_CBUF, MOE_SUB), 0)
    hit0 = slot == lp[0:1]
    hit1 = slot == lp[1:2]
    onehot = jnp.where(hit0 | hit1, 1.0, 0.0).astype(BF16)
    cbuf_ref[buf, :, 0:d] = jnp.dot(onehot, t, preferred_element_type=F32).astype(BF16)
    wslot = jnp.sum(jnp.where(hit0, rt[2:3], 0.0) + jnp.where(hit1, rt[3:4], 0.0),
                    axis=1, keepdims=True)
    w_hi = wslot.astype(BF16)
    w_lo = (wslot - w_hi.astype(F32)).astype(BF16)
    half = MOE_WCOLS // 2
    cbuf_ref[buf, :, d:d + half] = jnp.broadcast_to(w_hi, (MOE_CBUF, half))
    cbuf_ref[buf, :, d + half:] = jnp.broadcast_to(w_lo, (MOE_CBUF, half))
    _start_all(copies(s, buf))

    @pl.when(s == 0)
    def _():
        zsem = sems.at[2]
        zbuf_ref[...] = jnp.zeros(zbuf_ref.shape, BF16)
        pads = [(pln_ref[e], _seg_copy(zbuf_ref, xs_ref, zsem, 0, pst_ref[e], pln_ref[e]))
                for e in range(N_EXPERTS)]
        _start_all(pads)
        _wait_all(pads)
        tail_tiles = pln_ref[N_EXPERTS] // MOE_TILE

        def tail_copy(k):
            return _seg_copy(zbuf_ref, xs_ref, zsem, 0, pst_ref[N_EXPERTS] + k * MOE_TILE, MOE_TILE)

        def start(k, carry):
            tail_copy(k).start()
            return carry

        def wait(k, carry):
            tail_copy(k).wait()
            return carry
        lax.fori_loop(0, tail_tiles, start, 0)
        lax.fori_loop(0, tail_tiles, wait, 0)

    @pl.when(s == n_steps - 1)
    def _():
        @pl.when(s >= 1)
        def _():
            _wait_all(copies(s - 1, 1 - buf))
        _wait_all(copies(s, buf))


def moe_dispatch(t, route, plan):
    m, d = t.shape
    dw = d + MOE_WCOLS
    return pl.pallas_call(
        _dispatch_kernel,
        grid_spec=pltpu.PrefetchScalarGridSpec(
            num_scalar_prefetch=5,
            grid=(m // MOE_SUB,),
            in_specs=[
                pl.BlockSpec((MOE_SUB, d), lambda i, *_: (i, 0)),
                pl.BlockSpec((8, MOE_SUB), lambda i, *_: (0, i)),
            ],
            out_specs=pl.BlockSpec(memory_space=pl.ANY),
            scratch_shapes=[pltpu.VMEM((2, MOE_CBUF, dw), BF16), pltpu.VMEM((MOE_TILE, dw), BF16),
                            pltpu.SemaphoreType.DMA((3,))],
        ),
        out_shape=jax.ShapeDtypeStruct((plan["n_slots"], dw), BF16),
        compiler_params=pltpu.CompilerParams(dimension_semantics=("arbitrary",),
                                             vmem_limit_bytes=40 * MIB, has_side_effects=True),
        name="moe_dispatch",
    )(plan["seg"], plan["loc_off"], plan["glob_off"], plan["pad_start"], plan["pad_len"],
      t, route)


def _ffn_kernel(te_ref, nx_ref, nu_ref, x_ref, wg_hbm, wu_hbm, wd_hbm, o_ref,
                wgs_ref, wus_ref, wds_ref, wgb_ref, wub_ref, wdb_ref, sem, *, li):
    i = pl.program_id(0)
    d = o_ref.shape[1]

    def weight_copies(e):
        return [pltpu.make_async_copy(wg_hbm.at[li, e], wgs_ref, sem),
                pltpu.make_async_copy(wu_hbm.at[li, e], wus_ref, sem),
                pltpu.make_async_copy(wd_hbm.at[li, e], wds_ref, sem)]

    @pl.when(i < nu_ref[0])
    def _():
        e = te_ref[i]

        @pl.when(i == 0)
        def _():
            for cp in weight_copies(e):
                cp.start()

        @pl.when((i == 0) | (e != te_ref[jnp.maximum(i - 1, 0)]))
        def _():
            for cp in weight_copies(e):
                cp.wait()
            wgb_ref[...] = wgs_ref[...].astype(BF16)
            wub_ref[...] = wus_ref[...].astype(BF16)
            wdb_ref[...] = wds_ref[...].astype(BF16)

            @pl.when(nx_ref[i] >= 0)
            def _():
                for cp in weight_copies(nx_ref[i]):
                    cp.start()

        x = x_ref[:, 0:d]
        half = MOE_WCOLS // 2
        w = x_ref[:, d:d + 1].astype(F32) + x_ref[:, d + half:d + half + 1].astype(F32)
        gate = jnp.dot(x, wgb_ref[...], preferred_element_type=F32)
        up = jnp.dot(x, wub_ref[...], preferred_element_type=F32)
        hid = (gate * _sigmoid(gate) * up * w).astype(BF16)
        o_ref[...] = jnp.dot(hid, wdb_ref[...], preferred_element_type=F32).astype(o_ref.dtype)

    @pl.when(i >= nu_ref[0])
    def _():
        o_ref[...] = jnp.zeros(o_ref.shape, o_ref.dtype)


def grouped_ffn(xs, plan, w_gate, w_up, w_down, li):
    p, dw = xs.shape
    d = dw - MOE_WCOLS
    f = D_EXPERT
    tm = MOE_TILE
    hbm = pl.BlockSpec(memory_space=pl.ANY)
    return pl.pallas_call(
        functools.partial(_ffn_kernel, li=li),
        grid_spec=pltpu.PrefetchScalarGridSpec(
            num_scalar_prefetch=3,
            grid=(p // tm,),
            in_specs=[
                pl.BlockSpec((tm, dw), lambda i, te, nx, nu: (jnp.minimum(i, nu[0] - 1), 0)),
                hbm, hbm, hbm,
            ],
            out_specs=pl.BlockSpec((tm, d), lambda i, te, nx, nu: (i, 0)),
            scratch_shapes=[pltpu.VMEM((d, f), F32), pltpu.VMEM((d, f), F32), pltpu.VMEM((f, d), F32),
                            pltpu.VMEM((d, f), BF16), pltpu.VMEM((d, f), BF16),
                            pltpu.VMEM((f, d), BF16), pltpu.SemaphoreType.DMA(())],
        ),
        out_shape=jax.ShapeDtypeStruct((p, d), BF16),
        compiler_params=_cp(("arbitrary",), 48),
        name="grouped_ffn",
    )(plan["tile_expert"], plan["tile_next"], plan["n_used"], xs, w_gate, w_up, w_down)


def _combine_kernel(seg_ref, loc_ref, glob_ref, x_ref, lp_ref, ys_ref, g_ref, *rest, final_norm):
    if final_norm:
        o_ref, ybuf_ref, sems = rest
    else:
        o_ref, h_ref, ybuf_ref, sems = rest
    s = pl.program_id(0)
    n_steps = pl.num_programs(0)
    buf = s % 2

    def copies(step, b):
        return _segment_copies(
            step, seg_ref, loc_ref, glob_ref,
            lambda loc, glob, n: _seg_copy(ys_ref, ybuf_ref.at[b], sems.at[b], glob, loc, n))

    def fetch(step, b):
        ybuf_ref[b] = jnp.zeros(ybuf_ref.shape[1:], BF16)
        _start_all(copies(step, b))

    @pl.when(s == 0)
    def _():
        fetch(s, buf)

    @pl.when(s + 1 < n_steps)
    def _():
        fetch(s + 1, 1 - buf)

    _wait_all(copies(s, buf))
    lp = lp_ref[...]
    slot = lax.broadcasted_iota(jnp.int32, (MOE_SUB, MOE_CBUF), 1)
    onehot = jnp.where((slot == lp[:, 0:1]) | (slot == lp[:, 1:2]), 1.0, 0.0).astype(BF16)
    out = x_ref[...] + jnp.dot(onehot, ybuf_ref[buf], preferred_element_type=F32)
    ms = jnp.mean(out * out, axis=-1, keepdims=True)
    normed = out * lax.rsqrt(ms + RMS_EPS) * g_ref[...]
    if final_norm:
        o_ref[...] = normed
    else:
        o_ref[...] = out
        h_ref[...] = normed.astype(BF16)


def moe_combine(x, ys, route, plan, gain, final_norm):
    m, d = x.shape
    row = pl.BlockSpec((MOE_SUB, d), lambda i, *_: (i, 0))
    out_specs, out_shape = row, jax.ShapeDtypeStruct((m, d), F32)
    if not final_norm:
        out_specs, out_shape = [row, row], [out_shape, jax.ShapeDtypeStruct((m, d), BF16)]
    return pl.pallas_call(
        functools.partial(_combine_kernel, final_norm=final_norm),
        grid_spec=pltpu.PrefetchScalarGridSpec(
            num_scalar_prefetch=3,
            grid=(m // MOE_SUB,),
            in_specs=[
                row,
                pl.BlockSpec((MOE_SUB, 2), lambda i, *_: (i, 0)),
                pl.BlockSpec(memory_space=pl.ANY),
                pl.BlockSpec((1, d), lambda i, *_: (0, 0)),
            ],
            out_specs=out_specs,
            scratch_shapes=[pltpu.VMEM((2, MOE_CBUF, d), BF16), pltpu.SemaphoreType.DMA((2,))],
        ),
        out_shape=out_shape,
        compiler_params=pltpu.CompilerParams(dimension_semantics=("arbitrary",),
                                             vmem_limit_bytes=40 * MIB),
        name="moe_combine",
    )(plan["seg"], plan["loc_off"], plan["glob_off"], x, route[4:6].T.astype(jnp.int32), ys,
      gain.reshape(1, d))


def hier_moe_layer(x, norm_g, w_group, b_group, w_expert, b_expert, w_gate, w_up, w_down, li,
                   out_gain, final_norm):
    m, _ = x.shape
    route, seg, t = moe_router(x, norm_g, w_group, b_group, w_expert, b_expert)
    plan = moe_plan(seg, m)
    xs = moe_dispatch(t, route, plan)
    ys = grouped_ffn(xs, plan, w_gate, w_up, w_down, li)
    return moe_combine(x, ys, route, plan, out_gain, final_norm)


def _rope_tables(seq, dim):
    pos = jnp.arange(seq, dtype=F32)
    inv = ROPE_THETA ** (-jnp.arange(0, dim, 2, dtype=F32) / dim)
    ang = pos[:, None] * inv[None, :]
    ang = jnp.concatenate([ang, ang], axis=-1)
    sign = jnp.concatenate([-jnp.ones((dim // 2,), F32), jnp.ones((dim // 2,), F32)])
    return jnp.cos(ang), jnp.sin(ang) * sign[None, :]


def _lambda_init(depth_idx):
    return 0.8 - 0.6 * math.exp(-0.3 * depth_idx)


def kernel(x, mem, norm_mix, norm_mem, norm_ffn, norm_final, w_out, w_mem_kv, pool_w_in, pool_w_grp, pool_scale, diff_w_in, diff_lambda, diff_subln, ssd_w_in, ssd_conv_w, ssd_conv_b, ssd_dt_bias, ssd_a_log, ssd_d, ssd_norm, moe_w_group, moe_b_group, moe_w_expert, moe_b_expert, moe_w_gate, moe_w_up, moe_w_down):
    b, s, d = x.shape
    m = b * s
    mem_len = mem.shape[1]
    cos, sin_signed = _rope_tables(s, DIFF_HEAD_DIM)
    xt = x.reshape(m, d)
    mem_kv_all = memory_kv(mem.reshape(b * mem_len, d), norm_mem, w_mem_kv)
    mem_kv_all = mem_kv_all.reshape(DEPTH * b, mem_len, 2 * MEM_WIDTH)
    for i in range(DEPTH):
        kind, slot = i % N_MIXERS, i // N_MIXERS
        if i == 0:
            lhs, gain, tn = xt, norm_mix[i], IN_TN
        else:
            lhs, gain, tn = h_next, None, IN_TN_PRENORMED
        if kind == 0:
            n_in = MIX_WIDTH + MEM_WIDTH
            proj = norm_matmul(lhs, gain, pool_w_in, slot, n_in, IN_TM, tn, BF16)
            proj = proj.reshape(b, s, n_in)
            mix = pool_mixer(proj, pool_w_grp, slot, pool_scale[slot])
        elif kind == 1:
            n_in = 3 * MIX_WIDTH + MEM_WIDTH
            proj = norm_matmul(lhs, gain, diff_w_in, slot, n_in, IN_TM, tn, BF16,
                               rope=(cos, sin_signed, MIX_WIDTH, MIX_WIDTH, s))
            proj = proj.reshape(b, s, n_in)
            mix = diff_attention(proj, diff_lambda[slot], diff_subln[slot], _lambda_init(i))
        else:
            n_main = MIX_WIDTH + SSD_CONV_DIM
            lanes = 2 * SSD_HEAD_DIM
            w_dt = ssd_w_in[slot, :, n_main:SSD_MIX_IN].reshape(d, SSD_GROUPS, SSD_HEADS_PER_GROUP)
            w_dt = jnp.pad(w_dt, ((0, 0), (0, 0), (0, lanes - SSD_HEADS_PER_GROUP)))
            w_tail = jnp.concatenate(
                [ssd_w_in[slot, :, SSD_MIX_IN:], w_dt.reshape(d, SSD_GROUPS * lanes)], axis=1)
            n_in = n_main + SSD_TAIL
            proj = norm_matmul(lhs, gain, ssd_w_in, slot, n_main, IN_TM, tn, F32, tail=w_tail)
            proj = proj.reshape(b, s, n_in)
            mix = ssd_scan(proj, (n_main + MEM_WIDTH) // lanes, ssd_conv_w[slot], ssd_conv_b[slot],
                           ssd_dt_bias[slot], ssd_a_log[slot], ssd_d[slot], ssd_norm[slot])
        q_block = (MIX_WIDTH if kind == 0 else 3 * MIX_WIDTH if kind == 1
                   else MIX_WIDTH + SSD_CONV_DIM) // MEM_WIDTH
        xt = out_projection(xt, mix.reshape(m, MIX_WIDTH), proj, q_block, mem_kv_all, w_out, i)
        last = i == DEPTH - 1
        res = hier_moe_layer(xt, norm_ffn[i], moe_w_group[i], moe_b_group[i], moe_w_expert[i],
                             moe_b_expert[i], moe_w_gate, moe_w_up, moe_w_down, i,
                             norm_final if last else norm_mix[i + 1], last)
        if last:
            xt = res
        else:
            xt, h_next = res
    return xt.reshape(b, s, d)
```
